```python
import jax, jax.numpy as jnp
from jax import lax
import numpy as np

D_MODEL = 1024
BATCH = 4
SEQ = 4096
DEPTH = 1
DEC_BATCH = 4
DEC_SEQ = 8192
PAST_LEN = 128

MLA_HEADS = 8
MLA_NOPE_DIM = 64
MLA_ROPE_DIM = 32
MLA_V_DIM = 64
Q_LORA_RANK = 384
KV_LORA_RANK = 256
MLA_QK_DIM = MLA_NOPE_DIM + MLA_ROPE_DIM
MLA_WIDTH = MLA_HEADS * MLA_V_DIM
Q_BLOCK = 128
RET_HEADS = 8
RET_QK_DIM = 64
RET_V_DIM = 128
RET_CHUNK = 128
RET_QK_WIDTH = RET_HEADS * RET_QK_DIM
RET_V_WIDTH = RET_HEADS * RET_V_DIM
N_EXPERTS = 16
EXPERT_FF = 2816
CAPACITY_FACTOR = 2
ROPE_THETA = 10000.0
EPS = 1e-6
GN_EPS = 1e-5
IN_SPLITS = (Q_LORA_RANK, KV_LORA_RANK, MLA_ROPE_DIM, RET_QK_WIDTH, RET_QK_WIDTH, RET_V_WIDTH, RET_V_WIDTH, D_MODEL, D_MODEL)
IN_WIDTH = Q_LORA_RANK + KV_LORA_RANK + MLA_ROPE_DIM + 2 * RET_QK_WIDTH + 2 * RET_V_WIDTH + 2 * D_MODEL

kernel_name = 'hybrid_mla_retention_ec_encoder'


def rms_norm(x, g):
    x32 = x.astype(jnp.float32)
    y = x32 * lax.rsqrt(jnp.mean(x32 * x32, axis=-1, keepdims=True) + EPS)
    return (y * g.astype(jnp.float32)).astype(x.dtype)


def rope_tables(seq, dim):
    inv = 1.0 / (ROPE_THETA ** (jnp.arange(0, dim, 2, dtype=jnp.float32) / dim))
    ang = jnp.arange(seq, dtype=jnp.float32)[:, None] * inv[None, :]
    return jnp.cos(ang), jnp.sin(ang)


def apply_rope(x, cos, sin):
    extra = x.ndim - 3
    shp = cos.shape[:1] + (1,) * extra + cos.shape[1:]
    c = cos.reshape(shp).astype(x.dtype)
    s = sin.reshape(shp).astype(x.dtype)
    x1, x2 = jnp.split(x, 2, axis=-1)
    return jnp.concatenate([x1 * c - x2 * s, x1 * s + x2 * c], axis=-1)


def mla_branch(c_q, c_kv, k_r, cos, sin, q_norm_g, w_uq, kv_norm_g, w_ukv):
    b, s, _ = c_q.shape
    q = (rms_norm(c_q, q_norm_g) @ w_uq).reshape(b, s, MLA_HEADS, MLA_QK_DIM)
    q_nope, q_rope = jnp.split(q, [MLA_NOPE_DIM], axis=-1)
    q_rope = apply_rope(q_rope, cos, sin)
    kv = (rms_norm(c_kv, kv_norm_g) @ w_ukv).reshape(b, s, MLA_HEADS, MLA_NOPE_DIM + MLA_V_DIM)
    k_nope, v = jnp.split(kv, [MLA_NOPE_DIM], axis=-1)
    k_rope = apply_rope(k_r, cos, sin)
    scale = MLA_QK_DIM ** -0.5
    nblk = s // Q_BLOCK
    qn_blk = q_nope.reshape(b, nblk, Q_BLOCK, MLA_HEADS, MLA_NOPE_DIM).transpose(1, 0, 2, 3, 4)
    qr_blk = q_rope.reshape(b, nblk, Q_BLOCK, MLA_HEADS, MLA_ROPE_DIM).transpose(1, 0, 2, 3, 4)

    def attend(blk):
        qn, qr = blk
        sc = jnp.einsum('bqhd,bkhd->bhqk', qn, k_nope) + jnp.einsum('bqhr,bkr->bhqk', qr, k_rope)
        p = jax.nn.softmax(sc.astype(jnp.float32) * scale, axis=-1).astype(v.dtype)
        return jnp.einsum('bhqk,bkhd->bqhd', p, v)

    o = lax.map(attend, (qn_blk, qr_blk))
    return o.transpose(1, 0, 2, 3, 4).reshape(b, s, MLA_WIDTH)


def retention_scan(q, k, v, log_gamma, strict):
    b, s = q.shape[:2]
    dt = q.dtype
    C = RET_CHUNK
    n = s // C
    qc = q.reshape(b, n, C, RET_HEADS, RET_QK_DIM)
    kc = k.reshape(b, n, C, RET_HEADS, RET_QK_DIM)
    vc = v.reshape(b, n, C, RET_HEADS, RET_V_DIM)
    pos = jnp.arange(C, dtype=jnp.float32)
    diff = pos[:, None] - pos[None, :]
    mask = (diff > 0) if strict else (diff >= 0)
    decay = jnp.where(mask[None], jnp.exp(jnp.maximum(diff, 0.0)[None] * log_gamma[:, None, None]), 0.0)
    inner = jnp.einsum('bnihd,bnjhd->bnhij', qc, kc) * decay.astype(dt)
    y_inner = jnp.einsum('bnhij,bnjhe->bnihe', inner, vc)
    zeta = jnp.exp((C - 1 - pos)[:, None] * log_gamma[None, :]).astype(dt)
    xi = jnp.exp((pos + 1)[:, None] * log_gamma[None, :]).astype(dt)
    chunk_state = jnp.einsum('bnjhd,bnjhe->nbhde', kc * zeta[:, :, None], vc)
    g_chunk = jnp.exp(C * log_gamma)[None, :, None, None].astype(chunk_state.dtype)

    def step(R, S_i):
        return g_chunk * R + S_i, R

    R0 = jnp.zeros(chunk_state.shape[1:], chunk_state.dtype)
    _, R_prev = lax.scan(step, R0, chunk_state)
    y_cross = jnp.einsum('bnihd,nbhde->bnihe', qc * xi[:, :, None], R_prev)
    return (y_inner + y_cross).reshape(b, s, RET_HEADS, RET_V_DIM)


def retention_branch(rq, rk, rv, rg, cos, sin, decay_fwd, decay_bwd, gn_g):
    b, s, _ = rq.shape
    q = apply_rope(rq.reshape(b, s, RET_HEADS, RET_QK_DIM), cos, sin)
    k = apply_rope(rk.reshape(b, s, RET_HEADS, RET_QK_DIM), cos, sin) * (RET_QK_DIM ** -0.5)
    v = rv.reshape(b, s, RET_HEADS, RET_V_DIM)
    lg_f = jax.nn.log_sigmoid(decay_fwd.astype(jnp.float32))
    lg_b = jax.nn.log_sigmoid(decay_bwd.astype(jnp.float32))
    y_f = retention_scan(q, k, v, lg_f, False)
    y_b = jnp.flip(retention_scan(jnp.flip(q, 1), jnp.flip(k, 1), jnp.flip(v, 1), lg_b, True), 1)
    y = (y_f + y_b).astype(jnp.float32)
    mu = jnp.mean(y, axis=-1, keepdims=True)
    var = jnp.mean(jnp.square(y - mu), axis=-1, keepdims=True)
    y = ((y - mu) * lax.rsqrt(var + GN_EPS)).reshape(b, s, RET_V_WIDTH) * gn_g.astype(jnp.float32)
    return jax.nn.silu(rg) * y.astype(rg.dtype)


def expert_choice_ffn(x, w_router, w_gate, w_up, w_down):
    b, s, d = x.shape
    n = b * s
    cap = CAPACITY_FACTOR * n // N_EXPERTS
    xt = x.reshape(n, d)
    aff = jax.nn.softmax((xt @ w_router).astype(jnp.float32), axis=-1)
    g, idx = lax.top_k(aff.T, cap)
    xe = xt[idx]
    h = jax.nn.silu(jnp.einsum('ecd,edf->ecf', xe, w_gate)) * jnp.einsum('ecd,edf->ecf', xe, w_up)
    ye = jnp.einsum('ecf,efd->ecd', h, w_down) * g[..., None].astype(x.dtype)
    out = jnp.zeros_like(xt).at[idx.reshape(-1)].add(ye.reshape(-1, d))
    return out.reshape(b, s, d)


def encoder_layer(x, cos_m, sin_m, cos_r, sin_r, norm_mix_g, w_in, q_norm_g, w_uq, kv_norm_g, w_ukv,
                  ret_decay_fwd, ret_decay_bwd, ret_gn_g, w_branch_a, w_branch_b, w_out,
                  norm_ffn_g, w_router, w_exp_gate, w_exp_up, w_exp_down):
    h = rms_norm(x, norm_mix_g)
    z = h @ w_in
    offs = np.cumsum(np.array(IN_SPLITS))[:-1].tolist()
    c_q, c_kv, k_r, rq, rk, rv, rg, g_a, g_b = jnp.split(z, offs, axis=-1)
    a = mla_branch(c_q, c_kv, k_r, cos_m, sin_m, q_norm_g, w_uq, kv_norm_g, w_ukv) @ w_branch_a
    r = retention_branch(rq, rk, rv, rg, cos_r, sin_r, ret_decay_fwd, ret_decay_bwd, ret_gn_g) @ w_branch_b
    mixed = jax.nn.sigmoid(g_a) * a + jax.nn.sigmoid(g_b) * r
    x = x + mixed @ w_out
    x = x + expert_choice_ffn(rms_norm(x, norm_ffn_g), w_router, w_exp_gate, w_exp_up, w_exp_down)
    return x


def encode(x, norm_mix_g, w_in, q_norm_g, w_uq, kv_norm_g, w_ukv, ret_decay_fwd, ret_decay_bwd, ret_gn_g,
           w_branch_a, w_branch_b, w_out, norm_ffn_g, w_router, w_exp_gate, w_exp_up, w_exp_down, norm_final_g):
    s = x.shape[1]
    cos_m, sin_m = rope_tables(s, MLA_ROPE_DIM)
    cos_r, sin_r = rope_tables(s, RET_QK_DIM)
    for l in range(DEPTH):
        x = encoder_layer(x, cos_m, sin_m, cos_r, sin_r, norm_mix_g[l], w_in[l], q_norm_g[l], w_uq[l],
                          kv_norm_g[l], w_ukv[l], ret_decay_fwd[l], ret_decay_bwd[l], ret_gn_g[l],
                          w_branch_a[l], w_branch_b[l], w_out[l], norm_ffn_g[l], w_router[l],
                          w_exp_gate[l], w_exp_up[l], w_exp_down[l])
    return rms_norm(x, norm_final_g)


def setup_inputs(seed: int = 0) -> dict:
    key = jax.random.key(seed)
    ks = jax.random.split(key, 20)
    f32 = jnp.float32

    def w(k, shape, fan_in):
        return jax.random.normal(k, shape, f32) * (fan_in ** -0.5)

    def gain(k, shape):
        return 1.0 + 0.01 * jax.random.normal(k, shape, f32)

    base_decay = jnp.log(2.0 ** (5.0 + jnp.arange(RET_HEADS, dtype=f32)) - 1.0)
    return {
        'x_prompt': jax.random.normal(ks[0], (BATCH, SEQ, D_MODEL), f32),
        'x_sample': jax.random.normal(ks[1], (DEC_BATCH, DEC_SEQ, D_MODEL), f32),
        'norm_mix_g': gain(ks[2], (DEPTH, D_MODEL)),
        'w_in': w(ks[3], (DEPTH, D_MODEL, IN_WIDTH), D_MODEL),
        'q_norm_g': gain(ks[4], (DEPTH, Q_LORA_RANK)),
        'w_uq': w(ks[5], (DEPTH, Q_LORA_RANK, MLA_HEADS * MLA_QK_DIM), Q_LORA_RANK),
        'kv_norm_g': gain(ks[6], (DEPTH, KV_LORA_RANK)),
        'w_ukv': w(ks[7], (DEPTH, KV_LORA_RANK, MLA_HEADS * (MLA_NOPE_DIM + MLA_V_DIM)), KV_LORA_RANK),
        'ret_decay_fwd': base_decay[None] + 0.05 * jax.random.normal(ks[8], (DEPTH, RET_HEADS), f32),
        'ret_decay_bwd': base_decay[None] + 0.05 * jax.random.normal(ks[9], (DEPTH, RET_HEADS), f32),
        'ret_gn_g': gain(ks[10], (DEPTH, RET_V_WIDTH)),
        'w_branch_a': w(ks[11], (DEPTH, MLA_WIDTH, D_MODEL), MLA_WIDTH),
        'w_branch_b': w(ks[12], (DEPTH, RET_V_WIDTH, D_MODEL), RET_V_WIDTH),
        'w_out': w(ks[13], (DEPTH, D_MODEL, D_MODEL), D_MODEL),
        'norm_ffn_g': gain(ks[14], (DEPTH, D_MODEL)),
        'w_router': w(ks[15], (DEPTH, D_MODEL, N_EXPERTS), D_MODEL),
        'w_exp_gate': w(ks[16], (DEPTH, N_EXPERTS, D_MODEL, EXPERT_FF), D_MODEL),
        'w_exp_up': w(ks[17], (DEPTH, N_EXPERTS, D_MODEL, EXPERT_FF), D_MODEL),
        'w_exp_down': w(ks[18], (DEPTH, N_EXPERTS, EXPERT_FF, D_MODEL), EXPERT_FF),
        'norm_final_g': gain(ks[19], (D_MODEL,)),
    }


def reference(x_prompt, x_sample, norm_mix_g, w_in, q_norm_g, w_uq, kv_norm_g, w_ukv, ret_decay_fwd, ret_decay_bwd,
              ret_gn_g, w_branch_a, w_branch_b, w_out, norm_ffn_g, w_router, w_exp_gate, w_exp_up, w_exp_down,
              norm_final_g):
    y_prompt = encode(x_prompt, norm_mix_g, w_in, q_norm_g, w_uq, kv_norm_g, w_ukv, ret_decay_fwd, ret_decay_bwd,
                      ret_gn_g, w_branch_a, w_branch_b, w_out, norm_ffn_g, w_router, w_exp_gate, w_exp_up,
                      w_exp_down, norm_final_g)
    y_sample = encode(x_sample, norm_mix_g, w_in, q_norm_g, w_uq, kv_norm_g, w_ukv, ret_decay_fwd, ret_decay_bwd,
                      ret_gn_g, w_branch_a, w_branch_b, w_out, norm_ffn_g, w_router, w_exp_gate, w_exp_up,
                      w_exp_down, norm_final_g)
    return (y_prompt, y_sample)
```

```python
import functools

import jax
import jax.numpy as jnp
from jax import lax
from jax.experimental import pallas as pl
from jax.experimental.pallas import tpu as pltpu

D_MODEL = 1024
MLA_HEADS = 8
MLA_NOPE_DIM = 64
MLA_ROPE_DIM = 32
MLA_V_DIM = 64
MLA_QK_DIM = MLA_NOPE_DIM + MLA_ROPE_DIM
Q_LORA_RANK = 384
KV_LORA_RANK = 256
RET_HEADS = 8
RET_QK_DIM = 64
RET_V_DIM = 128
RET_QK_WIDTH = RET_HEADS * RET_QK_DIM
RET_V_WIDTH = RET_HEADS * RET_V_DIM
N_EXPERTS = 16
EXPERT_FF = 2816
CAPACITY_FACTOR = 2
ROPE_THETA = 10000.0
EPS = 1e-6
GN_EPS = 1e-5
IN_SPLITS = (Q_LORA_RANK, KV_LORA_RANK, MLA_ROPE_DIM, RET_QK_WIDTH, RET_QK_WIDTH, RET_V_WIDTH, RET_V_WIDTH,
             D_MODEL, D_MODEL)

LANES = 128
HEAD_PAD = LANES
MLA_PAD_WIDTH = MLA_HEADS * HEAD_PAD
VMEM_LIMIT_BYTES = 56 * 1024 * 1024

TOKEN_TILE = 256
ATTN_Q_TILE = 512
ATTN_K_TILE = 512
RET_CHUNK = 256
FFN_ROW_TILE = 256

F32 = jnp.float32
BF16 = jnp.bfloat16


def _dot(a, b):
    return jnp.dot(a, b, preferred_element_type=F32)


def _dot_nt(a, b):
    return lax.dot_general(a, b, (((1,), (1,)), ((), ())), preferred_element_type=F32)


def _dot_tn(a, b):
    return lax.dot_general(a, b, (((0,), (0,)), ((), ())), preferred_element_type=F32)


def _rms(x, g):
    return x * lax.rsqrt(jnp.mean(x * x, axis=-1, keepdims=True) + EPS) * g


def _sigmoid(x):
    return 1.0 / (1.0 + jnp.exp(-x))


def _rope_lanes(blk, cos, sin, half):
    lane = lax.broadcasted_iota(jnp.int32, blk.shape, 1)
    upper = (lane % (2 * half)) >= half
    partner = jnp.where(upper, pltpu.roll(blk, half, 1), pltpu.roll(blk, LANES - half, 1))
    return blk * cos + partner * sin


def _const_spec(shape):
    nd = len(shape)
    return pl.BlockSpec(shape, lambda *_: (0,) * nd, pipeline_mode=pl.Buffered(1))


def _pre_kernel(x_ref, gmix_ref, wcq_ref, wckv_ref, wkr_ref, wrq_ref, wrk_ref, wrv_ref, wrg_ref, wga_ref, wgb_ref,
                gq_ref, wq_ref, gkv_ref, wk_ref, place_ref, wv_ref,
                cosq_ref, sinq_ref, cosk_ref, sink_ref, cosrq_ref, sinrq_ref, cosrk_ref, sinrk_ref,
                q_out, k_out, v_out, rq_out, rk_out, rv_out, rg_out, ga_out, gb_out):
    h = _rms(x_ref[...], gmix_ref[...]).astype(BF16)

    qn = _rms(_dot(h, wcq_ref[...]), gq_ref[...]).astype(BF16)
    q = _dot(qn, wq_ref[...])
    cosq, sinq = cosq_ref[...], sinq_ref[...]
    for j in range(MLA_HEADS):
        sl = slice(HEAD_PAD * j, HEAD_PAD * (j + 1))
        q_out[:, sl] = _rope_lanes(q[:, sl], cosq, sinq, MLA_ROPE_DIM // 2).astype(BF16)

    kvn = _rms(_dot(h, wckv_ref[...]), gkv_ref[...]).astype(BF16)
    kr = _dot(h, wkr_ref[...]).astype(BF16)
    k = _dot(kvn, wk_ref[...]) + _dot(kr, place_ref[...])
    cosk, sink = cosk_ref[...], sink_ref[...]
    for j in range(MLA_HEADS):
        sl = slice(HEAD_PAD * j, HEAD_PAD * (j + 1))
        k_out[:, sl] = _rope_lanes(k[:, sl], cosk, sink, MLA_ROPE_DIM // 2).astype(BF16)
    v_out[...] = _dot(kvn, wv_ref[...]).astype(BF16)

    rq = _dot(h, wrq_ref[...])
    rk = _dot(h, wrk_ref[...])
    cosrq, sinrq, cosrk, sinrk = cosrq_ref[...], sinrq_ref[...], cosrk_ref[...], sinrk_ref[...]
    for j in range(RET_QK_WIDTH // LANES):
        sl = slice(LANES * j, LANES * (j + 1))
        rq_out[:, sl] = _rope_lanes(rq[:, sl], cosrq, sinrq, RET_QK_DIM // 2).astype(BF16)
        rk_out[:, sl] = _rope_lanes(rk[:, sl], cosrk, sinrk, RET_QK_DIM // 2).astype(BF16)

    rv_out[...] = _dot(h, wrv_ref[...]).astype(BF16)
    rg_out[...] = _dot(h, wrg_ref[...]).astype(BF16)
    ga_out[...] = _dot(h, wga_ref[...]).astype(BF16)
    gb_out[...] = _dot(h, wgb_ref[...]).astype(BF16)


def _pre_call(x2d, seq, weights, tables):
    n = x2d.shape[0]
    tm = TOKEN_TILE
    tiles_per_seq = seq // tm
    row = lambda width: pl.BlockSpec((tm, width), lambda i: (i, 0))
    tab = pl.BlockSpec((tm, LANES), lambda i: (i % tiles_per_seq, 0))
    out_widths = (MLA_PAD_WIDTH, MLA_PAD_WIDTH, MLA_HEADS * MLA_V_DIM, RET_QK_WIDTH, RET_QK_WIDTH,
                  RET_V_WIDTH, RET_V_WIDTH, D_MODEL, D_MODEL)
    return pl.pallas_call(
        _pre_kernel,
        grid=(n // tm,),
        in_specs=[row(D_MODEL)] + [_const_spec(w.shape) for w in weights] + [tab] * len(tables),
        out_specs=[row(w) for w in out_widths],
        out_shape=[jax.ShapeDtypeStruct((n, w), BF16) for w in out_widths],
        compiler_params=pltpu.CompilerParams(dimension_semantics=("arbitrary",), vmem_limit_bytes=VMEM_LIMIT_BYTES),
        name="pre_proj",
    )(x2d, *weights, *tables)


def _attn_kernel(q_ref, k_ref, v_ref, o_ref, m_scr, l_scr, acc_scr, *, tk, nk):
    heads_out = []
    for hh in range(2):
        sl = slice(HEAD_PAD * hh, HEAD_PAD * (hh + 1))
        q = q_ref[:, sl]
        m_scr[...] = jnp.full(m_scr.shape, -jnp.inf, F32)
        l_scr[...] = jnp.zeros(l_scr.shape, F32)
        acc_scr[...] = jnp.zeros(acc_scr.shape, F32)

        def body(kb, carry, q=q, sl=sl):
            off = pl.multiple_of(kb * tk, tk)
            s = _dot_nt(q, k_ref[pl.ds(off, tk), sl])
            m_old = m_scr[...]
            m_new = jnp.maximum(m_old, jnp.max(s, axis=-1, keepdims=True))
            alpha = jnp.exp(m_old - m_new)
            p = jnp.exp(s - m_new)
            l_scr[...] = alpha * l_scr[...] + jnp.sum(p, axis=-1, keepdims=True)
            acc_scr[...] = alpha * acc_scr[...] + _dot(p.astype(BF16), v_ref[pl.ds(off, tk), :])
            m_scr[...] = m_new
            return carry

        lax.fori_loop(0, nk, body, 0)
        heads_out.append(acc_scr[...] / l_scr[...])
    lane = lax.broadcasted_iota(jnp.int32, heads_out[0].shape, 1)
    o_ref[...] = jnp.where(lane < MLA_V_DIM, heads_out[0], heads_out[1]).astype(BF16)


def _attn_call(qcat, kcat, v, batch, seq):
    n = qcat.shape[0]
    tq = min(ATTN_Q_TILE, seq)
    tk = min(ATTN_K_TILE, seq)
    nq = seq // tq
    pairs = MLA_HEADS // 2
    return pl.pallas_call(
        functools.partial(_attn_kernel, tk=tk, nk=seq // tk),
        grid=(batch, pairs, nq),
        in_specs=[
            pl.BlockSpec((tq, 2 * HEAD_PAD), lambda b, p, i: (b * nq + i, p)),
            pl.BlockSpec((seq, 2 * HEAD_PAD), lambda b, p, i: (b, p)),
            pl.BlockSpec((seq, 2 * MLA_V_DIM), lambda b, p, i: (b, p)),
        ],
        out_specs=pl.BlockSpec((tq, 2 * MLA_V_DIM), lambda b, p, i: (b * nq + i, p)),
        out_shape=jax.ShapeDtypeStruct((n, MLA_HEADS * MLA_V_DIM), BF16),
        scratch_shapes=[pltpu.VMEM((tq, 1), F32), pltpu.VMEM((tq, 1), F32), pltpu.VMEM((tq, 2 * MLA_V_DIM), F32)],
        compiler_params=pltpu.CompilerParams(dimension_semantics=("arbitrary",) * 3,
                                             vmem_limit_bytes=VMEM_LIMIT_BYTES),
        name="mla_attn",
    )(qcat, kcat, v)


def _ret_kernel(lg_ref, q_ref, k_ref, v_ref, rg_ref, gn_ref, o_ref, rf_scr, rb_scr, *, chunk, nchunks):
    c_len = chunk
    head = pl.program_id(1)
    lgf = lg_ref[0, head]
    lgb = lg_ref[1, head]
    lane = lax.broadcasted_iota(jnp.int32, (c_len, LANES), 1)
    head_mask = (lane // RET_QK_DIM) == (head % 2)
    pos = lax.broadcasted_iota(jnp.int32, (c_len, LANES), 0).astype(F32)
    zeta_f = jnp.exp((c_len - 1.0 - pos) * lgf)
    xi_f = jnp.exp((pos + 1.0) * lgf)
    zeta_b = jnp.exp(pos * lgb)
    xi_b = jnp.exp((c_len - pos) * lgb)
    sq = (LANES, LANES)
    gchunk_f = jnp.exp(jnp.full(sq, c_len, F32) * lgf)
    gchunk_b = jnp.exp(jnp.full(sq, c_len, F32) * lgb)
    ii = lax.broadcasted_iota(jnp.int32, (c_len, c_len), 0)
    jj = lax.broadcasted_iota(jnp.int32, (c_len, c_len), 1)
    diff = (ii - jj).astype(F32)
    decay = jnp.where(diff >= 0.0, jnp.exp(jnp.maximum(diff, 0.0) * lgf), jnp.exp(jnp.maximum(-diff, 0.0) * lgb))

    def chunk_slice(c):
        return pl.ds(pl.multiple_of(c * c_len, c_len), c_len)

    def state_update(c, state, zeta, gchunk):
        rows = chunk_slice(c)
        km = jnp.where(head_mask, k_ref[rows, :].astype(F32), 0.0)
        return gchunk * state + _dot_tn((km * zeta).astype(BF16), v_ref[rows, :])

    def fwd_body(c, state):
        rf_scr[c] = state
        return state_update(c, state, zeta_f, gchunk_f)

    def bwd_body(t, state):
        c = nchunks - 1 - t
        rb_scr[c] = state
        return state_update(c, state, zeta_b, gchunk_b)

    lax.fori_loop(0, nchunks, fwd_body, jnp.zeros(sq, F32))
    lax.fori_loop(0, nchunks, bwd_body, jnp.zeros(sq, F32))

    gn = gn_ref[...]

    def out_body(c, carry):
        rows = chunk_slice(c)
        qm = jnp.where(head_mask, q_ref[rows, :].astype(F32), 0.0)
        v = v_ref[rows, :]
        inner = _dot_nt(qm.astype(BF16), k_ref[rows, :]) * decay
        y = _dot(inner.astype(BF16), v)
        y = y + _dot((qm * xi_f).astype(BF16), rf_scr[c].astype(BF16))
        y = y + _dot((qm * xi_b).astype(BF16), rb_scr[c].astype(BF16))
        mu = jnp.mean(y, axis=-1, keepdims=True)
        yc = y - mu
        var = jnp.mean(yc * yc, axis=-1, keepdims=True)
        yn = yc * lax.rsqrt(var + GN_EPS) * gn
        rg = rg_ref[rows, :].astype(F32)
        o_ref[rows, :] = (rg * _sigmoid(rg) * yn).astype(BF16)
        return carry

    lax.fori_loop(0, nchunks, out_body, 0)


def _ret_call(log_gamma, rq, rk, rv, rg, gn_g, batch, seq):
    n = rq.shape[0]
    chunk = min(RET_CHUNK, seq)
    nchunks = seq // chunk
    pair_blk = pl.BlockSpec((seq, LANES), lambda b, h: (b, h // 2))
    head_blk = pl.BlockSpec((seq, RET_V_DIM), lambda b, h: (b, h))
    return pl.pallas_call(
        functools.partial(_ret_kernel, chunk=chunk, nchunks=nchunks),
        grid=(batch, RET_HEADS),
        in_specs=[pl.BlockSpec(memory_space=pltpu.SMEM), pair_blk, pair_blk, head_blk, head_blk,
                  pl.BlockSpec((1, RET_V_DIM), lambda b, h: (0, h))],
        out_specs=head_blk,
        out_shape=jax.ShapeDtypeStruct((n, RET_V_WIDTH), BF16),
        scratch_shapes=[pltpu.VMEM((nchunks, LANES, LANES), F32), pltpu.VMEM((nchunks, LANES, LANES), F32)],
        compiler_params=pltpu.CompilerParams(dimension_semantics=("arbitrary",) * 2,
                                             vmem_limit_bytes=VMEM_LIMIT_BYTES),
        name="retention",
    )(log_gamma, rq, rk, rv, rg, gn_g)


def _post_kernel(o_ref, r_ref, ga_ref, gb_ref, x_ref, wa_ref, wb_ref, wo_ref, gffn_ref, wr_hi_lo_ref, wr_hi_ref,
                 x2_out, xn_out, aff_out):
    a = _dot(o_ref[...], wa_ref[...])
    r = _dot(r_ref[...], wb_ref[...])
    mixed = _sigmoid(ga_ref[...].astype(F32)) * a + _sigmoid(gb_ref[...].astype(F32)) * r
    x2 = x_ref[...] + _dot(mixed.astype(BF16), wo_ref[...])
    x2_out[...] = x2
    xn = _rms(x2, gffn_ref[...])
    xn_hi = xn.astype(BF16)
    xn_out[...] = xn_hi
    xn_lo = (xn - xn_hi.astype(F32)).astype(BF16)
    t = _dot(xn_hi, wr_hi_lo_ref[...]) + _dot(xn_lo, wr_hi_ref[...])
    logits = t + pltpu.roll(t, LANES - N_EXPERTS, 1)
    lane = lax.broadcasted_iota(jnp.int32, logits.shape, 1)
    logits = jnp.where(lane < N_EXPERTS, logits, -jnp.inf)
    e = jnp.exp(logits - jnp.max(logits, axis=-1, keepdims=True))
    aff_out[...] = e / jnp.sum(e, axis=-1, keepdims=True)


def _post_call(o, r, ga, gb, x2d, weights):
    n = x2d.shape[0]
    tm = TOKEN_TILE
    row = lambda width: pl.BlockSpec((tm, width), lambda i: (i, 0))
    return pl.pallas_call(
        _post_kernel,
        grid=(n // tm,),
        in_specs=[row(MLA_HEADS * MLA_V_DIM), row(RET_V_WIDTH), row(D_MODEL), row(D_MODEL), row(D_MODEL)]
        + [_const_spec(w.shape) for w in weights],
        out_specs=[row(D_MODEL), row(D_MODEL), row(LANES)],
        out_shape=[jax.ShapeDtypeStruct((n, D_MODEL), F32), jax.ShapeDtypeStruct((n, D_MODEL), BF16),
                   jax.ShapeDtypeStruct((n, LANES), F32)],
        compiler_params=pltpu.CompilerParams(dimension_semantics=("arbitrary",), vmem_limit_bytes=VMEM_LIMIT_BYTES),
        name="post_mix",
    )(o, r, ga, gb, x2d, *weights)


def _ffn_kernel(xe_ref, wg_ref, wu_ref, wd_ref, gate_ref, ye_ref):
    x = xe_ref[...]
    g = _dot(x, wg_ref[...])
    u = _dot(x, wu_ref[...])
    hid = (g * _sigmoid(g) * u).astype(BF16)
    ye_ref[...] = _dot(hid, wd_ref[...]) * gate_ref[...]


def _ffn_call(xe, wg, wu, wd, gate):
    n_exp, cap, d = xe.shape
    tm = min(FFN_ROW_TILE, cap)
    ff = wg.shape[-1]
    return pl.pallas_call(
        _ffn_kernel,
        grid=(n_exp, cap // tm),
        in_specs=[
            pl.BlockSpec((None, tm, d), lambda e, i: (e, i, 0)),
            pl.BlockSpec((None, d, ff), lambda e, i: (e, 0, 0)),
            pl.BlockSpec((None, d, ff), lambda e, i: (e, 0, 0)),
            pl.BlockSpec((None, ff, d), lambda e, i: (e, 0, 0)),
            pl.BlockSpec((None, tm, 1), lambda e, i: (e, i, 0)),
        ],
        out_specs=pl.BlockSpec((None, tm, d), lambda e, i: (e, i, 0)),
        out_shape=jax.ShapeDtypeStruct((n_exp, cap, d), F32),
        compiler_params=pltpu.CompilerParams(dimension_semantics=("arbitrary",) * 2,
                                             vmem_limit_bytes=VMEM_LIMIT_BYTES),
        name="expert_ffn",
    )(xe, wg, wu, wd, gate)


def _final_kernel(x2_ref, f_ref, g_ref, o_ref):
    o_ref[...] = _rms(x2_ref[...] + f_ref[...], g_ref[...])


def _final_call(x2, ffn, g):
    n = x2.shape[0]
    tm = 2 * TOKEN_TILE
    row = pl.BlockSpec((tm, D_MODEL), lambda i: (i, 0))
    return pl.pallas_call(
        _final_kernel,
        grid=(n // tm,),
        in_specs=[row, row, _const_spec(g.shape)],
        out_specs=row,
        out_shape=jax.ShapeDtypeStruct((n, D_MODEL), F32),
        compiler_params=pltpu.CompilerParams(dimension_semantics=("arbitrary",)),
        name="final_norm",
    )(x2, ffn, g)


def _rope_angles(seq, dim):
    inv = 1.0 / (ROPE_THETA ** (jnp.arange(0, dim, 2, dtype=F32) / dim))
    ang = jnp.arange(seq, dtype=F32)[:, None] * inv[None, :]
    return jnp.cos(ang), jnp.sin(ang)


def _mla_tables(seq, scale):
    c, s = _rope_angles(seq, MLA_ROPE_DIM)
    ones = jnp.ones((seq, MLA_NOPE_DIM), F32)
    zeros_n = jnp.zeros((seq, MLA_NOPE_DIM), F32)
    zeros_p = jnp.zeros((seq, HEAD_PAD - MLA_QK_DIM), F32)
    cos = jnp.concatenate([ones, c, c, zeros_p], axis=1) * scale
    sin = jnp.concatenate([zeros_n, -s, s, zeros_p], axis=1) * scale
    return cos, sin


def _ret_tables(seq, scale):
    c, s = _rope_angles(seq, RET_QK_DIM)
    cos = jnp.concatenate([c, c, c, c], axis=1) * scale
    sin = jnp.concatenate([-s, s, -s, s], axis=1) * scale
    return cos, sin


def _prepare_weights(norm_mix_g, w_in, q_norm_g, w_uq, kv_norm_g, w_ukv, ret_gn_g, w_branch_a, w_branch_b, w_out,
                     norm_ffn_g, w_router, norm_final_g):
    offs, acc = [], 0
    for width in IN_SPLITS:
        offs.append((acc, acc + width))
        acc += width
    w_cq, w_ckv, w_kr, w_rq, w_rk, w_rv, w_rg, w_ga, w_gb = [w_in[:, a:b].astype(BF16) for a, b in offs]
    w_kr = jnp.pad(w_kr, ((0, 0), (0, LANES - MLA_ROPE_DIM)))

    uq = w_uq.reshape(Q_LORA_RANK, MLA_HEADS, MLA_QK_DIM)
    wq_pad = jnp.pad(uq, ((0, 0), (0, 0), (0, HEAD_PAD - MLA_QK_DIM))).reshape(Q_LORA_RANK, MLA_PAD_WIDTH)
    ukv = w_ukv.reshape(KV_LORA_RANK, MLA_HEADS, MLA_NOPE_DIM + MLA_V_DIM)
    wk_pad = jnp.pad(ukv[:, :, :MLA_NOPE_DIM], ((0, 0), (0, 0), (0, HEAD_PAD - MLA_NOPE_DIM)))
    wk_pad = wk_pad.reshape(KV_LORA_RANK, MLA_PAD_WIDTH)
    wv = ukv[:, :, MLA_NOPE_DIM:].reshape(KV_LORA_RANK, MLA_HEADS * MLA_V_DIM)
    src = jnp.arange(LANES)[:, None]
    dst = jnp.arange(MLA_PAD_WIDTH)[None, :]
    place = ((dst % HEAD_PAD) - MLA_NOPE_DIM == src) & (src < MLA_ROPE_DIM)

    pre_w = (norm_mix_g.reshape(1, -1), w_cq, w_ckv, w_kr, w_rq, w_rk, w_rv, w_rg, w_ga, w_gb,
             q_norm_g.reshape(1, -1), wq_pad.astype(BF16), kv_norm_g.reshape(1, -1), wk_pad.astype(BF16),
             place.astype(BF16), wv.astype(BF16))

    wr_hi = w_router.astype(BF16)
    wr_lo = (w_router - wr_hi.astype(F32)).astype(BF16)
    pad_to = lambda w: jnp.pad(w, ((0, 0), (0, LANES - w.shape[1])))
    post_w = (w_branch_a.astype(BF16), w_branch_b.astype(BF16), w_out.astype(BF16), norm_ffn_g.reshape(1, -1),
              pad_to(jnp.concatenate([wr_hi, wr_lo], axis=1)), pad_to(wr_hi))
    return pre_w, post_w, ret_gn_g.reshape(1, -1), norm_final_g.reshape(1, -1)


def _encode(x, pre_w, post_w, log_gamma, gn_g, final_g, wg, wu, wd):
    batch, seq, d = x.shape
    n = batch * seq
    x2d = x.reshape(n, d)
    tables = (*_mla_tables(seq, MLA_QK_DIM ** -0.5), *_mla_tables(seq, 1.0),
              *_ret_tables(seq, 1.0), *_ret_tables(seq, RET_QK_DIM ** -0.5))
    qcat, kcat, v, rq, rk, rv, rg, ga, gb = _pre_call(x2d, seq, pre_w, tables)
    o = _attn_call(qcat, kcat, v, batch, seq)
    r = _ret_call(log_gamma, rq, rk, rv, rg, gn_g, batch, seq)
    x2, xn, aff = _post_call(o, r, ga, gb, x2d, post_w)

    cap = CAPACITY_FACTOR * n // N_EXPERTS
    gate, idx = lax.top_k(aff[:, :N_EXPERTS].T, cap)
    xe = xn[idx]
    ye = _ffn_call(xe, wg, wu, wd, gate[..., None])
    ffn = jnp.zeros((n, d), F32).at[idx.reshape(-1)].add(ye.reshape(-1, d))
    return _final_call(x2, ffn, final_g).reshape(batch, seq, d)


def kernel(x_prompt, x_sample, norm_mix_g, w_in, q_norm_g, w_uq, kv_norm_g, w_ukv, ret_decay_fwd, ret_decay_bwd,
           ret_gn_g, w_branch_a, w_branch_b, w_out, norm_ffn_g, w_router, w_exp_gate, w_exp_up, w_exp_down,
           norm_final_g):
    assert norm_mix_g.shape[0] == 1, "single-layer trunk"
    pre_w, post_w, gn_g, final_g = _prepare_weights(
        norm_mix_g[0], w_in[0], q_norm_g[0], w_uq[0], kv_norm_g[0], w_ukv[0], ret_gn_g[0], w_branch_a[0],
        w_branch_b[0], w_out[0], norm_ffn_g[0], w_router[0], norm_final_g)
    log_gamma = jnp.stack([jax.nn.log_sigmoid(ret_decay_fwd[0].astype(F32)),
                           jax.nn.log_sigmoid(ret_decay_bwd[0].astype(F32))])
    wg = w_exp_gate[0].astype(BF16)
    wu = w_exp_up[0].astype(BF16)
    wd = w_exp_down[0].astype(BF16)
    enc = functools.partial(_encode, pre_w=pre_w, post_w=post_w, log_gamma=log_gamma, gn_g=gn_g, final_g=final_g,
                            wg=wg, wu=wu, wd=wd)
    return enc(x_prompt), enc(x_sample)
```

```python
import functools

import jax
import jax.numpy as jnp
from jax import lax
from jax.experimental import pallas as pl
from jax.experimental.pallas import tpu as pltpu

D_MODEL = 1024
MLA_HEADS = 8
MLA_NOPE_DIM = 64
MLA_ROPE_DIM = 32
MLA_V_DIM = 64
MLA_QK_DIM = MLA_NOPE_DIM + MLA_ROPE_DIM
Q_LORA_RANK = 384
KV_LORA_RANK = 256
RET_HEADS = 8
RET_QK_DIM = 64
RET_V_DIM = 128
RET_QK_WIDTH = RET_HEADS * RET_QK_DIM
RET_V_WIDTH = RET_HEADS * RET_V_DIM
N_EXPERTS = 16
EXPERT_FF = 2816
CAPACITY_FACTOR = 2
ROPE_THETA = 10000.0
EPS = 1e-6
GN_EPS = 1e-5
IN_SPLITS = (Q_LORA_RANK, KV_LORA_RANK, MLA_ROPE_DIM, RET_QK_WIDTH, RET_QK_WIDTH, RET_V_WIDTH, RET_V_WIDTH,
             D_MODEL, D_MODEL)

LANES = 128
HEAD_PAD = LANES
MLA_PAD_WIDTH = MLA_HEADS * HEAD_PAD
VMEM_LIMIT_BYTES = 56 * 1024 * 1024

TOKEN_TILE = 256
VT_KEYS = TOKEN_TILE
LOG2_E = 1.4426950408889634
ATTN_Q_TILE = 512
ATTN_K_TILE = 512
RET_CHUNK = 256
FFN_ROW_TILE = 256

F32 = jnp.float32
BF16 = jnp.bfloat16


def _dot(a, b):
    return jnp.dot(a, b, preferred_element_type=F32)


def _dot_nt(a, b):
    return lax.dot_general(a, b, (((1,), (1,)), ((), ())), preferred_element_type=F32)


def _dot_tn(a, b):
    return lax.dot_general(a, b, (((0,), (0,)), ((), ())), preferred_element_type=F32)


def _rms(x, g):
    return x * lax.rsqrt(jnp.mean(x * x, axis=-1, keepdims=True) + EPS) * g


def _sigmoid(x):
    return 1.0 / (1.0 + jnp.exp(-x))


def _rope_lanes(blk, cos, sin, half):
    lane = lax.broadcasted_iota(jnp.int32, blk.shape, 1)
    upper = (lane % (2 * half)) >= half
    partner = jnp.where(upper, pltpu.roll(blk, half, 1), pltpu.roll(blk, LANES - half, 1))
    return blk * cos + partner * sin


def _const_spec(shape):
    nd = len(shape)
    return pl.BlockSpec(shape, lambda *_: (0,) * nd, pipeline_mode=pl.Buffered(1))


def _pre_kernel(x_ref, gmix_ref, wcq_ref, wckv_ref, wkr_ref, wrq_ref, wrk_ref, wrv_ref, wrg_ref, wga_ref, wgb_ref,
                gq_ref, wq_ref, gkv_ref, wk_ref, place_ref, wv_ref,
                cosq_ref, sinq_ref, cosk_ref, sink_ref, cosrq_ref, sinrq_ref, cosrk_ref, sinrk_ref,
                q_out, k_out, vt_out, rq_out, rk_out, rv_out, rg_out, ga_out, gb_out):
    h = _rms(x_ref[...], gmix_ref[...]).astype(BF16)

    qn = _rms(_dot(h, wcq_ref[...]), gq_ref[...]).astype(BF16)
    q = _dot(qn, wq_ref[...])
    cosq, sinq = cosq_ref[...], sinq_ref[...]
    for j in range(MLA_HEADS):
        sl = slice(HEAD_PAD * j, HEAD_PAD * (j + 1))
        q_out[:, sl] = _rope_lanes(q[:, sl], cosq, sinq, MLA_ROPE_DIM // 2).astype(BF16)

    kvn = _rms(_dot(h, wckv_ref[...]), gkv_ref[...]).astype(BF16)
    kr = _dot(h, wkr_ref[...]).astype(BF16)
    k = _dot(kvn, wk_ref[...]) + _dot(kr, place_ref[...])
    cosk, sink = cosk_ref[...], sink_ref[...]
    for j in range(MLA_HEADS):
        sl = slice(HEAD_PAD * j, HEAD_PAD * (j + 1))
        k_out[:, sl] = _rope_lanes(k[:, sl], cosk, sink, MLA_ROPE_DIM // 2).astype(BF16)
    vt = _dot_nt(wv_ref[...], kvn)
    vrow = lax.broadcasted_iota(jnp.int32, vt.shape, 0)
    vt_out[...] = jnp.where(vrow % HEAD_PAD == MLA_V_DIM, 1.0, vt).astype(BF16)

    rq = _dot(h, wrq_ref[...])
    rk = _dot(h, wrk_ref[...])
    cosrq, sinrq, cosrk, sinrk = cosrq_ref[...], sinrq_ref[...], cosrk_ref[...], sinrk_ref[...]
    for j in range(RET_QK_WIDTH // LANES):
        sl = slice(LANES * j, LANES * (j + 1))
        rq_out[:, sl] = _rope_lanes(rq[:, sl], cosrq, sinrq, RET_QK_DIM // 2).astype(BF16)
        rk_out[:, sl] = _rope_lanes(rk[:, sl], cosrk, sinrk, RET_QK_DIM // 2).astype(BF16)

    rv_out[...] = _dot(h, wrv_ref[...]).astype(BF16)
    rg_out[...] = _dot(h, wrg_ref[...]).astype(BF16)
    ga_out[...] = _dot(h, wga_ref[...]).astype(BF16)
    gb_out[...] = _dot(h, wgb_ref[...]).astype(BF16)


def _pre_call(x2d, seq, weights, tables):
    n = x2d.shape[0]
    tm = TOKEN_TILE
    tiles_per_seq = seq // tm
    row = lambda width: pl.BlockSpec((tm, width), lambda i: (i, 0))
    tab = pl.BlockSpec((tm, LANES), lambda i: (i % tiles_per_seq, 0))
    out_widths = (MLA_PAD_WIDTH, MLA_PAD_WIDTH, None, RET_QK_WIDTH, RET_QK_WIDTH,
                  RET_V_WIDTH, RET_V_WIDTH, D_MODEL, D_MODEL)
    vt_spec = pl.BlockSpec((None, None, MLA_PAD_WIDTH, tm), lambda i: (i // tiles_per_seq, i % tiles_per_seq, 0, 0))
    vt_shape = jax.ShapeDtypeStruct((n // seq, tiles_per_seq, MLA_PAD_WIDTH, tm), BF16)
    return pl.pallas_call(
        _pre_kernel,
        grid=(n // tm,),
        in_specs=[row(D_MODEL)] + [_const_spec(w.shape) for w in weights] + [tab] * len(tables),
        out_specs=[vt_spec if w is None else row(w) for w in out_widths],
        out_shape=[vt_shape if w is None else jax.ShapeDtypeStruct((n, w), BF16) for w in out_widths],
        compiler_params=pltpu.CompilerParams(dimension_semantics=("arbitrary",), vmem_limit_bytes=VMEM_LIMIT_BYTES),
        name="pre_proj",
    )(x2d, *weights, *tables)


def _attn_kernel(q_ref, k_ref, vt_ref, o_ref, st_scr, mc_scr, m_scr, acc_scr, *, tk, nk):
    sub = tk // VT_KEYS
    heads = [slice(HEAD_PAD * hh, HEAD_PAD * (hh + 1)) for hh in range(2)]

    def scores(kb, buf):
        off = pl.multiple_of(kb * tk, tk)
        for hh, sl in enumerate(heads):
            st = _dot_nt(k_ref[pl.ds(off, tk), sl], q_ref[:, sl])
            st_scr[buf, hh] = st
            mc_scr[buf, hh] = jnp.max(st, axis=0, keepdims=True)

    def accumulate(kb, buf):
        for hh, sl in enumerate(heads):
            m_old = m_scr[hh]
            m_new = jnp.maximum(m_old, mc_scr[buf, hh])
            alpha = jnp.exp2(m_old - m_new)
            pt = jnp.exp2(st_scr[buf, hh] - m_new).astype(BF16)
            pv = _dot(vt_ref[kb * sub, sl, :], pt[0:VT_KEYS, :])
            for j in range(1, sub):
                pv = pv + _dot(vt_ref[kb * sub + j, sl, :], pt[VT_KEYS * j:VT_KEYS * (j + 1), :])
            acc_scr[hh] = alpha * acc_scr[hh] + pv
            m_scr[hh] = m_new

    m_scr[...] = jnp.full(m_scr.shape, -jnp.inf, F32)
    acc_scr[...] = jnp.zeros(acc_scr.shape, F32)
    scores(0, 0)
    if nk > 1:
        assert nk % 2 == 0

        def body(j, carry):
            kb = 2 * j
            scores(kb + 1, 1)
            accumulate(kb, 0)
            scores(kb + 2, 0)
            accumulate(kb + 1, 1)
            return carry

        lax.fori_loop(0, nk // 2 - 1, body, 0)
        scores(nk - 1, 1)
        accumulate(nk - 2, 0)
        accumulate(nk - 1, 1)
    else:
        accumulate(0, 0)
    outs = []
    for hh in range(2):
        acc = acc_scr[hh]
        outs.append(acc[0:MLA_V_DIM, :] / acc[MLA_V_DIM:MLA_V_DIM + 1, :])
    o_ref[...] = jnp.concatenate(outs, axis=0).T.astype(BF16)


def _attn_call(qcat, kcat, vt, batch, seq):
    n = qcat.shape[0]
    tq = min(ATTN_Q_TILE, seq)
    tk = min(ATTN_K_TILE, seq)
    nq = seq // tq
    pairs = MLA_HEADS // 2
    return pl.pallas_call(
        functools.partial(_attn_kernel, tk=tk, nk=seq // tk),
        grid=(batch, pairs, nq),
        in_specs=[
            pl.BlockSpec((tq, 2 * HEAD_PAD), lambda b, p, i: (b * nq + i, p)),
            pl.BlockSpec((seq, 2 * HEAD_PAD), lambda b, p, i: (b, p)),
            pl.BlockSpec((None, seq // VT_KEYS, 2 * HEAD_PAD, VT_KEYS), lambda b, p, i: (b, 0, p, 0)),
        ],
        out_specs=pl.BlockSpec((tq, 2 * MLA_V_DIM), lambda b, p, i: (b * nq + i, p)),
        out_shape=jax.ShapeDtypeStruct((n, MLA_HEADS * MLA_V_DIM), BF16),
        scratch_shapes=[pltpu.VMEM((2, 2, tk, tq), F32), pltpu.VMEM((2, 2, 1, tq), F32),
                        pltpu.VMEM((2, 1, tq), F32), pltpu.VMEM((2, HEAD_PAD, tq), F32)],
        compiler_params=pltpu.CompilerParams(dimension_semantics=("arbitrary",) * 3,
                                             vmem_limit_bytes=VMEM_LIMIT_BYTES),
        name="mla_attn",
    )(qcat, kcat, vt)


def _ret_kernel(lg_ref, q_ref, k_ref, v_ref, rg_ref, gn_ref, o_ref, rb_scr, *, chunk, nchunks):
    c_len = chunk
    pair = pl.program_id(1)
    lane = lax.broadcasted_iota(jnp.int32, (c_len, LANES), 1)
    pos = lax.broadcasted_iota(jnp.int32, (c_len, LANES), 0).astype(F32)
    ii = lax.broadcasted_iota(jnp.int32, (c_len, c_len), 0)
    jj = lax.broadcasted_iota(jnp.int32, (c_len, c_len), 1)
    diff = (ii - jj).astype(F32)
    sq = (LANES, LANES)
    two = range(2)
    lgf = [lg_ref[0, 2 * pair + hh] for hh in two]
    lgb = [lg_ref[1, 2 * pair + hh] for hh in two]
    mask = [(lane // RET_QK_DIM) == hh for hh in two]
    vsl = [slice(RET_V_DIM * hh, RET_V_DIM * (hh + 1)) for hh in two]
    zeta_f = [jnp.exp((c_len - 1.0 - pos) * lgf[hh]) for hh in two]
    xi_f = [jnp.exp((pos + 1.0) * lgf[hh]) for hh in two]
    zeta_b = [jnp.exp(pos * lgb[hh]) for hh in two]
    xi_b = [jnp.exp((c_len - pos) * lgb[hh]) for hh in two]
    gchunk_f = [jnp.exp(jnp.full(sq, c_len, F32) * lgf[hh]) for hh in two]
    gchunk_b = [jnp.exp(jnp.full(sq, c_len, F32) * lgb[hh]) for hh in two]
    decay = [jnp.where(diff >= 0.0, jnp.exp(jnp.maximum(diff, 0.0) * lgf[hh]),
                       jnp.exp(jnp.maximum(-diff, 0.0) * lgb[hh])) for hh in two]
    gn = gn_ref[...]

    def chunk_slice(c):
        return pl.ds(pl.multiple_of(c * c_len, c_len), c_len)

    def bwd_body(t, states):
        c = nchunks - 1 - t
        rows = chunk_slice(c)
        k = k_ref[rows, :].astype(F32)
        new = []
        for hh in two:
            rb_scr[hh, c] = states[hh].astype(BF16)
            kz = (jnp.where(mask[hh], k, 0.0) * zeta_b[hh]).astype(BF16)
            new.append(gchunk_b[hh] * states[hh] + _dot_tn(kz, v_ref[rows, vsl[hh]]))
        return tuple(new)

    zero_states = (jnp.zeros(sq, F32), jnp.zeros(sq, F32))
    lax.fori_loop(0, nchunks, bwd_body, zero_states)

    def fwd_body(c, states):
        rows = chunk_slice(c)
        q = q_ref[rows, :].astype(F32)
        k16 = k_ref[rows, :]
        k = k16.astype(F32)
        qm = [jnp.where(mask[hh], q, 0.0) for hh in two]
        inner = [_dot_nt(qm[hh].astype(BF16), k16) for hh in two]
        ys = []
        for hh in two:
            lhs = jnp.concatenate([(inner[hh] * decay[hh]).astype(BF16), (qm[hh] * xi_f[hh]).astype(BF16),
                                   (qm[hh] * xi_b[hh]).astype(BF16)], axis=1)
            rhs = jnp.concatenate([v_ref[rows, vsl[hh]], states[hh].astype(BF16), rb_scr[hh, c]], axis=0)
            ys.append(_dot(lhs, rhs))
        new = []
        for hh in two:
            y = ys[hh]
            mu = jnp.mean(y, axis=-1, keepdims=True)
            yc = y - mu
            var = jnp.mean(yc * yc, axis=-1, keepdims=True)
            yn = yc * lax.rsqrt(var + GN_EPS) * gn[:, vsl[hh]]
            rg = rg_ref[rows, vsl[hh]].astype(F32)
            o_ref[rows, vsl[hh]] = (rg * _sigmoid(rg) * yn).astype(BF16)
            kz = (jnp.where(mask[hh], k, 0.0) * zeta_f[hh]).astype(BF16)
            new.append(gchunk_f[hh] * states[hh] + _dot_tn(kz, v_ref[rows, vsl[hh]]))
        return tuple(new)

    lax.fori_loop(0, nchunks, fwd_body, zero_states)


def _ret_call(log_gamma, rq, rk, rv, rg, gn_g, batch, seq):
    n = rq.shape[0]
    chunk = min(RET_CHUNK, seq)
    nchunks = seq // chunk
    qk_blk = pl.BlockSpec((seq, LANES), lambda b, p: (b, p))
    v_blk = pl.BlockSpec((seq, 2 * RET_V_DIM), lambda b, p: (b, p))
    return pl.pallas_call(
        functools.partial(_ret_kernel, chunk=chunk, nchunks=nchunks),
        grid=(batch, RET_HEADS // 2),
        in_specs=[pl.BlockSpec(memory_space=pltpu.SMEM), qk_blk, qk_blk, v_blk, v_blk,
                  pl.BlockSpec((1, 2 * RET_V_DIM), lambda b, p: (0, p))],
        out_specs=v_blk,
        out_shape=jax.ShapeDtypeStruct((n, RET_V_WIDTH), BF16),
        scratch_shapes=[pltpu.VMEM((2, nchunks, LANES, LANES), BF16)],
        compiler_params=pltpu.CompilerParams(dimension_semantics=("arbitrary",) * 2,
                                             vmem_limit_bytes=VMEM_LIMIT_BYTES),
        name="retention",
    )(log_gamma, rq, rk, rv, rg, gn_g)


def _post_kernel(o_ref, r_ref, ga_ref, gb_ref, x_ref, wa_ref, wb_ref, wo_ref, gffn_ref, wr_hi_lo_ref, wr_hi_ref,
                 x2_out, xn_out, aff_out):
    a = _dot(o_ref[...], wa_ref[...])
    r = _dot(r_ref[...], wb_ref[...])
    mixed = _sigmoid(ga_ref[...].astype(F32)) * a + _sigmoid(gb_ref[...].astype(F32)) * r
    x2 = x_ref[...] + _dot(mixed.astype(BF16), wo_ref[...])
    x2_out[...] = x2
    xn = _rms(x2, gffn_ref[...])
    xn_hi = xn.astype(BF16)
    xn_out[...] = xn_hi
    xn_lo = (xn - xn_hi.astype(F32)).astype(BF16)
    t = _dot(xn_hi, wr_hi_lo_ref[...]) + _dot(xn_lo, wr_hi_ref[...])
    logits = t + pltpu.roll(t, LANES - N_EXPERTS, 1)
    lane = lax.broadcasted_iota(jnp.int32, logits.shape, 1)
    logits = jnp.where(lane < N_EXPERTS, logits, -jnp.inf)
    e = jnp.exp(logits - jnp.max(logits, axis=-1, keepdims=True))
    aff_out[...] = e / jnp.sum(e, axis=-1, keepdims=True)


def _post_call(o, r, ga, gb, x2d, weights):
    n = x2d.shape[0]
    tm = TOKEN_TILE
    row = lambda width: pl.BlockSpec((tm, width), lambda i: (i, 0))
    return pl.pallas_call(
        _post_kernel,
        grid=(n // tm,),
        in_specs=[row(MLA_HEADS * MLA_V_DIM), row(RET_V_WIDTH), row(D_MODEL), row(D_MODEL), row(D_MODEL)]
        + [_const_spec(w.shape) for w in weights],
        out_specs=[row(D_MODEL), row(D_MODEL), row(LANES)],
        out_shape=[jax.ShapeDtypeStruct((n, D_MODEL), F32), jax.ShapeDtypeStruct((n, D_MODEL), BF16),
                   jax.ShapeDtypeStruct((n, LANES), F32)],
        compiler_params=pltpu.CompilerParams(dimension_semantics=("arbitrary",), vmem_limit_bytes=VMEM_LIMIT_BYTES),
        name="post_mix",
    )(o, r, ga, gb, x2d, *weights)


def _ffn_kernel(xe_ref, wg_ref, wu_ref, wd_ref, gate_ref, ye_ref):
    x = xe_ref[...]
    g = _dot(x, wg_ref[...])
    u = _dot(x, wu_ref[...])
    hid = (g * _sigmoid(g) * u).astype(BF16)
    ye_ref[...] = _dot(hid, wd_ref[...]) * gate_ref[...]


def _ffn_call(xe, wg, wu, wd, gate):
    n_exp, cap, d = xe.shape
    tm = min(FFN_ROW_TILE, cap)
    ff = wg.shape[-1]
    return pl.pallas_call(
        _ffn_kernel,
        grid=(n_exp, cap // tm),
        in_specs=[
            pl.BlockSpec((None, tm, d), lambda e, i: (e, i, 0)),
            pl.BlockSpec((None, d, ff), lambda e, i: (e, 0, 0)),
            pl.BlockSpec((None, d, ff), lambda e, i: (e, 0, 0)),
            pl.BlockSpec((None, ff, d), lambda e, i: (e, 0, 0)),
            pl.BlockSpec((None, tm, 1), lambda e, i: (e, i, 0)),
        ],
        out_specs=pl.BlockSpec((None, tm, d), lambda e, i: (e, i, 0)),
        out_shape=jax.ShapeDtypeStruct((n_exp, cap, d), F32),
        compiler_params=pltpu.CompilerParams(dimension_semantics=("arbitrary",) * 2,
                                             vmem_limit_bytes=VMEM_LIMIT_BYTES),
        name="expert_ffn",
    )(xe, wg, wu, wd, gate)


def _final_kernel(x2_ref, f_ref, g_ref, o_ref):
    o_ref[...] = _rms(x2_ref[...] + f_ref[...], g_ref[...])


def _final_call(x2, ffn, g):
    n = x2.shape[0]
    tm = 2 * TOKEN_TILE
    row = pl.BlockSpec((tm, D_MODEL), lambda i: (i, 0))
    return pl.pallas_call(
        _final_kernel,
        grid=(n // tm,),
        in_specs=[row, row, _const_spec(g.shape)],
        out_specs=row,
        out_shape=jax.ShapeDtypeStruct((n, D_MODEL), F32),
        compiler_params=pltpu.CompilerParams(dimension_semantics=("arbitrary",)),
        name="final_norm",
    )(x2, ffn, g)


def _rope_angles(seq, dim):
    inv = 1.0 / (ROPE_THETA ** (jnp.arange(0, dim, 2, dtype=F32) / dim))
    ang = jnp.arange(seq, dtype=F32)[:, None] * inv[None, :]
    return jnp.cos(ang), jnp.sin(ang)


def _mla_tables(seq, scale):
    c, s = _rope_angles(seq, MLA_ROPE_DIM)
    ones = jnp.ones((seq, MLA_NOPE_DIM), F32)
    zeros_n = jnp.zeros((seq, MLA_NOPE_DIM), F32)
    zeros_p = jnp.zeros((seq, HEAD_PAD - MLA_QK_DIM), F32)
    cos = jnp.concatenate([ones, c, c, zeros_p], axis=1) * scale
    sin = jnp.concatenate([zeros_n, -s, s, zeros_p], axis=1) * scale
    return cos, sin


def _ret_tables(seq, scale):
    c, s = _rope_angles(seq, RET_QK_DIM)
    cos = jnp.concatenate([c, c, c, c], axis=1) * scale
    sin = jnp.concatenate([-s, s, -s, s], axis=1) * scale
    return cos, sin


def _prepare_weights(norm_mix_g, w_in, q_norm_g, w_uq, kv_norm_g, w_ukv, ret_gn_g, w_branch_a, w_branch_b, w_out,
                     norm_ffn_g, w_router, norm_final_g):
    offs, acc = [], 0
    for width in IN_SPLITS:
        offs.append((acc, acc + width))
        acc += width
    w_cq, w_ckv, w_kr, w_rq, w_rk, w_rv, w_rg, w_ga, w_gb = [w_in[:, a:b].astype(BF16) for a, b in offs]
    w_kr = jnp.pad(w_kr, ((0, 0), (0, LANES - MLA_ROPE_DIM)))

    uq = w_uq.reshape(Q_LORA_RANK, MLA_HEADS, MLA_QK_DIM)
    wq_pad = jnp.pad(uq, ((0, 0), (0, 0), (0, HEAD_PAD - MLA_QK_DIM))).reshape(Q_LORA_RANK, MLA_PAD_WIDTH)
    ukv = w_ukv.reshape(KV_LORA_RANK, MLA_HEADS, MLA_NOPE_DIM + MLA_V_DIM)
    wk_pad = jnp.pad(ukv[:, :, :MLA_NOPE_DIM], ((0, 0), (0, 0), (0, HEAD_PAD - MLA_NOPE_DIM)))
    wk_pad = wk_pad.reshape(KV_LORA_RANK, MLA_PAD_WIDTH)
    wv = jnp.pad(ukv[:, :, MLA_NOPE_DIM:], ((0, 0), (0, 0), (0, HEAD_PAD - MLA_V_DIM)))
    wv = wv.reshape(KV_LORA_RANK, MLA_PAD_WIDTH).T
    src = jnp.arange(LANES)[:, None]
    dst = jnp.arange(MLA_PAD_WIDTH)[None, :]
    place = ((dst % HEAD_PAD) - MLA_NOPE_DIM == src) & (src < MLA_ROPE_DIM)

    pre_w = (norm_mix_g.reshape(1, -1), w_cq, w_ckv, w_kr, w_rq, w_rk, w_rv, w_rg, w_ga, w_gb,
             q_norm_g.reshape(1, -1), wq_pad.astype(BF16), kv_norm_g.reshape(1, -1), wk_pad.astype(BF16),
             place.astype(BF16), wv.astype(BF16))

    wr_hi = w_router.astype(BF16)
    wr_lo = (w_router - wr_hi.astype(F32)).astype(BF16)
    pad_to = lambda w: jnp.pad(w, ((0, 0), (0, LANES - w.shape[1])))
    post_w = (w_branch_a.astype(BF16), w_branch_b.astype(BF16), w_out.astype(BF16), norm_ffn_g.reshape(1, -1),
              pad_to(jnp.concatenate([wr_hi, wr_lo], axis=1)), pad_to(wr_hi))
    return pre_w, post_w, ret_gn_g.reshape(1, -1), norm_final_g.reshape(1, -1)


def _encode(x, pre_w, post_w, log_gamma, gn_g, final_g, wg, wu, wd):
    batch, seq, d = x.shape
    n = batch * seq
    x2d = x.reshape(n, d)
    tables = (*_mla_tables(seq, MLA_QK_DIM ** -0.5 * LOG2_E), *_mla_tables(seq, 1.0),
              *_ret_tables(seq, 1.0), *_ret_tables(seq, RET_QK_DIM ** -0.5))
    qcat, kcat, vt, rq, rk, rv, rg, ga, gb = _pre_call(x2d, seq, pre_w, tables)
    o = _attn_call(qcat, kcat, vt, batch, seq)
    r = _ret_call(log_gamma, rq, rk, rv, rg, gn_g, batch, seq)
    x2, xn, aff = _post_call(o, r, ga, gb, x2d, post_w)

    cap = CAPACITY_FACTOR * n // N_EXPERTS
    gate, idx = lax.top_k(aff[:, :N_EXPERTS].T, cap)
    xe = xn[idx]
    ye = _ffn_call(xe, wg, wu, wd, gate[..., None])
    ffn = jnp.zeros((n, d), F32).at[idx.reshape(-1)].add(ye.reshape(-1, d))
    return _final_call(x2, ffn, final_g).reshape(batch, seq, d)


def kernel(x_prompt, x_sample, norm_mix_g, w_in, q_norm_g, w_uq, kv_norm_g, w_ukv, ret_decay_fwd, ret_decay_bwd,
           ret_gn_g, w_branch_a, w_branch_b, w_out, norm_ffn_g, w_router, w_exp_gate, w_exp_up, w_exp_down,
           norm_final_g):
    assert norm_mix_g.shape[0] == 1, "single-layer trunk"
    pre_w, post_w, gn_g, final_g = _prepare_weights(
        norm_mix_g[0], w_in[0], q_norm_g[0], w_uq[0], kv_norm_g[0], w_ukv[0], ret_gn_g[0], w_branch_a[0],
        w_branch_b[0], w_out[0], norm_ffn_g[0], w_router[0], norm_final_g)
    log_gamma = jnp.stack([jax.nn.log_sigmoid(ret_decay_fwd[0].astype(F32)),
                           jax.nn.log_sigmoid(ret_decay_bwd[0].astype(F32))])
    wg = w_exp_gate[0].astype(BF16)
    wu = w_exp_up[0].astype(BF16)
    wd = w_exp_down[0].astype(BF16)
    enc = functools.partial(_encode, pre_w=pre_w, post_w=post_w, log_gamma=log_gamma, gn_g=gn_g, final_g=final_g,
                            wg=wg, wu=wu, wd=wd)
    return enc(x_prompt), enc(x_sample)
```

```python
import functools

import jax
import jax.numpy as jnp
from jax import lax
from jax.experimental import pallas as pl
from jax.experimental.pallas import tpu as pltpu

D_MODEL = 1024
MLA_HEADS = 8
MLA_NOPE_DIM = 64
MLA_ROPE_DIM = 32
MLA_V_DIM = 64
MLA_QK_DIM = MLA_NOPE_DIM + MLA_ROPE_DIM
Q_LORA_RANK = 384
KV_LORA_RANK = 256
RET_HEADS = 8
RET_QK_DIM = 64
RET_V_DIM = 128
RET_QK_WIDTH = RET_HEADS * RET_QK_DIM
RET_V_WIDTH = RET_HEADS * RET_V_DIM
N_EXPERTS = 16
EXPERT_FF = 2816
CAPACITY_FACTOR = 2
ROPE_THETA = 10000.0
EPS = 1e-6
GN_EPS = 1e-5
IN_SPLITS = (Q_LORA_RANK, KV_LORA_RANK, MLA_ROPE_DIM, RET_QK_WIDTH, RET_QK_WIDTH, RET_V_WIDTH, RET_V_WIDTH,
             D_MODEL, D_MODEL)

LANES = 128
HEAD_PAD = LANES
MLA_PAD_WIDTH = MLA_HEADS * HEAD_PAD
VMEM_LIMIT_BYTES = 56 * 1024 * 1024

TOKEN_TILE = 256
VT_KEYS = TOKEN_TILE
LOG2_E = 1.4426950408889634
ROUTE_TILE = TOKEN_TILE
WINDOW_ALIGN = 16
WINDOW_ROWS = ROUTE_TILE + WINDOW_ALIGN
ATTN_Q_TILE = 512
ATTN_K_TILE = 512
RET_CHUNK = 256
FFN_ROW_TILE = 256

F32 = jnp.float32
BF16 = jnp.bfloat16


def _dot(a, b):
    return jnp.dot(a, b, preferred_element_type=F32)


def _dot_nt(a, b):
    return lax.dot_general(a, b, (((1,), (1,)), ((), ())), preferred_element_type=F32)


def _dot_tn(a, b):
    return lax.dot_general(a, b, (((0,), (0,)), ((), ())), preferred_element_type=F32)


def _rms(x, g):
    return x * lax.rsqrt(jnp.mean(x * x, axis=-1, keepdims=True) + EPS) * g


def _sigmoid(x):
    return 1.0 / (1.0 + jnp.exp(-x))


def _rope_lanes(blk, cos, sin, half):
    lane = lax.broadcasted_iota(jnp.int32, blk.shape, 1)
    upper = (lane % (2 * half)) >= half
    partner = jnp.where(upper, pltpu.roll(blk, half, 1), pltpu.roll(blk, LANES - half, 1))
    return blk * cos + partner * sin


def _const_spec(shape):
    nd = len(shape)
    return pl.BlockSpec(shape, lambda *_: (0,) * nd, pipeline_mode=pl.Buffered(1))


def _pre_kernel(x_ref, gmix_ref, wcq_ref, wckv_ref, wkr_ref, wrq_ref, wrk_ref, wrv_ref, wrg_ref, wga_ref, wgb_ref,
                gq_ref, wq_ref, gkv_ref, wk_ref, place_ref, wv_ref,
                cosq_ref, sinq_ref, cosk_ref, sink_ref, cosrq_ref, sinrq_ref, cosrk_ref, sinrk_ref,
                q_out, k_out, vt_out, rq_out, rk_out, rv_out, rg_out, ga_out, gb_out):
    h = _rms(x_ref[...], gmix_ref[...]).astype(BF16)

    qn = _rms(_dot(h, wcq_ref[...]), gq_ref[...]).astype(BF16)
    q = _dot(qn, wq_ref[...])
    cosq, sinq = cosq_ref[...], sinq_ref[...]
    for j in range(MLA_HEADS):
        sl = slice(HEAD_PAD * j, HEAD_PAD * (j + 1))
        q_out[:, sl] = _rope_lanes(q[:, sl], cosq, sinq, MLA_ROPE_DIM // 2).astype(BF16)

    kvn = _rms(_dot(h, wckv_ref[...]), gkv_ref[...]).astype(BF16)
    kr = _dot(h, wkr_ref[...]).astype(BF16)
    k = _dot(kvn, wk_ref[...]) + _dot(kr, place_ref[...])
    cosk, sink = cosk_ref[...], sink_ref[...]
    for j in range(MLA_HEADS):
        sl = slice(HEAD_PAD * j, HEAD_PAD * (j + 1))
        k_out[:, sl] = _rope_lanes(k[:, sl], cosk, sink, MLA_ROPE_DIM // 2).astype(BF16)
    vt = _dot_nt(wv_ref[...], kvn)
    vrow = lax.broadcasted_iota(jnp.int32, vt.shape, 0)
    vt_out[...] = jnp.where(vrow % HEAD_PAD == MLA_V_DIM, 1.0, vt).astype(BF16)

    rq = _dot(h, wrq_ref[...])
    rk = _dot(h, wrk_ref[...])
    cosrq, sinrq, cosrk, sinrk = cosrq_ref[...], sinrq_ref[...], cosrk_ref[...], sinrk_ref[...]
    for j in range(RET_QK_WIDTH // LANES):
        sl = slice(LANES * j, LANES * (j + 1))
        rq_out[:, sl] = _rope_lanes(rq[:, sl], cosrq, sinrq, RET_QK_DIM // 2).astype(BF16)
        rk_out[:, sl] = _rope_lanes(rk[:, sl], cosrk, sinrk, RET_QK_DIM // 2).astype(BF16)

    rv_out[...] = _dot(h, wrv_ref[...]).astype(BF16)
    rg_out[...] = _dot(h, wrg_ref[...]).astype(BF16)
    ga_out[...] = _dot(h, wga_ref[...]).astype(BF16)
    gb_out[...] = _dot(h, wgb_ref[...]).astype(BF16)


def _pre_call(x2d, seq, weights, tables):
    n = x2d.shape[0]
    tm = TOKEN_TILE
    tiles_per_seq = seq // tm
    row = lambda width: pl.BlockSpec((tm, width), lambda i: (i, 0))
    tab = pl.BlockSpec((tm, LANES), lambda i: (i % tiles_per_seq, 0))
    out_widths = (MLA_PAD_WIDTH, MLA_PAD_WIDTH, None, RET_QK_WIDTH, RET_QK_WIDTH,
                  RET_V_WIDTH, RET_V_WIDTH, D_MODEL, D_MODEL)
    vt_spec = pl.BlockSpec((None, None, MLA_PAD_WIDTH, tm), lambda i: (i // tiles_per_seq, i % tiles_per_seq, 0, 0))
    vt_shape = jax.ShapeDtypeStruct((n // seq, tiles_per_seq, MLA_PAD_WIDTH, tm), BF16)
    return pl.pallas_call(
        _pre_kernel,
        grid=(n // tm,),
        in_specs=[row(D_MODEL)] + [_const_spec(w.shape) for w in weights] + [tab] * len(tables),
        out_specs=[vt_spec if w is None else row(w) for w in out_widths],
        out_shape=[vt_shape if w is None else jax.ShapeDtypeStruct((n, w), BF16) for w in out_widths],
        compiler_params=pltpu.CompilerParams(dimension_semantics=("arbitrary",), vmem_limit_bytes=VMEM_LIMIT_BYTES),
        name="pre_proj",
    )(x2d, *weights, *tables)


def _attn_kernel(q_ref, k_ref, vt_ref, o_ref, st_scr, mc_scr, m_scr, acc_scr, *, tk, nk):
    sub = tk // VT_KEYS
    heads = [slice(HEAD_PAD * hh, HEAD_PAD * (hh + 1)) for hh in range(2)]

    def scores(kb, buf):
        off = pl.multiple_of(kb * tk, tk)
        for hh, sl in enumerate(heads):
            st = _dot_nt(k_ref[pl.ds(off, tk), sl], q_ref[:, sl])
            st_scr[buf, hh] = st
            mc_scr[buf, hh] = jnp.max(st, axis=0, keepdims=True)

    def accumulate(kb, buf):
        for hh, sl in enumerate(heads):
            m_old = m_scr[hh]
            m_new = jnp.maximum(m_old, mc_scr[buf, hh])
            alpha = jnp.exp2(m_old - m_new)
            pt = jnp.exp2(st_scr[buf, hh] - m_new).astype(BF16)
            pv = _dot(vt_ref[kb * sub, sl, :], pt[0:VT_KEYS, :])
            for j in range(1, sub):
                pv = pv + _dot(vt_ref[kb * sub + j, sl, :], pt[VT_KEYS * j:VT_KEYS * (j + 1), :])
            acc_scr[hh] = alpha * acc_scr[hh] + pv
            m_scr[hh] = m_new

    m_scr[...] = jnp.full(m_scr.shape, -jnp.inf, F32)
    acc_scr[...] = jnp.zeros(acc_scr.shape, F32)
    scores(0, 0)
    if nk > 1:
        assert nk % 2 == 0

        def body(j, carry):
            kb = 2 * j
            scores(kb + 1, 1)
            accumulate(kb, 0)
            scores(kb + 2, 0)
            accumulate(kb + 1, 1)
            return carry

        lax.fori_loop(0, nk // 2 - 1, body, 0)
        scores(nk - 1, 1)
        accumulate(nk - 2, 0)
        accumulate(nk - 1, 1)
    else:
        accumulate(0, 0)
    outs = []
    for hh in range(2):
        acc = acc_scr[hh]
        outs.append(acc[0:MLA_V_DIM, :] / acc[MLA_V_DIM:MLA_V_DIM + 1, :])
    o_ref[...] = jnp.concatenate(outs, axis=0).T.astype(BF16)


def _attn_call(qcat, kcat, vt, batch, seq):
    n = qcat.shape[0]
    tq = min(ATTN_Q_TILE, seq)
    tk = min(ATTN_K_TILE, seq)
    nq = seq // tq
    pairs = MLA_HEADS // 2
    return pl.pallas_call(
        functools.partial(_attn_kernel, tk=tk, nk=seq // tk),
        grid=(batch, pairs, nq),
        in_specs=[
            pl.BlockSpec((tq, 2 * HEAD_PAD), lambda b, p, i: (b * nq + i, p)),
            pl.BlockSpec((seq, 2 * HEAD_PAD), lambda b, p, i: (b, p)),
            pl.BlockSpec((None, seq // VT_KEYS, 2 * HEAD_PAD, VT_KEYS), lambda b, p, i: (b, 0, p, 0)),
        ],
        out_specs=pl.BlockSpec((tq, 2 * MLA_V_DIM), lambda b, p, i: (b * nq + i, p)),
        out_shape=jax.ShapeDtypeStruct((n, MLA_HEADS * MLA_V_DIM), BF16),
        scratch_shapes=[pltpu.VMEM((2, 2, tk, tq), F32), pltpu.VMEM((2, 2, 1, tq), F32),
                        pltpu.VMEM((2, 1, tq), F32), pltpu.VMEM((2, HEAD_PAD, tq), F32)],
        compiler_params=pltpu.CompilerParams(dimension_semantics=("arbitrary",) * 3,
                                             vmem_limit_bytes=VMEM_LIMIT_BYTES),
        name="mla_attn",
    )(qcat, kcat, vt)


def _ret_kernel(lg_ref, q_ref, k_ref, v_ref, rg_ref, gn_ref, o_ref, rb_scr, *, chunk, nchunks):
    c_len = chunk
    pair = pl.program_id(1)
    lane = lax.broadcasted_iota(jnp.int32, (c_len, LANES), 1)
    pos = lax.broadcasted_iota(jnp.int32, (c_len, LANES), 0).astype(F32)
    ii = lax.broadcasted_iota(jnp.int32, (c_len, c_len), 0)
    jj = lax.broadcasted_iota(jnp.int32, (c_len, c_len), 1)
    diff = (ii - jj).astype(F32)
    sq = (LANES, LANES)
    two = range(2)
    lgf = [lg_ref[0, 2 * pair + hh] for hh in two]
    lgb = [lg_ref[1, 2 * pair + hh] for hh in two]
    mask = [(lane // RET_QK_DIM) == hh for hh in two]
    vsl = [slice(RET_V_DIM * hh, RET_V_DIM * (hh + 1)) for hh in two]
    zeta_f = [jnp.exp((c_len - 1.0 - pos) * lgf[hh]) for hh in two]
    xi_f = [jnp.exp((pos + 1.0) * lgf[hh]) for hh in two]
    zeta_b = [jnp.exp(pos * lgb[hh]) for hh in two]
    xi_b = [jnp.exp((c_len - pos) * lgb[hh]) for hh in two]
    gchunk_f = [jnp.exp(jnp.full(sq, c_len, F32) * lgf[hh]) for hh in two]
    gchunk_b = [jnp.exp(jnp.full(sq, c_len, F32) * lgb[hh]) for hh in two]
    decay = [jnp.where(diff >= 0.0, jnp.exp(jnp.maximum(diff, 0.0) * lgf[hh]),
                       jnp.exp(jnp.maximum(-diff, 0.0) * lgb[hh])) for hh in two]
    gn = gn_ref[...]

    def chunk_slice(c):
        return pl.ds(pl.multiple_of(c * c_len, c_len), c_len)

    def bwd_body(t, states):
        c = nchunks - 1 - t
        rows = chunk_slice(c)
        k = k_ref[rows, :].astype(F32)
        new = []
        for hh in two:
            rb_scr[hh, c] = states[hh].astype(BF16)
            kz = (jnp.where(mask[hh], k, 0.0) * zeta_b[hh]).astype(BF16)
            new.append(gchunk_b[hh] * states[hh] + _dot_tn(kz, v_ref[rows, vsl[hh]]))
        return tuple(new)

    zero_states = (jnp.zeros(sq, F32), jnp.zeros(sq, F32))
    lax.fori_loop(0, nchunks, bwd_body, zero_states)

    def fwd_body(c, states):
        rows = chunk_slice(c)
        q = q_ref[rows, :].astype(F32)
        k16 = k_ref[rows, :]
        k = k16.astype(F32)
        qm = [jnp.where(mask[hh], q, 0.0) for hh in two]
        inner = [_dot_nt(qm[hh].astype(BF16), k16) for hh in two]
        ys = []
        for hh in two:
            lhs = jnp.concatenate([(inner[hh] * decay[hh]).astype(BF16), (qm[hh] * xi_f[hh]).astype(BF16),
                                   (qm[hh] * xi_b[hh]).astype(BF16)], axis=1)
            rhs = jnp.concatenate([v_ref[rows, vsl[hh]], states[hh].astype(BF16), rb_scr[hh, c]], axis=0)
            ys.append(_dot(lhs, rhs))
        new = []
        for hh in two:
            y = ys[hh]
            mu = jnp.mean(y, axis=-1, keepdims=True)
            yc = y - mu
            var = jnp.mean(yc * yc, axis=-1, keepdims=True)
            yn = yc * lax.rsqrt(var + GN_EPS) * gn[:, vsl[hh]]
            rg = rg_ref[rows, vsl[hh]].astype(F32)
            o_ref[rows, vsl[hh]] = (rg * _sigmoid(rg) * yn).astype(BF16)
            kz = (jnp.where(mask[hh], k, 0.0) * zeta_f[hh]).astype(BF16)
            new.append(gchunk_f[hh] * states[hh] + _dot_tn(kz, v_ref[rows, vsl[hh]]))
        return tuple(new)

    lax.fori_loop(0, nchunks, fwd_body, zero_states)


def _ret_call(log_gamma, rq, rk, rv, rg, gn_g, batch, seq):
    n = rq.shape[0]
    chunk = min(RET_CHUNK, seq)
    nchunks = seq // chunk
    qk_blk = pl.BlockSpec((seq, LANES), lambda b, p: (b, p))
    v_blk = pl.BlockSpec((seq, 2 * RET_V_DIM), lambda b, p: (b, p))
    return pl.pallas_call(
        functools.partial(_ret_kernel, chunk=chunk, nchunks=nchunks),
        grid=(batch, RET_HEADS // 2),
        in_specs=[pl.BlockSpec(memory_space=pltpu.SMEM), qk_blk, qk_blk, v_blk, v_blk,
                  pl.BlockSpec((1, 2 * RET_V_DIM), lambda b, p: (0, p))],
        out_specs=v_blk,
        out_shape=jax.ShapeDtypeStruct((n, RET_V_WIDTH), BF16),
        scratch_shapes=[pltpu.VMEM((2, nchunks, LANES, LANES), BF16)],
        compiler_params=pltpu.CompilerParams(dimension_semantics=("arbitrary",) * 2,
                                             vmem_limit_bytes=VMEM_LIMIT_BYTES),
        name="retention",
    )(log_gamma, rq, rk, rv, rg, gn_g)


def _post_kernel(o_ref, r_ref, ga_ref, gb_ref, x_ref, wa_ref, wb_ref, wo_ref, gffn_ref, wr_hi_lo_ref, wr_hi_ref,
                 x2_out, xn_out, aff_out, afft_out):
    a = _dot(o_ref[...], wa_ref[...])
    r = _dot(r_ref[...], wb_ref[...])
    mixed = _sigmoid(ga_ref[...].astype(F32)) * a + _sigmoid(gb_ref[...].astype(F32)) * r
    x2 = x_ref[...] + _dot(mixed.astype(BF16), wo_ref[...])
    x2_out[...] = x2
    xn = _rms(x2, gffn_ref[...])
    xn_hi = xn.astype(BF16)
    xn_out[...] = xn_hi
    xn_lo = (xn - xn_hi.astype(F32)).astype(BF16)
    t = _dot(xn_hi, wr_hi_lo_ref[...]) + _dot(xn_lo, wr_hi_ref[...])
    logits = t + pltpu.roll(t, LANES - N_EXPERTS, 1)
    lane = lax.broadcasted_iota(jnp.int32, logits.shape, 1)
    logits = jnp.where(lane < N_EXPERTS, logits, -jnp.inf)
    e = jnp.exp(logits - jnp.max(logits, axis=-1, keepdims=True))
    aff = e / jnp.sum(e, axis=-1, keepdims=True)
    aff_out[...] = aff
    afft_out[...] = aff.T[0:N_EXPERTS, :]


def _post_call(o, r, ga, gb, x2d, weights):
    n = x2d.shape[0]
    tm = TOKEN_TILE
    row = lambda width: pl.BlockSpec((tm, width), lambda i: (i, 0))
    return pl.pallas_call(
        _post_kernel,
        grid=(n // tm,),
        in_specs=[row(MLA_HEADS * MLA_V_DIM), row(RET_V_WIDTH), row(D_MODEL), row(D_MODEL), row(D_MODEL)]
        + [_const_spec(w.shape) for w in weights],
        out_specs=[row(D_MODEL), row(D_MODEL), row(LANES), pl.BlockSpec((None, N_EXPERTS, tm), lambda i: (i, 0, 0))],
        out_shape=[jax.ShapeDtypeStruct((n, D_MODEL), F32), jax.ShapeDtypeStruct((n, D_MODEL), BF16),
                   jax.ShapeDtypeStruct((n, LANES), F32), jax.ShapeDtypeStruct((n // tm, N_EXPERTS, tm), F32)],
        compiler_params=pltpu.CompilerParams(dimension_semantics=("arbitrary",), vmem_limit_bytes=VMEM_LIMIT_BYTES),
        name="post_mix",
    )(o, r, ga, gb, x2d, *weights)


def _ffn_kernel(xe_ref, wg_ref, wu_ref, wd_ref, ye_ref):
    x = xe_ref[...]
    g = _dot(x, wg_ref[...])
    u = _dot(x, wu_ref[...])
    hid = (g * _sigmoid(g) * u).astype(BF16)
    ye_ref[...] = _dot(hid, wd_ref[...]).astype(BF16)


def _ffn_call(xe, wg, wu, wd):
    n_exp, cap, d = xe.shape
    tm = min(FFN_ROW_TILE, cap)
    ff = wg.shape[-1]
    return pl.pallas_call(
        _ffn_kernel,
        grid=(n_exp, cap // tm),
        in_specs=[
            pl.BlockSpec((None, tm, d), lambda e, i: (e, i, 0)),
            pl.BlockSpec((None, d, ff), lambda e, i: (e, 0, 0)),
            pl.BlockSpec((None, d, ff), lambda e, i: (e, 0, 0)),
            pl.BlockSpec((None, ff, d), lambda e, i: (e, 0, 0)),
        ],
        out_specs=pl.BlockSpec((None, tm, d), lambda e, i: (e, i, 0)),
        out_shape=jax.ShapeDtypeStruct((n_exp, cap, d), BF16),
        compiler_params=pltpu.CompilerParams(dimension_semantics=("arbitrary",) * 2,
                                             vmem_limit_bytes=VMEM_LIMIT_BYTES),
        name="expert_ffn",
    )(xe, wg, wu, wd)


def _route_kernel(afft_ref, code_ref, code_by_expert_ref, starts_ref, *, cap):
    nt, ne, t = afft_ref.shape
    bits = jnp.maximum(pltpu.bitcast(afft_ref[...], jnp.int32), 0)
    idx = (lax.broadcasted_iota(jnp.int32, (nt, ne, t), 0) * t + lax.broadcasted_iota(jnp.int32, (nt, ne, t), 2))
    capf = jnp.float32(cap)

    def count(flag):
        per_lane = jnp.sum(flag.astype(F32), axis=0, keepdims=True)
        return jnp.sum(per_lane, axis=2, keepdims=True)

    def thr_body(i, thr):
        cand = thr | jnp.left_shift(jnp.int32(1), 30 - i)
        return jnp.where(count(bits >= cand) >= capf, cand, thr)

    thr = lax.fori_loop(0, 31, thr_body, jnp.zeros((1, ne, 1), jnp.int32))
    above = bits > thr
    tied = bits == thr
    need = capf - count(above)
    nbits = (nt * t - 1).bit_length()

    def idx_body(i, last):
        cand = last | jnp.left_shift(jnp.int32(1), nbits - 1 - i)
        return jnp.where(count(tied & (idx < cand)) <= need - 1.0, cand, last)

    last = lax.fori_loop(0, nbits, idx_body, jnp.zeros((1, ne, 1), jnp.int32))
    sel = (above | (tied & (idx <= last))).astype(F32)

    row = lax.broadcasted_iota(jnp.int32, (t, t), 0)
    col = lax.broadcasted_iota(jnp.int32, (t, t), 1)
    earlier = (row < col).astype(BF16)
    rank = _dot(sel.reshape(nt * ne, t).astype(BF16), earlier).reshape(nt, ne, t)
    code_ref[...] = jnp.where(sel > 0.0, rank, -1.0)
    for e in range(ne):
        code_by_expert_ref[e] = code_ref[:, e, :]
    per_tile = jnp.sum(sel, axis=2, keepdims=True)
    run = jnp.zeros((ne, 1), F32)
    for b in range(nt):
        starts_ref[b] = run.astype(jnp.int32)
        run = run + per_tile[b]


def _route_call(afft, cap):
    nt, ne, t = afft.shape
    full = lambda shape: pl.BlockSpec(shape, lambda: (0,) * len(shape))
    code, code_by_expert, starts = pl.pallas_call(
        functools.partial(_route_kernel, cap=cap),
        in_specs=[full((nt, ne, t))],
        out_specs=[full((nt, ne, t)), full((ne, nt, t)), full((nt, ne, 1))],
        out_shape=[jax.ShapeDtypeStruct((nt, ne, t), F32), jax.ShapeDtypeStruct((ne, nt, t), F32),
                   jax.ShapeDtypeStruct((nt, ne, 1), jnp.int32)],
        compiler_params=pltpu.CompilerParams(vmem_limit_bytes=VMEM_LIMIT_BYTES),
        name="route",
    )(afft)
    return code, code_by_expert, starts.reshape(nt, ne)


def _gather_kernel(starts_ref, code_ref, xn_hbm, xe_ref, xbuf, sem, first_scr, acc_scr, *, cap, nt):
    e = pl.program_id(0)
    j = pl.program_id(1)
    rows, t = xe_ref.shape[0], ROUTE_TILE
    lo = j * rows
    hi = lo + rows

    def seg_start(b):
        return starts_ref[jnp.minimum(b, nt - 1), e]

    def seg_end(b):
        return jnp.where(b + 1 < nt, seg_start(b + 1), cap)

    def feeds(b):
        return jnp.logical_and(b < nt, seg_start(b) < hi)

    def tile_copy(b, slot):
        return pltpu.make_async_copy(xn_hbm.at[pl.ds(pl.multiple_of(b * t, t), t), :], xbuf.at[slot], sem.at[slot])

    @pl.when(j == 0)
    def _():
        first_scr[0] = 0

    first = lax.while_loop(lambda b: seg_end(b) <= lo, lambda b: b + 1, first_scr[0])
    first_scr[0] = first
    tile_copy(first, 0).start()
    acc_scr[...] = jnp.zeros(acc_scr.shape, F32)
    slot_row = lax.broadcasted_iota(jnp.int32, (rows, t), 0).astype(F32)

    def body(state):
        b, k = state
        slot = k % 2
        tile_copy(b, slot).wait()

        @pl.when(feeds(b + 1))
        def _():
            tile_copy(b + 1, 1 - slot).start()

        code = code_ref[pl.ds(b, 1), :]
        pos = code + (seg_start(b) - lo).astype(F32)
        take = jnp.logical_and(code >= 0.0, pos == slot_row).astype(BF16)
        acc_scr[...] += _dot(take, xbuf[slot])
        return b + 1, k + 1

    lax.while_loop(lambda s: feeds(s[0]), body, (first, jnp.int32(0)))
    xe_ref[...] = acc_scr[...].astype(BF16)


def _gather_call(starts, code, xn, cap):
    ne, nt, t = code.shape
    rows = min(FFN_ROW_TILE, cap)
    d = xn.shape[1]
    grid_spec = pltpu.PrefetchScalarGridSpec(
        num_scalar_prefetch=1,
        grid=(ne, cap // rows),
        in_specs=[pl.BlockSpec((None, nt, t), lambda e, j, s: (e, 0, 0)),
                  pl.BlockSpec(memory_space=pl.ANY)],
        out_specs=pl.BlockSpec((None, rows, d), lambda e, j, s: (e, j, 0)),
        scratch_shapes=[pltpu.VMEM((2, t, d), BF16), pltpu.SemaphoreType.DMA((2,)), pltpu.SMEM((1,), jnp.int32),
                        pltpu.VMEM((rows, d), F32)],
    )
    return pl.pallas_call(
        functools.partial(_gather_kernel, cap=cap, nt=nt),
        grid_spec=grid_spec,
        out_shape=jax.ShapeDtypeStruct((ne, cap, d), BF16),
        compiler_params=pltpu.CompilerParams(dimension_semantics=("arbitrary",) * 2,
                                             vmem_limit_bytes=VMEM_LIMIT_BYTES),
        name="gather",
    )(starts, code, xn)


def _combine_kernel(starts_ref, x2_ref, aff_ref, code_ref, gfin_ref, ye_hbm, o_ref, buf, sem, *, cap):
    b = pl.program_id(0)
    nb = pl.num_programs(0)
    slot = b % 2
    t = ROUTE_TILE

    def win_start(tile, e):
        aligned = jnp.bitwise_and(starts_ref[tile, e], -WINDOW_ALIGN)
        return pl.multiple_of(jnp.minimum(aligned, cap - WINDOW_ROWS), WINDOW_ALIGN)

    def win_copy(sl, e, tile):
        return pltpu.make_async_copy(ye_hbm.at[e, pl.ds(win_start(tile, e), WINDOW_ROWS), :], buf.at[sl, e],
                                     sem.at[sl, e])

    @pl.when(b == 0)
    def _():
        for e in range(N_EXPERTS):
            win_copy(0, e, 0).start()

    @pl.when(b + 1 < nb)
    def _():
        for e in range(N_EXPERTS):
            win_copy(1 - slot, e, b + 1).start()

    row = lax.broadcasted_iota(jnp.int32, (t, t), 0)
    col = lax.broadcasted_iota(jnp.int32, (t, t), 1)
    ranks = _dot_nt((row == col).astype(BF16), code_ref[...].astype(BF16))
    aff = aff_ref[...]
    colf = col.astype(F32)

    for e in range(N_EXPERTS):
        win_copy(slot, e, b).wait()

    acc = jnp.zeros((t, D_MODEL), F32)
    tail = jnp.zeros((t, t), F32)
    for e in range(N_EXPERTS):
        rank = ranks[:, e:e + 1]
        pos = rank + (starts_ref[b, e] - win_start(b, e)).astype(F32)
        g = jnp.where(rank >= 0.0, aff[:, e:e + 1], 0.0)
        acc = acc + _dot(jnp.where(pos == colf, g, 0.0).astype(BF16), buf[slot, e, 0:t, :])
        in_tail = jnp.logical_and(pos >= float(t), pos - float(t - WINDOW_ALIGN * e) == colf)
        tail = tail + jnp.where(in_tail, g, 0.0)
    tail_rows = jnp.concatenate([buf[slot, e, t:WINDOW_ROWS, :] for e in range(N_EXPERTS)], axis=0)
    acc = acc + _dot(tail.astype(BF16), tail_rows)
    o_ref[...] = _rms(x2_ref[...] + acc, gfin_ref[...])


def _combine_call(starts, x2, aff, code, final_g, ye, cap):
    nt, ne, t = code.shape
    n, d = x2.shape
    assert WINDOW_ALIGN * ne == t and cap >= WINDOW_ROWS and cap % WINDOW_ALIGN == 0
    grid_spec = pltpu.PrefetchScalarGridSpec(
        num_scalar_prefetch=1,
        grid=(nt,),
        in_specs=[pl.BlockSpec((t, d), lambda b, s: (b, 0)),
                  pl.BlockSpec((t, LANES), lambda b, s: (b, 0)),
                  pl.BlockSpec((None, ne, t), lambda b, s: (b, 0, 0)),
                  pl.BlockSpec((1, d), lambda b, s: (0, 0)),
                  pl.BlockSpec(memory_space=pl.ANY)],
        out_specs=pl.BlockSpec((t, d), lambda b, s: (b, 0)),
        scratch_shapes=[pltpu.VMEM((2, ne, WINDOW_ROWS, d), BF16), pltpu.SemaphoreType.DMA((2, ne))],
    )
    return pl.pallas_call(
        functools.partial(_combine_kernel, cap=cap),
        grid_spec=grid_spec,
        out_shape=jax.ShapeDtypeStruct((n, d), F32),
        compiler_params=pltpu.CompilerParams(dimension_semantics=("arbitrary",), vmem_limit_bytes=VMEM_LIMIT_BYTES),
        name="combine",
    )(starts, x2, aff, code, final_g, ye)


def _rope_angles(seq, dim):
    inv = 1.0 / (ROPE_THETA ** (jnp.arange(0, dim, 2, dtype=F32) / dim))
    ang = jnp.arange(seq, dtype=F32)[:, None] * inv[None, :]
    return jnp.cos(ang), jnp.sin(ang)


def _mla_tables(seq, scale):
    c, s = _rope_angles(seq, MLA_ROPE_DIM)
    ones = jnp.ones((seq, MLA_NOPE_DIM), F32)
    zeros_n = jnp.zeros((seq, MLA_NOPE_DIM), F32)
    zeros_p = jnp.zeros((seq, HEAD_PAD - MLA_QK_DIM), F32)
    cos = jnp.concatenate([ones, c, c, zeros_p], axis=1) * scale
    sin = jnp.concatenate([zeros_n, -s, s, zeros_p], axis=1) * scale
    return cos, sin


def _ret_tables(seq, scale):
    c, s = _rope_angles(seq, RET_QK_DIM)
    cos = jnp.concatenate([c, c, c, c], axis=1) * scale
    sin = jnp.concatenate([-s, s, -s, s], axis=1) * scale
    return cos, sin


def _prepare_weights(norm_mix_g, w_in, q_norm_g, w_uq, kv_norm_g, w_ukv, ret_gn_g, w_branch_a, w_branch_b, w_out,
                     norm_ffn_g, w_router, norm_final_g):
    offs, acc = [], 0
    for width in IN_SPLITS:
        offs.append((acc, acc + width))
        acc += width
    w_cq, w_ckv, w_kr, w_rq, w_rk, w_rv, w_rg, w_ga, w_gb = [w_in[:, a:b].astype(BF16) for a, b in offs]
    w_kr = jnp.pad(w_kr, ((0, 0), (0, LANES - MLA_ROPE_DIM)))

    uq = w_uq.reshape(Q_LORA_RANK, MLA_HEADS, MLA_QK_DIM)
    wq_pad = jnp.pad(uq, ((0, 0), (0, 0), (0, HEAD_PAD - MLA_QK_DIM))).reshape(Q_LORA_RANK, MLA_PAD_WIDTH)
    ukv = w_ukv.reshape(KV_LORA_RANK, MLA_HEADS, MLA_NOPE_DIM + MLA_V_DIM)
    wk_pad = jnp.pad(ukv[:, :, :MLA_NOPE_DIM], ((0, 0), (0, 0), (0, HEAD_PAD - MLA_NOPE_DIM)))
    wk_pad = wk_pad.reshape(KV_LORA_RANK, MLA_PAD_WIDTH)
    wv = jnp.pad(ukv[:, :, MLA_NOPE_DIM:], ((0, 0), (0, 0), (0, HEAD_PAD - MLA_V_DIM)))
    wv = wv.reshape(KV_LORA_RANK, MLA_PAD_WIDTH).T
    src = jnp.arange(LANES)[:, None]
    dst = jnp.arange(MLA_PAD_WIDTH)[None, :]
    place = ((dst % HEAD_PAD) - MLA_NOPE_DIM == src) & (src < MLA_ROPE_DIM)

    pre_w = (norm_mix_g.reshape(1, -1), w_cq, w_ckv, w_kr, w_rq, w_rk, w_rv, w_rg, w_ga, w_gb,
             q_norm_g.reshape(1, -1), wq_pad.astype(BF16), kv_norm_g.reshape(1, -1), wk_pad.astype(BF16),
             place.astype(BF16), wv.astype(BF16))

    wr_hi = w_router.astype(BF16)
    wr_lo = (w_router - wr_hi.astype(F32)).astype(BF16)
    pad_to = lambda w: jnp.pad(w, ((0, 0), (0, LANES - w.shape[1])))
    post_w = (w_branch_a.astype(BF16), w_branch_b.astype(BF16), w_out.astype(BF16), norm_ffn_g.reshape(1, -1),
              pad_to(jnp.concatenate([wr_hi, wr_lo], axis=1)), pad_to(wr_hi))
    return pre_w, post_w, ret_gn_g.reshape(1, -1), norm_final_g.reshape(1, -1)


def _encode(x, pre_w, post_w, log_gamma, gn_g, final_g, wg, wu, wd):
    batch, seq, d = x.shape
    n = batch * seq
    x2d = x.reshape(n, d)
    tables = (*_mla_tables(seq, MLA_QK_DIM ** -0.5 * LOG2_E), *_mla_tables(seq, 1.0),
              *_ret_tables(seq, 1.0), *_ret_tables(seq, RET_QK_DIM ** -0.5))
    qcat, kcat, vt, rq, rk, rv, rg, ga, gb = _pre_call(x2d, seq, pre_w, tables)
    o = _attn_call(qcat, kcat, vt, batch, seq)
    r = _ret_call(log_gamma, rq, rk, rv, rg, gn_g, batch, seq)
    x2, xn, aff, afft = _post_call(o, r, ga, gb, x2d, post_w)

    cap = CAPACITY_FACTOR * n // N_EXPERTS
    code, code_by_expert, starts = _route_call(afft, cap)
    xe = _gather_call(starts, code_by_expert, xn, cap)
    ye = _ffn_call(xe, wg, wu, wd)
    return _combine_call(starts, x2, aff, code, final_g, ye, cap).reshape(batch, seq, d)


def kernel(x_prompt, x_sample, norm_mix_g, w_in, q_norm_g, w_uq, kv_norm_g, w_ukv, ret_decay_fwd, ret_decay_bwd,
           ret_gn_g, w_branch_a, w_branch_b, w_out, norm_ffn_g, w_router, w_exp_gate, w_exp_up, w_exp_down,
           norm_final_g):
    assert norm_mix_g.shape[0] == 1, "single-layer trunk"
    pre_w, post_w, gn_g, final_g = _prepare_weights(
        norm_mix_g[0], w_in[0], q_norm_g[0], w_uq[0], kv_norm_g[0], w_ukv[0], ret_gn_g[0], w_branch_a[0],
        w_branch_b[0], w_out[0], norm_ffn_g[0], w_router[0], norm_final_g)
    log_gamma = jnp.stack([jax.nn.log_sigmoid(ret_decay_fwd[0].astype(F32)),
                           jax.nn.log_sigmoid(ret_decay_bwd[0].astype(F32))])
    wg = w_exp_gate[0].astype(BF16)
    wu = w_exp_up[0].astype(BF16)
    wd = w_exp_down[0].astype(BF16)
    enc = functools.partial(_encode, pre_w=pre_w, post_w=post_w, log_gamma=log_gamma, gn_g=gn_g, final_g=final_g,
                            wg=wg, wu=wu, wd=wd)
    return enc(x_prompt), enc(x_sample)
```

```python
import functools

import jax
import jax.numpy as jnp
from jax import lax
from jax.experimental import pallas as pl
from jax.experimental.pallas import tpu as pltpu

D_MODEL = 1024
MLA_HEADS = 8
MLA_NOPE_DIM = 64
MLA_ROPE_DIM = 32
MLA_V_DIM = 64
MLA_QK_DIM = MLA_NOPE_DIM + MLA_ROPE_DIM
Q_LORA_RANK = 384
KV_LORA_RANK = 256
RET_HEADS = 8
RET_QK_DIM = 64
RET_V_DIM = 128
RET_QK_WIDTH = RET_HEADS * RET_QK_DIM
RET_V_WIDTH = RET_HEADS * RET_V_DIM
N_EXPERTS = 16
EXPERT_FF = 2816
CAPACITY_FACTOR = 2
ROPE_THETA = 10000.0
EPS = 1e-6
GN_EPS = 1e-5
IN_SPLITS = (Q_LORA_RANK, KV_LORA_RANK, MLA_ROPE_DIM, RET_QK_WIDTH, RET_QK_WIDTH, RET_V_WIDTH, RET_V_WIDTH,
             D_MODEL, D_MODEL)

LANES = 128
HEAD_PAD = LANES
MLA_PAD_WIDTH = MLA_HEADS * HEAD_PAD
VMEM_LIMIT_BYTES = 56 * 1024 * 1024

TOKEN_TILE = 256
VT_KEYS = TOKEN_TILE
LOG2_E = 1.4426950408889634
ROUTE_TILE = TOKEN_TILE
WINDOW_ALIGN = 16
WINDOW_ROWS = ROUTE_TILE + WINDOW_ALIGN
ATTN_Q_TILE = 512
ATTN_K_TILE = 512
RET_CHUNK = 256
FFN_ROW_TILE = 256

F32 = jnp.float32
BF16 = jnp.bfloat16


def _dot(a, b):
    return jnp.dot(a, b, preferred_element_type=F32)


def _dot_nt(a, b):
    return lax.dot_general(a, b, (((1,), (1,)), ((), ())), preferred_element_type=F32)


def _dot_tn(a, b):
    return lax.dot_general(a, b, (((0,), (0,)), ((), ())), preferred_element_type=F32)


def _rms(x, g):
    return x * lax.rsqrt(jnp.mean(x * x, axis=-1, keepdims=True) + EPS) * g


def _sigmoid(x):
    return 1.0 / (1.0 + jnp.exp(-x))


def _rope_lanes(blk, cos, sin, half):
    lane = lax.broadcasted_iota(jnp.int32, blk.shape, 1)
    upper = (lane % (2 * half)) >= half
    partner = jnp.where(upper, pltpu.roll(blk, half, 1), pltpu.roll(blk, LANES - half, 1))
    return blk * cos + partner * sin


def _const_spec(shape):
    nd = len(shape)
    return pl.BlockSpec(shape, lambda *_: (0,) * nd, pipeline_mode=pl.Buffered(1))


def _pre_kernel(x_ref, gmix_ref, wcq_ref, wckv_ref, wkr_ref, wrq_ref, wrk_ref, wrv_ref, wrg_ref, wga_ref, wgb_ref,
                gq_ref, wq_ref, gkv_ref, wk_ref, place_ref, wv_ref,
                cosq_ref, sinq_ref, cosk_ref, sink_ref, cosrq_ref, sinrq_ref, cosrk_ref, sinrk_ref,
                q_out, k_out, vt_out, rq_out, rk_out, rv_out, rg_out, ga_out, gb_out):
    h = _rms(x_ref[...], gmix_ref[...]).astype(BF16)

    qn = _rms(_dot(h, wcq_ref[...]), gq_ref[...]).astype(BF16)
    q = _dot(qn, wq_ref[...])
    cosq, sinq = cosq_ref[...], sinq_ref[...]
    for j in range(MLA_HEADS):
        sl = slice(HEAD_PAD * j, HEAD_PAD * (j + 1))
        q_out[:, sl] = _rope_lanes(q[:, sl], cosq, sinq, MLA_ROPE_DIM // 2).astype(BF16)

    kvn = _rms(_dot(h, wckv_ref[...]), gkv_ref[...]).astype(BF16)
    kr = _dot(h, wkr_ref[...]).astype(BF16)
    k = _dot(kvn, wk_ref[...]) + _dot(kr, place_ref[...])
    cosk, sink = cosk_ref[...], sink_ref[...]
    for j in range(MLA_HEADS):
        sl = slice(HEAD_PAD * j, HEAD_PAD * (j + 1))
        k_out[:, sl] = _rope_lanes(k[:, sl], cosk, sink, MLA_ROPE_DIM // 2).astype(BF16)
    vt = _dot_nt(wv_ref[...], kvn)
    vrow = lax.broadcasted_iota(jnp.int32, vt.shape, 0)
    vt_out[...] = jnp.where(vrow % HEAD_PAD == MLA_V_DIM, 1.0, vt).astype(BF16)

    rq = _dot(h, wrq_ref[...])
    rk = _dot(h, wrk_ref[...])
    cosrq, sinrq, cosrk, sinrk = cosrq_ref[...], sinrq_ref[...], cosrk_ref[...], sinrk_ref[...]
    for j in range(RET_QK_WIDTH // LANES):
        sl = slice(LANES * j, LANES * (j + 1))
        rq_out[:, sl] = _rope_lanes(rq[:, sl], cosrq, sinrq, RET_QK_DIM // 2).astype(BF16)
        rk_out[:, sl] = _rope_lanes(rk[:, sl], cosrk, sinrk, RET_QK_DIM // 2).astype(BF16)

    rv_out[...] = _dot(h, wrv_ref[...]).astype(BF16)
    rg_out[...] = _dot(h, wrg_ref[...]).astype(BF16)
    ga_out[...] = _dot(h, wga_ref[...]).astype(BF16)
    gb_out[...] = _dot(h, wgb_ref[...]).astype(BF16)


def _pre_call(x2d, seq, weights, tables):
    n = x2d.shape[0]
    tm = TOKEN_TILE
    tiles_per_seq = seq // tm
    row = lambda width: pl.BlockSpec((tm, width), lambda i: (i, 0))
    tab = pl.BlockSpec((tm, LANES), lambda i: (i % tiles_per_seq, 0))
    out_widths = (MLA_PAD_WIDTH, MLA_PAD_WIDTH, None, RET_QK_WIDTH, RET_QK_WIDTH,
                  RET_V_WIDTH, RET_V_WIDTH, D_MODEL, D_MODEL)
    vt_spec = pl.BlockSpec((None, None, MLA_PAD_WIDTH, tm), lambda i: (i // tiles_per_seq, i % tiles_per_seq, 0, 0))
    vt_shape = jax.ShapeDtypeStruct((n // seq, tiles_per_seq, MLA_PAD_WIDTH, tm), BF16)
    return pl.pallas_call(
        _pre_kernel,
        grid=(n // tm,),
        in_specs=[row(D_MODEL)] + [_const_spec(w.shape) for w in weights] + [tab] * len(tables),
        out_specs=[vt_spec if w is None else row(w) for w in out_widths],
        out_shape=[vt_shape if w is None else jax.ShapeDtypeStruct((n, w), BF16) for w in out_widths],
        compiler_params=pltpu.CompilerParams(dimension_semantics=("arbitrary",), vmem_limit_bytes=VMEM_LIMIT_BYTES),
        name="pre_proj",
    )(x2d, *weights, *tables)


def _attn_kernel(q_ref, k_ref, vt_ref, o_ref, st_scr, mc_scr, m_scr, acc_scr, *, tk, nk):
    sub = tk // VT_KEYS
    heads = [slice(HEAD_PAD * hh, HEAD_PAD * (hh + 1)) for hh in range(2)]

    def scores(kb, buf):
        off = pl.multiple_of(kb * tk, tk)
        for hh, sl in enumerate(heads):
            st = _dot_nt(k_ref[pl.ds(off, tk), sl], q_ref[:, sl])
            st_scr[buf, hh] = st
            mc_scr[buf, hh] = jnp.max(st, axis=0, keepdims=True)

    def accumulate(kb, buf):
        for hh, sl in enumerate(heads):
            m_old = m_scr[hh]
            m_new = jnp.maximum(m_old, mc_scr[buf, hh])
            alpha = jnp.exp2(m_old - m_new)
            pt = jnp.exp2(st_scr[buf, hh] - m_new).astype(BF16)
            pv = _dot(vt_ref[kb * sub, sl, :], pt[0:VT_KEYS, :])
            for j in range(1, sub):
                pv = pv + _dot(vt_ref[kb * sub + j, sl, :], pt[VT_KEYS * j:VT_KEYS * (j + 1), :])
            acc_scr[hh] = alpha * acc_scr[hh] + pv
            m_scr[hh] = m_new

    m_scr[...] = jnp.full(m_scr.shape, -jnp.inf, F32)
    acc_scr[...] = jnp.zeros(acc_scr.shape, F32)
    scores(0, 0)
    if nk > 1:
        assert nk % 2 == 0

        def body(j, carry):
            kb = 2 * j
            scores(kb + 1, 1)
            accumulate(kb, 0)
            scores(kb + 2, 0)
            accumulate(kb + 1, 1)
            return carry

        lax.fori_loop(0, nk // 2 - 1, body, 0)
        scores(nk - 1, 1)
        accumulate(nk - 2, 0)
        accumulate(nk - 1, 1)
    else:
        accumulate(0, 0)
    outs = []
    for hh in range(2):
        acc = acc_scr[hh]
        outs.append(acc[0:MLA_V_DIM, :] / acc[MLA_V_DIM:MLA_V_DIM + 1, :])
    o_ref[...] = jnp.concatenate(outs, axis=0).T.astype(BF16)


def _attn_call(qcat, kcat, vt, batch, seq):
    n = qcat.shape[0]
    tq = min(ATTN_Q_TILE, seq)
    tk = min(ATTN_K_TILE, seq)
    nq = seq // tq
    pairs = MLA_HEADS // 2
    return pl.pallas_call(
        functools.partial(_attn_kernel, tk=tk, nk=seq // tk),
        grid=(batch, pairs, nq),
        in_specs=[
            pl.BlockSpec((tq, 2 * HEAD_PAD), lambda b, p, i: (b * nq + i, p)),
            pl.BlockSpec((seq, 2 * HEAD_PAD), lambda b, p, i: (b, p)),
            pl.BlockSpec((None, seq // VT_KEYS, 2 * HEAD_PAD, VT_KEYS), lambda b, p, i: (b, 0, p, 0)),
        ],
        out_specs=pl.BlockSpec((tq, 2 * MLA_V_DIM), lambda b, p, i: (b * nq + i, p)),
        out_shape=jax.ShapeDtypeStruct((n, MLA_HEADS * MLA_V_DIM), BF16),
        scratch_shapes=[pltpu.VMEM((2, 2, tk, tq), F32), pltpu.VMEM((2, 2, 1, tq), F32),
                        pltpu.VMEM((2, 1, tq), F32), pltpu.VMEM((2, HEAD_PAD, tq), F32)],
        compiler_params=pltpu.CompilerParams(dimension_semantics=("arbitrary",) * 3,
                                             vmem_limit_bytes=VMEM_LIMIT_BYTES),
        name="mla_attn",
    )(qcat, kcat, vt)


def _ret_kernel(lg_ref, q_ref, k_ref, v_ref, rg_ref, gn_ref, o_ref, rb_scr, *, chunk, nchunks):
    c_len = chunk
    pair = pl.program_id(1)
    lane = lax.broadcasted_iota(jnp.int32, (c_len, LANES), 1)
    pos = lax.broadcasted_iota(jnp.int32, (c_len, LANES), 0).astype(F32)
    ii = lax.broadcasted_iota(jnp.int32, (c_len, c_len), 0)
    jj = lax.broadcasted_iota(jnp.int32, (c_len, c_len), 1)
    diff = (ii - jj).astype(F32)
    sq = (LANES, LANES)
    two = range(2)
    lgf = [lg_ref[0, 2 * pair + hh] for hh in two]
    lgb = [lg_ref[1, 2 * pair + hh] for hh in two]
    mask = [(lane // RET_QK_DIM) == hh for hh in two]
    vsl = [slice(RET_V_DIM * hh, RET_V_DIM * (hh + 1)) for hh in two]
    zeta_f = [jnp.exp((c_len - 1.0 - pos) * lgf[hh]) for hh in two]
    xi_f = [jnp.exp((pos + 1.0) * lgf[hh]) for hh in two]
    zeta_b = [jnp.exp(pos * lgb[hh]) for hh in two]
    xi_b = [jnp.exp((c_len - pos) * lgb[hh]) for hh in two]
    gchunk_f = [jnp.exp(jnp.full(sq, c_len, F32) * lgf[hh]) for hh in two]
    gchunk_b = [jnp.exp(jnp.full(sq, c_len, F32) * lgb[hh]) for hh in two]
    decay = [jnp.where(diff >= 0.0, jnp.exp(jnp.maximum(diff, 0.0) * lgf[hh]),
                       jnp.exp(jnp.maximum(-diff, 0.0) * lgb[hh])) for hh in two]
    gn = gn_ref[...]

    def chunk_slice(c):
        return pl.ds(pl.multiple_of(c * c_len, c_len), c_len)

    def bwd_body(t, states):
        c = nchunks - 1 - t
        rows = chunk_slice(c)
        k = k_ref[rows, :].astype(F32)
        new = []
        for hh in two:
            rb_scr[hh, c] = states[hh].astype(BF16)
            kz = (jnp.where(mask[hh], k, 0.0) * zeta_b[hh]).astype(BF16)
            new.append(gchunk_b[hh] * states[hh] + _dot_tn(kz, v_ref[rows, vsl[hh]]))
        return tuple(new)

    zero_states = (jnp.zeros(sq, F32), jnp.zeros(sq, F32))
    lax.fori_loop(0, nchunks, bwd_body, zero_states)

    def fwd_body(c, states):
        rows = chunk_slice(c)
        q = q_ref[rows, :].astype(F32)
        k16 = k_ref[rows, :]
        k = k16.astype(F32)
        qm = [jnp.where(mask[hh], q, 0.0) for hh in two]
        inner = [_dot_nt(qm[hh].astype(BF16), k16) for hh in two]
        ys = []
        for hh in two:
            lhs = jnp.concatenate([(inner[hh] * decay[hh]).astype(BF16), (qm[hh] * xi_f[hh]).astype(BF16),
                                   (qm[hh] * xi_b[hh]).astype(BF16)], axis=1)
            rhs = jnp.concatenate([v_ref[rows, vsl[hh]], states[hh].astype(BF16), rb_scr[hh, c]], axis=0)
            ys.append(_dot(lhs, rhs))
        new = []
        for hh in two:
            y = ys[hh]
            mu = jnp.mean(y, axis=-1, keepdims=True)
            yc = y - mu
            var = jnp.mean(yc * yc, axis=-1, keepdims=True)
            yn = yc * lax.rsqrt(var + GN_EPS) * gn[:, vsl[hh]]
            rg = rg_ref[rows, vsl[hh]].astype(F32)
            o_ref[rows, vsl[hh]] = (rg * _sigmoid(rg) * yn).astype(BF16)
            kz = (jnp.where(mask[hh], k, 0.0) * zeta_f[hh]).astype(BF16)
            new.append(gchunk_f[hh] * states[hh] + _dot_tn(kz, v_ref[rows, vsl[hh]]))
        return tuple(new)

    lax.fori_loop(0, nchunks, fwd_body, zero_states)


def _ret_call(log_gamma, rq, rk, rv, rg, gn_g, batch, seq):
    n = rq.shape[0]
    chunk = min(RET_CHUNK, seq)
    nchunks = seq // chunk
    qk_blk = pl.BlockSpec((seq, LANES), lambda b, p: (b, p))
    v_blk = pl.BlockSpec((seq, 2 * RET_V_DIM), lambda b, p: (b, p))
    return pl.pallas_call(
        functools.partial(_ret_kernel, chunk=chunk, nchunks=nchunks),
        grid=(batch, RET_HEADS // 2),
        in_specs=[pl.BlockSpec(memory_space=pltpu.SMEM), qk_blk, qk_blk, v_blk, v_blk,
                  pl.BlockSpec((1, 2 * RET_V_DIM), lambda b, p: (0, p))],
        out_specs=v_blk,
        out_shape=jax.ShapeDtypeStruct((n, RET_V_WIDTH), BF16),
        scratch_shapes=[pltpu.VMEM((2, nchunks, LANES, LANES), BF16)],
        compiler_params=pltpu.CompilerParams(dimension_semantics=("arbitrary",) * 2,
                                             vmem_limit_bytes=VMEM_LIMIT_BYTES),
        name="retention",
    )(log_gamma, rq, rk, rv, rg, gn_g)


def _post_kernel(o_ref, r_ref, ga_ref, gb_ref, x_ref, wa_ref, wb_ref, wo_ref, gffn_ref, wr_hi_lo_ref, wr_hi_ref,
                 x2_out, xn_out, aff_out, afft_out):
    a = _dot(o_ref[...], wa_ref[...])
    r = _dot(r_ref[...], wb_ref[...])
    mixed = _sigmoid(ga_ref[...].astype(F32)) * a + _sigmoid(gb_ref[...].astype(F32)) * r
    x2 = x_ref[...] + _dot(mixed.astype(BF16), wo_ref[...])
    x2_out[...] = x2
    xn = _rms(x2, gffn_ref[...])
    xn_hi = xn.astype(BF16)
    xn_out[...] = xn_hi
    xn_lo = (xn - xn_hi.astype(F32)).astype(BF16)
    t = _dot(xn_hi, wr_hi_lo_ref[...]) + _dot(xn_lo, wr_hi_ref[...])
    logits = t + pltpu.roll(t, LANES - N_EXPERTS, 1)
    lane = lax.broadcasted_iota(jnp.int32, logits.shape, 1)
    logits = jnp.where(lane < N_EXPERTS, logits, -jnp.inf)
    e = jnp.exp(logits - jnp.max(logits, axis=-1, keepdims=True))
    aff = e / jnp.sum(e, axis=-1, keepdims=True)
    aff_out[...] = aff
    afft_out[...] = aff.T[0:N_EXPERTS, :]


def _post_call(o, r, ga, gb, x2d, weights):
    n = x2d.shape[0]
    tm = TOKEN_TILE
    row = lambda width: pl.BlockSpec((tm, width), lambda i: (i, 0))
    return pl.pallas_call(
        _post_kernel,
        grid=(n // tm,),
        in_specs=[row(MLA_HEADS * MLA_V_DIM), row(RET_V_WIDTH), row(D_MODEL), row(D_MODEL), row(D_MODEL)]
        + [_const_spec(w.shape) for w in weights],
        out_specs=[row(D_MODEL), row(D_MODEL), row(LANES), pl.BlockSpec((None, N_EXPERTS, tm), lambda i: (i, 0, 0))],
        out_shape=[jax.ShapeDtypeStruct((n, D_MODEL), F32), jax.ShapeDtypeStruct((n, D_MODEL), BF16),
                   jax.ShapeDtypeStruct((n, LANES), F32), jax.ShapeDtypeStruct((n // tm, N_EXPERTS, tm), F32)],
        compiler_params=pltpu.CompilerParams(dimension_semantics=("arbitrary",), vmem_limit_bytes=VMEM_LIMIT_BYTES),
        name="post_mix",
    )(o, r, ga, gb, x2d, *weights)


def _ffn_kernel(xe_ref, wg_ref, wu_ref, wd_ref, ye_ref):
    x = xe_ref[...]
    g = _dot(x, wg_ref[...])
    u = _dot(x, wu_ref[...])
    hid = (g * _sigmoid(g) * u).astype(BF16)
    ye_ref[...] = _dot(hid, wd_ref[...]).astype(BF16)


def _ffn_call(xe, wg, wu, wd, cap):
    n_exp, _, d = xe.shape
    tm = min(FFN_ROW_TILE, cap)
    ff = wg.shape[-1]
    return pl.pallas_call(
        _ffn_kernel,
        grid=(n_exp, cap // tm),
        in_specs=[
            pl.BlockSpec((None, tm, d), lambda e, i: (e, i, 0)),
            pl.BlockSpec((None, d, ff), lambda e, i: (e, 0, 0)),
            pl.BlockSpec((None, d, ff), lambda e, i: (e, 0, 0)),
            pl.BlockSpec((None, ff, d), lambda e, i: (e, 0, 0)),
        ],
        out_specs=pl.BlockSpec((None, tm, d), lambda e, i: (e, i, 0)),
        out_shape=jax.ShapeDtypeStruct((n_exp, cap, d), BF16),
        compiler_params=pltpu.CompilerParams(dimension_semantics=("arbitrary",) * 2,
                                             vmem_limit_bytes=VMEM_LIMIT_BYTES),
        name="expert_ffn",
    )(xe, wg, wu, wd)


def _route_kernel(afft_ref, code_ref, starts_ref, *, cap):
    nt, ne, t = afft_ref.shape
    bits = jnp.maximum(pltpu.bitcast(afft_ref[...], jnp.int32), 0)
    idx = (lax.broadcasted_iota(jnp.int32, (nt, ne, t), 0) * t + lax.broadcasted_iota(jnp.int32, (nt, ne, t), 2))
    capf = jnp.float32(cap)

    def count(flag):
        per_lane = jnp.sum(flag.astype(F32), axis=0, keepdims=True)
        return jnp.sum(per_lane, axis=2, keepdims=True)

    def thr_body(i, thr):
        cand = thr | jnp.left_shift(jnp.int32(1), 30 - i)
        return jnp.where(count(bits >= cand) >= capf, cand, thr)

    thr = lax.fori_loop(0, 31, thr_body, jnp.zeros((1, ne, 1), jnp.int32))
    above = bits > thr
    tied = bits == thr
    need = capf - count(above)
    nbits = (nt * t - 1).bit_length()

    def idx_body(i, last):
        cand = last | jnp.left_shift(jnp.int32(1), nbits - 1 - i)
        return jnp.where(count(tied & (idx < cand)) <= need - 1.0, cand, last)

    last = lax.fori_loop(0, nbits, idx_body, jnp.zeros((1, ne, 1), jnp.int32))
    sel = (above | (tied & (idx <= last))).astype(F32)

    row = lax.broadcasted_iota(jnp.int32, (t, t), 0)
    col = lax.broadcasted_iota(jnp.int32, (t, t), 1)
    earlier = (row < col).astype(BF16)
    rank = _dot(sel.reshape(nt * ne, t).astype(BF16), earlier).reshape(nt, ne, t)
    code_ref[...] = jnp.where(sel > 0.0, rank, -1.0)
    per_tile = jnp.sum(sel, axis=2, keepdims=True)
    run = jnp.zeros((ne, 1), F32)
    for b in range(nt):
        starts_ref[b] = run.astype(jnp.int32)
        run = run + per_tile[b]


def _route_call(afft, cap):
    nt, ne, t = afft.shape
    full = lambda shape: pl.BlockSpec(shape, lambda: (0,) * len(shape))
    code, starts = pl.pallas_call(
        functools.partial(_route_kernel, cap=cap),
        in_specs=[full((nt, ne, t))],
        out_specs=[full((nt, ne, t)), full((nt, ne, 1))],
        out_shape=[jax.ShapeDtypeStruct((nt, ne, t), F32), jax.ShapeDtypeStruct((nt, ne, 1), jnp.int32)],
        compiler_params=pltpu.CompilerParams(vmem_limit_bytes=VMEM_LIMIT_BYTES),
        name="route",
    )(afft)
    return code, starts.reshape(nt, ne)


def _window_start(starts_ref, tile, e):
    return pl.multiple_of(jnp.bitwise_and(starts_ref[tile, e], -WINDOW_ALIGN), WINDOW_ALIGN)


def _dispatch_kernel(starts_ref, xn_ref, code_ref, xe_hbm, buf, sem, *, cap):
    b = pl.program_id(0)
    nb = pl.num_programs(0)
    slot = b % 2
    t = ROUTE_TILE

    def win_copy(sl, e, tile):
        return pltpu.make_async_copy(buf.at[sl, e],
                                     xe_hbm.at[e, pl.ds(_window_start(starts_ref, tile, e), WINDOW_ROWS), :],
                                     sem.at[sl, e])

    @pl.when(b == 0)
    def _():
        buf[1] = jnp.zeros(buf.shape[1:], BF16)
        fills = [pltpu.make_async_copy(buf.at[1, e], xe_hbm.at[e, pl.ds(cap, WINDOW_ROWS), :], sem.at[1, e])
                 for e in range(N_EXPERTS)]
        for f in fills:
            f.start()
        for f in fills:
            f.wait()

    code = code_ref[...]
    rowf = lax.broadcasted_iota(jnp.int32, (WINDOW_ROWS, t), 0).astype(F32)
    xn = xn_ref[...]
    prev = jnp.maximum(b - 1, 0)
    for e in range(N_EXPERTS):
        first = _window_start(starts_ref, b, e)
        pos = code[e:e + 1, :] + (starts_ref[b, e] - first).astype(F32)
        take = jnp.logical_and(code[e:e + 1, :] >= 0.0, pos == rowf).astype(BF16)
        win = _dot(take, xn)
        back = pl.multiple_of(first - _window_start(starts_ref, prev, e), WINDOW_ALIGN)
        carry = buf[1 - slot, e, pl.ds(back, WINDOW_ALIGN), :].astype(F32)
        buf[slot, e, 0:WINDOW_ALIGN, :] = (win[0:WINDOW_ALIGN, :] + carry).astype(BF16)
        buf[slot, e, WINDOW_ALIGN:WINDOW_ROWS, :] = win[WINDOW_ALIGN:WINDOW_ROWS, :].astype(BF16)

    @pl.when(b > 0)
    def _():
        for e in range(N_EXPERTS):
            win_copy(1 - slot, e, b - 1).wait()

    for e in range(N_EXPERTS):
        win_copy(slot, e, b).start()

    @pl.when(b == nb - 1)
    def _():
        for e in range(N_EXPERTS):
            win_copy(slot, e, b).wait()


def _dispatch_call(starts, code, xn, cap):
    nt, ne, t = code.shape
    d = xn.shape[1]
    grid_spec = pltpu.PrefetchScalarGridSpec(
        num_scalar_prefetch=1,
        grid=(nt,),
        in_specs=[pl.BlockSpec((t, d), lambda b, s: (b, 0)),
                  pl.BlockSpec((None, ne, t), lambda b, s: (b, 0, 0))],
        out_specs=pl.BlockSpec(memory_space=pl.ANY),
        scratch_shapes=[pltpu.VMEM((2, ne, WINDOW_ROWS, d), BF16), pltpu.SemaphoreType.DMA((2, ne))],
    )
    return pl.pallas_call(
        functools.partial(_dispatch_kernel, cap=cap),
        grid_spec=grid_spec,
        out_shape=jax.ShapeDtypeStruct((ne, cap + WINDOW_ROWS, d), BF16),
        compiler_params=pltpu.CompilerParams(dimension_semantics=("arbitrary",), vmem_limit_bytes=VMEM_LIMIT_BYTES),
        name="dispatch",
    )(starts, xn, code)


def _combine_kernel(starts_ref, x2_ref, aff_ref, code_ref, gfin_ref, ye_hbm, o_ref, buf, sem, *, cap):
    b = pl.program_id(0)
    nb = pl.num_programs(0)
    slot = b % 2
    t = ROUTE_TILE

    def win_start(tile, e):
        aligned = jnp.bitwise_and(starts_ref[tile, e], -WINDOW_ALIGN)
        return pl.multiple_of(jnp.minimum(aligned, cap - WINDOW_ROWS), WINDOW_ALIGN)

    def win_copy(sl, e, tile):
        return pltpu.make_async_copy(ye_hbm.at[e, pl.ds(win_start(tile, e), WINDOW_ROWS), :], buf.at[sl, e],
                                     sem.at[sl, e])

    @pl.when(b == 0)
    def _():
        for e in range(N_EXPERTS):
            win_copy(0, e, 0).start()

    @pl.when(b + 1 < nb)
    def _():
        for e in range(N_EXPERTS):
            win_copy(1 - slot, e, b + 1).start()

    row = lax.broadcasted_iota(jnp.int32, (t, t), 0)
    col = lax.broadcasted_iota(jnp.int32, (t, t), 1)
    ranks = _dot_nt((row == col).astype(BF16), code_ref[...].astype(BF16))
    aff = aff_ref[...]
    colf = col.astype(F32)

    for e in range(N_EXPERTS):
        win_copy(slot, e, b).wait()

    acc = jnp.zeros((t, D_MODEL), F32)
    tail = jnp.zeros((t, t), F32)
    for e in range(N_EXPERTS):
        rank = ranks[:, e:e + 1]
        pos = rank + (starts_ref[b, e] - win_start(b, e)).astype(F32)
        g = jnp.where(rank >= 0.0, aff[:, e:e + 1], 0.0)
        acc = acc + _dot(jnp.where(pos == colf, g, 0.0).astype(BF16), buf[slot, e, 0:t, :])
        in_tail = jnp.logical_and(pos >= float(t), pos - float(t - WINDOW_ALIGN * e) == colf)
        tail = tail + jnp.where(in_tail, g, 0.0)
    tail_rows = jnp.concatenate([buf[slot, e, t:WINDOW_ROWS, :] for e in range(N_EXPERTS)], axis=0)
    acc = acc + _dot(tail.astype(BF16), tail_rows)
    o_ref[...] = _rms(x2_ref[...] + acc, gfin_ref[...])


def _combine_call(starts, x2, aff, code, final_g, ye, cap):
    nt, ne, t = code.shape
    n, d = x2.shape
    assert WINDOW_ALIGN * ne == t and cap >= WINDOW_ROWS and cap % WINDOW_ALIGN == 0
    grid_spec = pltpu.PrefetchScalarGridSpec(
        num_scalar_prefetch=1,
        grid=(nt,),
        in_specs=[pl.BlockSpec((t, d), lambda b, s: (b, 0)),
                  pl.BlockSpec((t, LANES), lambda b, s: (b, 0)),
                  pl.BlockSpec((None, ne, t), lambda b, s: (b, 0, 0)),
                  pl.BlockSpec((1, d), lambda b, s: (0, 0)),
                  pl.BlockSpec(memory_space=pl.ANY)],
        out_specs=pl.BlockSpec((t, d), lambda b, s: (b, 0)),
        scratch_shapes=[pltpu.VMEM((2, ne, WINDOW_ROWS, d), BF16), pltpu.SemaphoreType.DMA((2, ne))],
    )
    return pl.pallas_call(
        functools.partial(_combine_kernel, cap=cap),
        grid_spec=grid_spec,
        out_shape=jax.ShapeDtypeStruct((n, d), F32),
        compiler_params=pltpu.CompilerParams(dimension_semantics=("arbitrary",), vmem_limit_bytes=VMEM_LIMIT_BYTES),
        name="combine",
    )(starts, x2, aff, code, final_g, ye)


def _rope_angles(seq, dim):
    inv = 1.0 / (ROPE_THETA ** (jnp.arange(0, dim, 2, dtype=F32) / dim))
    ang = jnp.arange(seq, dtype=F32)[:, None] * inv[None, :]
    return jnp.cos(ang), jnp.sin(ang)


def _mla_tables(seq, scale):
    c, s = _rope_angles(seq, MLA_ROPE_DIM)
    ones = jnp.ones((seq, MLA_NOPE_DIM), F32)
    zeros_n = jnp.zeros((seq, MLA_NOPE_DIM), F32)
    zeros_p = jnp.zeros((seq, HEAD_PAD - MLA_QK_DIM), F32)
    cos = jnp.concatenate([ones, c, c, zeros_p], axis=1) * scale
    sin = jnp.concatenate([zeros_n, -s, s, zeros_p], axis=1) * scale
    return cos, sin


def _ret_tables(seq, scale):
    c, s = _rope_angles(seq, RET_QK_DIM)
    cos = jnp.concatenate([c, c, c, c], axis=1) * scale
    sin = jnp.concatenate([-s, s, -s, s], axis=1) * scale
    return cos, sin


def _prepare_weights(norm_mix_g, w_in, q_norm_g, w_uq, kv_norm_g, w_ukv, ret_gn_g, w_branch_a, w_branch_b, w_out,
                     norm_ffn_g, w_router, norm_final_g):
    offs, acc = [], 0
    for width in IN_SPLITS:
        offs.append((acc, acc + width))
        acc += width
    w_cq, w_ckv, w_kr, w_rq, w_rk, w_rv, w_rg, w_ga, w_gb = [w_in[:, a:b].astype(BF16) for a, b in offs]
    w_kr = jnp.pad(w_kr, ((0, 0), (0, LANES - MLA_ROPE_DIM)))

    uq = w_uq.reshape(Q_LORA_RANK, MLA_HEADS, MLA_QK_DIM)
    wq_pad = jnp.pad(uq, ((0, 0), (0, 0), (0, HEAD_PAD - MLA_QK_DIM))).reshape(Q_LORA_RANK, MLA_PAD_WIDTH)
    ukv = w_ukv.reshape(KV_LORA_RANK, MLA_HEADS, MLA_NOPE_DIM + MLA_V_DIM)
    wk_pad = jnp.pad(ukv[:, :, :MLA_NOPE_DIM], ((0, 0), (0, 0), (0, HEAD_PAD - MLA_NOPE_DIM)))
    wk_pad = wk_pad.reshape(KV_LORA_RANK, MLA_PAD_WIDTH)
    wv = jnp.pad(ukv[:, :, MLA_NOPE_DIM:], ((0, 0), (0, 0), (0, HEAD_PAD - MLA_V_DIM)))
    wv = wv.reshape(KV_LORA_RANK, MLA_PAD_WIDTH).T
    src = jnp.arange(LANES)[:, None]
    dst = jnp.arange(MLA_PAD_WIDTH)[None, :]
    place = ((dst % HEAD_PAD) - MLA_NOPE_DIM == src) & (src < MLA_ROPE_DIM)

    pre_w = (norm_mix_g.reshape(1, -1), w_cq, w_ckv, w_kr, w_rq, w_rk, w_rv, w_rg, w_ga, w_gb,
             q_norm_g.reshape(1, -1), wq_pad.astype(BF16), kv_norm_g.reshape(1, -1), wk_pad.astype(BF16),
             place.astype(BF16), wv.astype(BF16))

    wr_hi = w_router.astype(BF16)
    wr_lo = (w_router - wr_hi.astype(F32)).astype(BF16)
    pad_to = lambda w: jnp.pad(w, ((0, 0), (0, LANES - w.shape[1])))
    post_w = (w_branch_a.astype(BF16), w_branch_b.astype(BF16), w_out.astype(BF16), norm_ffn_g.reshape(1, -1),
              pad_to(jnp.concatenate([wr_hi, wr_lo], axis=1)), pad_to(wr_hi))
    return pre_w, post_w, ret_gn_g.reshape(1, -1), norm_final_g.reshape(1, -1)


def _encode(x, pre_w, post_w, log_gamma, gn_g, final_g, wg, wu, wd):
    batch, seq, d = x.shape
    n = batch * seq
    x2d = x.reshape(n, d)
    tables = (*_mla_tables(seq, MLA_QK_DIM ** -0.5 * LOG2_E), *_mla_tables(seq, 1.0),
              *_ret_tables(seq, 1.0), *_ret_tables(seq, RET_QK_DIM ** -0.5))
    qcat, kcat, vt, rq, rk, rv, rg, ga, gb = _pre_call(x2d, seq, pre_w, tables)
    o = _attn_call(qcat, kcat, vt, batch, seq)
    r = _ret_call(log_gamma, rq, rk, rv, rg, gn_g, batch, seq)
    x2, xn, aff, afft = _post_call(o, r, ga, gb, x2d, post_w)

    cap = CAPACITY_FACTOR * n // N_EXPERTS
    code, starts = _route_call(afft, cap)
    xe = _dispatch_call(starts, code, xn, cap)
    ye = _ffn_call(xe, wg, wu, wd, cap)
    return _combine_call(starts, x2, aff, code, final_g, ye, cap).reshape(batch, seq, d)


def kernel(x_prompt, x_sample, norm_mix_g, w_in, q_norm_g, w_uq, kv_norm_g, w_ukv, ret_decay_fwd, ret_decay_bwd,
           ret_gn_g, w_branch_a, w_branch_b, w_out, norm_ffn_g, w_router, w_exp_gate, w_exp_up, w_exp_down,
           norm_final_g):
    assert norm_mix_g.shape[0] == 1, "single-layer trunk"
    pre_w, post_w, gn_g, final_g = _prepare_weights(
        norm_mix_g[0], w_in[0], q_norm_g[0], w_uq[0], kv_norm_g[0], w_ukv[0], ret_gn_g[0], w_branch_a[0],
        w_branch_b[0], w_out[0], norm_ffn_g[0], w_router[0], norm_final_g)
    log_gamma = jnp.stack([jax.nn.log_sigmoid(ret_decay_fwd[0].astype(F32)),
                           jax.nn.log_sigmoid(ret_decay_bwd[0].astype(F32))])
    wg = w_exp_gate[0].astype(BF16)
    wu = w_exp_up[0].astype(BF16)
    wd = w_exp_down[0].astype(BF16)
    enc = functools.partial(_encode, pre_w=pre_w, post_w=post_w, log_gamma=log_gamma, gn_g=gn_g, final_g=final_g,
                            wg=wg, wu=wu, wd=wd)
    return enc(x_prompt), enc(x_sample)
```

```python
import functools

import jax
import jax.numpy as jnp
from jax import lax
from jax.experimental import pallas as pl
from jax.experimental.pallas import tpu as pltpu

D_MODEL = 1024
MLA_HEADS = 8
MLA_NOPE_DIM = 64
MLA_ROPE_DIM = 32
MLA_V_DIM = 64
MLA_QK_DIM = MLA_NOPE_DIM + MLA_ROPE_DIM
Q_LORA_RANK = 384
KV_LORA_RANK = 256
RET_HEADS = 8
RET_QK_DIM = 64
RET_V_DIM = 128
RET_QK_WIDTH = RET_HEADS * RET_QK_DIM
RET_V_WIDTH = RET_HEADS * RET_V_DIM
N_EXPERTS = 16
EXPERT_FF = 2816
CAPACITY_FACTOR = 2
ROPE_THETA = 10000.0
EPS = 1e-6
GN_EPS = 1e-5
IN_SPLITS = (Q_LORA_RANK, KV_LORA_RANK, MLA_ROPE_DIM, RET_QK_WIDTH, RET_QK_WIDTH, RET_V_WIDTH, RET_V_WIDTH,
             D_MODEL, D_MODEL)

LANES = 128
HEAD_PAD = LANES
MLA_PAD_WIDTH = MLA_HEADS * HEAD_PAD
VMEM_LIMIT_BYTES = 56 * 1024 * 1024

TOKEN_TILE = 256
VT_KEYS = TOKEN_TILE
LOG2_E = 1.4426950408889634
ROUTE_TILE = TOKEN_TILE
WINDOW_ALIGN = 16
WINDOW_ROWS = ROUTE_TILE + WINDOW_ALIGN
SHORT_SEG = 48
SHORT_ROWS = SHORT_SEG + WINDOW_ALIGN
ATTN_Q_TILE = 512
ATTN_K_TILE = 512
RET_CHUNK = 256
FFN_ROW_TILE = 256

F32 = jnp.float32
BF16 = jnp.bfloat16


def _dot(a, b):
    return jnp.dot(a, b, preferred_element_type=F32)


def _dot_nt(a, b):
    return lax.dot_general(a, b, (((1,), (1,)), ((), ())), preferred_element_type=F32)


def _dot_tn(a, b):
    return lax.dot_general(a, b, (((0,), (0,)), ((), ())), preferred_element_type=F32)


def _rms(x, g):
    return x * lax.rsqrt(jnp.mean(x * x, axis=-1, keepdims=True) + EPS) * g


def _sigmoid(x):
    return 1.0 / (1.0 + jnp.exp(-x))


def _rope_lanes(blk, cos, sin, half):
    lane = lax.broadcasted_iota(jnp.int32, blk.shape, 1)
    upper = (lane % (2 * half)) >= half
    partner = jnp.where(upper, pltpu.roll(blk, half, 1), pltpu.roll(blk, LANES - half, 1))
    return blk * cos + partner * sin


def _const_spec(shape):
    nd = len(shape)
    return pl.BlockSpec(shape, lambda *_: (0,) * nd, pipeline_mode=pl.Buffered(1))


def _pre_kernel(x_ref, gmix_ref, wcq_ref, wckv_ref, wkr_ref, wrq_ref, wrk_ref, wrv_ref, wrg_ref, wga_ref, wgb_ref,
                gq_ref, wq_ref, gkv_ref, wk_ref, place_ref, wv_ref,
                cosq_ref, sinq_ref, cosk_ref, sink_ref, cosrq_ref, sinrq_ref, cosrk_ref, sinrk_ref,
                q_out, k_out, vt_out, rq_out, rk_out, rv_out, rg_out, ga_out, gb_out):
    h = _rms(x_ref[...], gmix_ref[...]).astype(BF16)

    qn = _rms(_dot(h, wcq_ref[...]), gq_ref[...]).astype(BF16)
    q = _dot(qn, wq_ref[...])
    cosq, sinq = cosq_ref[...], sinq_ref[...]
    for j in range(MLA_HEADS):
        sl = slice(HEAD_PAD * j, HEAD_PAD * (j + 1))
        q_out[:, sl] = _rope_lanes(q[:, sl], cosq, sinq, MLA_ROPE_DIM // 2).astype(BF16)

    kvn = _rms(_dot(h, wckv_ref[...]), gkv_ref[...]).astype(BF16)
    kr = _dot(h, wkr_ref[...]).astype(BF16)
    k = _dot(kvn, wk_ref[...]) + _dot(kr, place_ref[...])
    cosk, sink = cosk_ref[...], sink_ref[...]
    for j in range(MLA_HEADS):
        sl = slice(HEAD_PAD * j, HEAD_PAD * (j + 1))
        k_out[:, sl] = _rope_lanes(k[:, sl], cosk, sink, MLA_ROPE_DIM // 2).astype(BF16)
    vt = _dot_nt(wv_ref[...], kvn)
    vrow = lax.broadcasted_iota(jnp.int32, vt.shape, 0)
    vt_out[...] = jnp.where(vrow % HEAD_PAD == MLA_V_DIM, 1.0, vt).astype(BF16)

    rq = _dot(h, wrq_ref[...])
    rk = _dot(h, wrk_ref[...])
    cosrq, sinrq, cosrk, sinrk = cosrq_ref[...], sinrq_ref[...], cosrk_ref[...], sinrk_ref[...]
    for j in range(RET_QK_WIDTH // LANES):
        sl = slice(LANES * j, LANES * (j + 1))
        rq_out[:, sl] = _rope_lanes(rq[:, sl], cosrq, sinrq, RET_QK_DIM // 2).astype(BF16)
        rk_out[:, sl] = _rope_lanes(rk[:, sl], cosrk, sinrk, RET_QK_DIM // 2).astype(BF16)

    rv_out[...] = _dot(h, wrv_ref[...]).astype(BF16)
    rg_out[...] = _dot(h, wrg_ref[...]).astype(BF16)
    ga_out[...] = _dot(h, wga_ref[...]).astype(BF16)
    gb_out[...] = _dot(h, wgb_ref[...]).astype(BF16)


def _pre_call(x2d, seq, weights, tables):
    n = x2d.shape[0]
    tm = TOKEN_TILE
    tiles_per_seq = seq // tm
    row = lambda width: pl.BlockSpec((tm, width), lambda i: (i, 0))
    tab = pl.BlockSpec((tm, LANES), lambda i: (i % tiles_per_seq, 0))
    out_widths = (MLA_PAD_WIDTH, MLA_PAD_WIDTH, None, RET_QK_WIDTH, RET_QK_WIDTH,
                  RET_V_WIDTH, RET_V_WIDTH, D_MODEL, D_MODEL)
    vt_spec = pl.BlockSpec((None, None, MLA_PAD_WIDTH, tm), lambda i: (i // tiles_per_seq, i % tiles_per_seq, 0, 0))
    vt_shape = jax.ShapeDtypeStruct((n // seq, tiles_per_seq, MLA_PAD_WIDTH, tm), BF16)
    return pl.pallas_call(
        _pre_kernel,
        grid=(n // tm,),
        in_specs=[row(D_MODEL)] + [_const_spec(w.shape) for w in weights] + [tab] * len(tables),
        out_specs=[vt_spec if w is None else row(w) for w in out_widths],
        out_shape=[vt_shape if w is None else jax.ShapeDtypeStruct((n, w), BF16) for w in out_widths],
        compiler_params=pltpu.CompilerParams(dimension_semantics=("arbitrary",), vmem_limit_bytes=VMEM_LIMIT_BYTES),
        name="pre_proj",
    )(x2d, *weights, *tables)


def _attn_kernel(q_ref, k_ref, vt_ref, o_ref, st_scr, mc_scr, m_scr, acc_scr, *, tk, nk):
    sub = tk // VT_KEYS
    heads = [slice(HEAD_PAD * hh, HEAD_PAD * (hh + 1)) for hh in range(2)]

    def scores(kb, buf):
        off = pl.multiple_of(kb * tk, tk)
        for hh, sl in enumerate(heads):
            st = _dot_nt(k_ref[pl.ds(off, tk), sl], q_ref[:, sl])
            st_scr[buf, hh] = st
            mc_scr[buf, hh] = jnp.max(st, axis=0, keepdims=True)

    def accumulate(kb, buf):
        for hh, sl in enumerate(heads):
            m_old = m_scr[hh]
            m_new = jnp.maximum(m_old, mc_scr[buf, hh])
            alpha = jnp.exp2(m_old - m_new)
            pt = jnp.exp2(st_scr[buf, hh] - m_new).astype(BF16)
            pv = _dot(vt_ref[kb * sub, sl, :], pt[0:VT_KEYS, :])
            for j in range(1, sub):
                pv = pv + _dot(vt_ref[kb * sub + j, sl, :], pt[VT_KEYS * j:VT_KEYS * (j + 1), :])
            acc_scr[hh] = alpha * acc_scr[hh] + pv
            m_scr[hh] = m_new

    m_scr[...] = jnp.full(m_scr.shape, -jnp.inf, F32)
    acc_scr[...] = jnp.zeros(acc_scr.shape, F32)
    scores(0, 0)
    if nk > 1:
        assert nk % 2 == 0

        def body(j, carry):
            kb = 2 * j
            scores(kb + 1, 1)
            accumulate(kb, 0)
            scores(kb + 2, 0)
            accumulate(kb + 1, 1)
            return carry

        lax.fori_loop(0, nk // 2 - 1, body, 0)
        scores(nk - 1, 1)
        accumulate(nk - 2, 0)
        accumulate(nk - 1, 1)
    else:
        accumulate(0, 0)
    outs = []
    for hh in range(2):
        acc = acc_scr[hh]
        outs.append(acc[0:MLA_V_DIM, :] / acc[MLA_V_DIM:MLA_V_DIM + 1, :])
    o_ref[...] = jnp.concatenate(outs, axis=0).T.astype(BF16)


def _attn_call(qcat, kcat, vt, batch, seq):
    n = qcat.shape[0]
    tq = min(ATTN_Q_TILE, seq)
    tk = min(ATTN_K_TILE, seq)
    nq = seq // tq
    pairs = MLA_HEADS // 2
    return pl.pallas_call(
        functools.partial(_attn_kernel, tk=tk, nk=seq // tk),
        grid=(batch, pairs, nq),
        in_specs=[
            pl.BlockSpec((tq, 2 * HEAD_PAD), lambda b, p, i: (b * nq + i, p)),
            pl.BlockSpec((seq, 2 * HEAD_PAD), lambda b, p, i: (b, p)),
            pl.BlockSpec((None, seq // VT_KEYS, 2 * HEAD_PAD, VT_KEYS), lambda b, p, i: (b, 0, p, 0)),
        ],
        out_specs=pl.BlockSpec((tq, 2 * MLA_V_DIM), lambda b, p, i: (b * nq + i, p)),
        out_shape=jax.ShapeDtypeStruct((n, MLA_HEADS * MLA_V_DIM), BF16),
        scratch_shapes=[pltpu.VMEM((2, 2, tk, tq), F32), pltpu.VMEM((2, 2, 1, tq), F32),
                        pltpu.VMEM((2, 1, tq), F32), pltpu.VMEM((2, HEAD_PAD, tq), F32)],
        compiler_params=pltpu.CompilerParams(dimension_semantics=("arbitrary",) * 3,
                                             vmem_limit_bytes=VMEM_LIMIT_BYTES),
        name="mla_attn",
    )(qcat, kcat, vt)


def _ret_kernel(lg_ref, q_ref, k_ref, v_ref, rg_ref, gn_ref, o_ref, rb_scr, *, chunk, nchunks):
    c_len = chunk
    pair = pl.program_id(1)
    lane = lax.broadcasted_iota(jnp.int32, (c_len, LANES), 1)
    pos = lax.broadcasted_iota(jnp.int32, (c_len, LANES), 0).astype(F32)
    ii = lax.broadcasted_iota(jnp.int32, (c_len, c_len), 0)
    jj = lax.broadcasted_iota(jnp.int32, (c_len, c_len), 1)
    diff = (ii - jj).astype(F32)
    sq = (LANES, LANES)
    two = range(2)
    lgf = [lg_ref[0, 2 * pair + hh] for hh in two]
    lgb = [lg_ref[1, 2 * pair + hh] for hh in two]
    mask = [(lane // RET_QK_DIM) == hh for hh in two]
    vsl = [slice(RET_V_DIM * hh, RET_V_DIM * (hh + 1)) for hh in two]
    zeta_f = [jnp.exp((c_len - 1.0 - pos) * lgf[hh]) for hh in two]
    xi_f = [jnp.exp((pos + 1.0) * lgf[hh]) for hh in two]
    zeta_b = [jnp.exp(pos * lgb[hh]) for hh in two]
    xi_b = [jnp.exp((c_len - pos) * lgb[hh]) for hh in two]
    gchunk_f = [jnp.exp(jnp.full(sq, c_len, F32) * lgf[hh]) for hh in two]
    gchunk_b = [jnp.exp(jnp.full(sq, c_len, F32) * lgb[hh]) for hh in two]
    decay = [jnp.where(diff >= 0.0, jnp.exp(jnp.maximum(diff, 0.0) * lgf[hh]),
                       jnp.exp(jnp.maximum(-diff, 0.0) * lgb[hh])) for hh in two]
    gn = gn_ref[...]

    def chunk_slice(c):
        return pl.ds(pl.multiple_of(c * c_len, c_len), c_len)

    def bwd_body(t, states):
        c = nchunks - 1 - t
        rows = chunk_slice(c)
        k = k_ref[rows, :].astype(F32)
        new = []
        for hh in two:
            rb_scr[hh, c] = states[hh].astype(BF16)
            kz = (jnp.where(mask[hh], k, 0.0) * zeta_b[hh]).astype(BF16)
            new.append(gchunk_b[hh] * states[hh] + _dot_tn(kz, v_ref[rows, vsl[hh]]))
        return tuple(new)

    zero_states = (jnp.zeros(sq, F32), jnp.zeros(sq, F32))
    lax.fori_loop(0, nchunks, bwd_body, zero_states)

    def fwd_body(c, states):
        rows = chunk_slice(c)
        q = q_ref[rows, :].astype(F32)
        k16 = k_ref[rows, :]
        k = k16.astype(F32)
        qm = [jnp.where(mask[hh], q, 0.0) for hh in two]
        inner = [_dot_nt(qm[hh].astype(BF16), k16) for hh in two]
        ys = []
        for hh in two:
            lhs = jnp.concatenate([(inner[hh] * decay[hh]).astype(BF16), (qm[hh] * xi_f[hh]).astype(BF16),
                                   (qm[hh] * xi_b[hh]).astype(BF16)], axis=1)
            rhs = jnp.concatenate([v_ref[rows, vsl[hh]], states[hh].astype(BF16), rb_scr[hh, c]], axis=0)
            ys.append(_dot(lhs, rhs))
        new = []
        for hh in two:
            y = ys[hh]
            mu = jnp.mean(y, axis=-1, keepdims=True)
            yc = y - mu
            var = jnp.mean(yc * yc, axis=-1, keepdims=True)
            yn = yc * lax.rsqrt(var + GN_EPS) * gn[:, vsl[hh]]
            rg = rg_ref[rows, vsl[hh]].astype(F32)
            o_ref[rows, vsl[hh]] = (rg * _sigmoid(rg) * yn).astype(BF16)
            kz = (jnp.where(mask[hh], k, 0.0) * zeta_f[hh]).astype(BF16)
            new.append(gchunk_f[hh] * states[hh] + _dot_tn(kz, v_ref[rows, vsl[hh]]))
        return tuple(new)

    lax.fori_loop(0, nchunks, fwd_body, zero_states)


def _ret_call(log_gamma, rq, rk, rv, rg, gn_g, batch, seq):
    n = rq.shape[0]
    chunk = min(RET_CHUNK, seq)
    nchunks = seq // chunk
    qk_blk = pl.BlockSpec((seq, LANES), lambda b, p: (b, p))
    v_blk = pl.BlockSpec((seq, 2 * RET_V_DIM), lambda b, p: (b, p))
    return pl.pallas_call(
        functools.partial(_ret_kernel, chunk=chunk, nchunks=nchunks),
        grid=(batch, RET_HEADS // 2),
        in_specs=[pl.BlockSpec(memory_space=pltpu.SMEM), qk_blk, qk_blk, v_blk, v_blk,
                  pl.BlockSpec((1, 2 * RET_V_DIM), lambda b, p: (0, p))],
        out_specs=v_blk,
        out_shape=jax.ShapeDtypeStruct((n, RET_V_WIDTH), BF16),
        scratch_shapes=[pltpu.VMEM((2, nchunks, LANES, LANES), BF16)],
        compiler_params=pltpu.CompilerParams(dimension_semantics=("arbitrary",) * 2,
                                             vmem_limit_bytes=VMEM_LIMIT_BYTES),
        name="retention",
    )(log_gamma, rq, rk, rv, rg, gn_g)


def _post_kernel(o_ref, r_ref, ga_ref, gb_ref, x_ref, wa_ref, wb_ref, wo_ref, gffn_ref, wr_hi_lo_ref, wr_hi_ref,
                 x2_out, xn_out, aff_out, afft_out):
    a = _dot(o_ref[...], wa_ref[...])
    r = _dot(r_ref[...], wb_ref[...])
    mixed = _sigmoid(ga_ref[...].astype(F32)) * a + _sigmoid(gb_ref[...].astype(F32)) * r
    x2 = x_ref[...] + _dot(mixed.astype(BF16), wo_ref[...])
    x2_out[...] = x2
    xn = _rms(x2, gffn_ref[...])
    xn_hi = xn.astype(BF16)
    xn_out[...] = xn_hi
    xn_lo = (xn - xn_hi.astype(F32)).astype(BF16)
    t = _dot(xn_hi, wr_hi_lo_ref[...]) + _dot(xn_lo, wr_hi_ref[...])
    logits = t + pltpu.roll(t, LANES - N_EXPERTS, 1)
    lane = lax.broadcasted_iota(jnp.int32, logits.shape, 1)
    logits = jnp.where(lane < N_EXPERTS, logits, -jnp.inf)
    e = jnp.exp(logits - jnp.max(logits, axis=-1, keepdims=True))
    aff = e / jnp.sum(e, axis=-1, keepdims=True)
    aff_out[...] = aff
    afft_out[...] = aff.T[0:N_EXPERTS, :]


def _post_call(o, r, ga, gb, x2d, weights):
    n = x2d.shape[0]
    tm = TOKEN_TILE
    row = lambda width: pl.BlockSpec((tm, width), lambda i: (i, 0))
    return pl.pallas_call(
        _post_kernel,
        grid=(n // tm,),
        in_specs=[row(MLA_HEADS * MLA_V_DIM), row(RET_V_WIDTH), row(D_MODEL), row(D_MODEL), row(D_MODEL)]
        + [_const_spec(w.shape) for w in weights],
        out_specs=[row(D_MODEL), row(D_MODEL), row(LANES), pl.BlockSpec((None, N_EXPERTS, tm), lambda i: (i, 0, 0))],
        out_shape=[jax.ShapeDtypeStruct((n, D_MODEL), F32), jax.ShapeDtypeStruct((n, D_MODEL), BF16),
                   jax.ShapeDtypeStruct((n, LANES), F32), jax.ShapeDtypeStruct((n // tm, N_EXPERTS, tm), F32)],
        compiler_params=pltpu.CompilerParams(dimension_semantics=("arbitrary",), vmem_limit_bytes=VMEM_LIMIT_BYTES),
        name="post_mix",
    )(o, r, ga, gb, x2d, *weights)


def _ffn_kernel(xe_ref, wg_ref, wu_ref, wd_ref, ye_ref):
    x = xe_ref[...]
    g = _dot(x, wg_ref[...])
    u = _dot(x, wu_ref[...])
    hid = (g * _sigmoid(g) * u).astype(BF16)
    ye_ref[...] = _dot(hid, wd_ref[...]).astype(BF16)


def _ffn_call(xe, wg, wu, wd, cap):
    n_exp, _, d = xe.shape
    tm = min(FFN_ROW_TILE, cap)
    ff = wg.shape[-1]
    return pl.pallas_call(
        _ffn_kernel,
        grid=(n_exp, cap // tm),
        in_specs=[
            pl.BlockSpec((None, tm, d), lambda e, i: (e, i, 0)),
            pl.BlockSpec((None, d, ff), lambda e, i: (e, 0, 0)),
            pl.BlockSpec((None, d, ff), lambda e, i: (e, 0, 0)),
            pl.BlockSpec((None, ff, d), lambda e, i: (e, 0, 0)),
        ],
        out_specs=pl.BlockSpec((None, tm, d), lambda e, i: (e, i, 0)),
        out_shape=jax.ShapeDtypeStruct((n_exp, cap, d), BF16),
        compiler_params=pltpu.CompilerParams(dimension_semantics=("arbitrary",) * 2,
                                             vmem_limit_bytes=VMEM_LIMIT_BYTES),
        name="expert_ffn",
    )(xe, wg, wu, wd)


def _route_kernel(afft_ref, code_ref, starts_ref, *, cap):
    nt, ne, t = afft_ref.shape
    bits = jnp.maximum(pltpu.bitcast(afft_ref[...], jnp.int32), 0)
    idx = (lax.broadcasted_iota(jnp.int32, (nt, ne, t), 0) * t + lax.broadcasted_iota(jnp.int32, (nt, ne, t), 2))
    capf = jnp.float32(cap)

    def count(flag):
        per_lane = jnp.sum(flag.astype(F32), axis=0, keepdims=True)
        return jnp.sum(per_lane, axis=2, keepdims=True)

    def thr_body(i, thr):
        cand = thr | jnp.left_shift(jnp.int32(1), 30 - i)
        return jnp.where(count(bits >= cand) >= capf, cand, thr)

    thr = lax.fori_loop(0, 31, thr_body, jnp.zeros((1, ne, 1), jnp.int32))
    above = bits > thr
    tied = bits == thr
    need = capf - count(above)
    nbits = (nt * t - 1).bit_length()

    def idx_body(i, last):
        cand = last | jnp.left_shift(jnp.int32(1), nbits - 1 - i)
        return jnp.where(count(tied & (idx < cand)) <= need - 1.0, cand, last)

    last = lax.fori_loop(0, nbits, idx_body, jnp.zeros((1, ne, 1), jnp.int32))
    sel = (above | (tied & (idx <= last))).astype(F32)

    row = lax.broadcasted_iota(jnp.int32, (t, t), 0)
    col = lax.broadcasted_iota(jnp.int32, (t, t), 1)
    earlier = (row < col).astype(BF16)
    rank = _dot(sel.reshape(nt * ne, t).astype(BF16), earlier).reshape(nt, ne, t)
    code_ref[...] = jnp.where(sel > 0.0, rank, -1.0)
    per_tile = jnp.sum(sel, axis=2, keepdims=True)
    run = jnp.zeros((ne, 1), F32)
    for b in range(nt):
        starts_ref[b] = run.astype(jnp.int32)
        run = run + per_tile[b]


def _route_call(afft, cap):
    nt, ne, t = afft.shape
    full = lambda shape: pl.BlockSpec(shape, lambda: (0,) * len(shape))
    code, starts = pl.pallas_call(
        functools.partial(_route_kernel, cap=cap),
        in_specs=[full((nt, ne, t))],
        out_specs=[full((nt, ne, t)), full((nt, ne, 1))],
        out_shape=[jax.ShapeDtypeStruct((nt, ne, t), F32), jax.ShapeDtypeStruct((nt, ne, 1), jnp.int32)],
        compiler_params=pltpu.CompilerParams(vmem_limit_bytes=VMEM_LIMIT_BYTES),
        name="route",
    )(afft)
    return code, starts.reshape(nt, ne)


def _window_start(starts_ref, tile, e):
    return pl.multiple_of(jnp.bitwise_and(starts_ref[tile, e], -WINDOW_ALIGN), WINDOW_ALIGN)


def _tile_is_short(starts_ref, tile, ntiles, cap):
    nxt = jnp.minimum(tile + 1, ntiles - 1)
    longest = jnp.int32(0)
    for e in range(N_EXPERTS):
        end = jnp.where(tile + 1 < ntiles, starts_ref[nxt, e], cap)
        longest = jnp.maximum(longest, end - starts_ref[tile, e])
    return longest <= SHORT_SEG


def _dispatch_kernel(starts_ref, xn_ref, code_ref, xe_hbm, buf, sem, *, cap):
    b = pl.program_id(0)
    nb = pl.num_programs(0)
    slot = b % 2
    t = ROUTE_TILE
    prev = jnp.maximum(b - 1, 0)
    short_now = _tile_is_short(starts_ref, b, nb, cap)
    short_prev = _tile_is_short(starts_ref, prev, nb, cap)

    def win_copy(sl, e, tile, rows):
        return pltpu.make_async_copy(buf.at[sl, e, pl.ds(0, rows), :],
                                     xe_hbm.at[e, pl.ds(_window_start(starts_ref, tile, e), rows), :],
                                     sem.at[sl, e])

    @pl.when(b == 0)
    def _():
        buf[1] = jnp.zeros(buf.shape[1:], BF16)
        fills = [pltpu.make_async_copy(buf.at[1, e], xe_hbm.at[e, pl.ds(cap, WINDOW_ROWS), :], sem.at[1, e])
                 for e in range(N_EXPERTS)]
        for f in fills:
            f.start()
        for f in fills:
            f.wait()

    def fill_windows(rows, stacked):
        code = code_ref[...]
        rowf = lax.broadcasted_iota(jnp.int32, (rows, t), 0).astype(F32)
        xn = xn_ref[...]
        takes = []
        for e in range(N_EXPERTS):
            pos = code[e:e + 1, :] + (starts_ref[b, e] - _window_start(starts_ref, b, e)).astype(F32)
            takes.append(jnp.logical_and(code[e:e + 1, :] >= 0.0, pos == rowf).astype(BF16))
        if stacked:
            wins_all = _dot(jnp.concatenate(takes, axis=0), xn)
            wins = [wins_all[rows * e:rows * (e + 1), :] for e in range(N_EXPERTS)]
        else:
            wins = [_dot(take, xn) for take in takes]
        for e in range(N_EXPERTS):
            back = pl.multiple_of(_window_start(starts_ref, b, e) - _window_start(starts_ref, prev, e), WINDOW_ALIGN)
            carry = buf[1 - slot, e, pl.ds(back, WINDOW_ALIGN), :].astype(F32)
            buf[slot, e, 0:WINDOW_ALIGN, :] = (wins[e][0:WINDOW_ALIGN, :] + carry).astype(BF16)
            buf[slot, e, WINDOW_ALIGN:rows, :] = wins[e][WINDOW_ALIGN:rows, :].astype(BF16)

    def for_windows(cond, rows, action):
        @pl.when(cond)
        def _():
            for e in range(N_EXPERTS):
                action(e, rows)

    long_now = jnp.logical_not(short_now)
    pl.when(short_now)(lambda: fill_windows(SHORT_ROWS, True))
    pl.when(long_now)(lambda: fill_windows(WINDOW_ROWS, False))
    for_windows(jnp.logical_and(b > 0, short_prev), SHORT_ROWS, lambda e, r: win_copy(1 - slot, e, b - 1, r).wait())
    for_windows(jnp.logical_and(b > 0, jnp.logical_not(short_prev)), WINDOW_ROWS,
                lambda e, r: win_copy(1 - slot, e, b - 1, r).wait())
    for_windows(short_now, SHORT_ROWS, lambda e, r: win_copy(slot, e, b, r).start())
    for_windows(long_now, WINDOW_ROWS, lambda e, r: win_copy(slot, e, b, r).start())
    last = b == nb - 1
    for_windows(jnp.logical_and(last, short_now), SHORT_ROWS, lambda e, r: win_copy(slot, e, b, r).wait())
    for_windows(jnp.logical_and(last, long_now), WINDOW_ROWS, lambda e, r: win_copy(slot, e, b, r).wait())


def _dispatch_call(starts, code, xn, cap):
    nt, ne, t = code.shape
    d = xn.shape[1]
    grid_spec = pltpu.PrefetchScalarGridSpec(
        num_scalar_prefetch=1,
        grid=(nt,),
        in_specs=[pl.BlockSpec((t, d), lambda b, s: (b, 0)),
                  pl.BlockSpec((None, ne, t), lambda b, s: (b, 0, 0))],
        out_specs=pl.BlockSpec(memory_space=pl.ANY),
        scratch_shapes=[pltpu.VMEM((2, ne, WINDOW_ROWS, d), BF16), pltpu.SemaphoreType.DMA((2, ne))],
    )
    return pl.pallas_call(
        functools.partial(_dispatch_kernel, cap=cap),
        grid_spec=grid_spec,
        out_shape=jax.ShapeDtypeStruct((ne, cap + WINDOW_ROWS, d), BF16),
        compiler_params=pltpu.CompilerParams(dimension_semantics=("arbitrary",), vmem_limit_bytes=VMEM_LIMIT_BYTES),
        name="dispatch",
    )(starts, xn, code)


def _combine_kernel(starts_ref, x2_ref, aff_ref, code_ref, gfin_ref, ye_hbm, o_ref, buf, sem, *, cap):
    b = pl.program_id(0)
    nb = pl.num_programs(0)
    slot = b % 2
    t = ROUTE_TILE
    nxt = jnp.minimum(b + 1, nb - 1)
    short_now = _tile_is_short(starts_ref, b, nb, cap)
    short_next = _tile_is_short(starts_ref, nxt, nb, cap)
    long_now = jnp.logical_not(short_now)

    def win_start(tile, e, rows):
        aligned = jnp.bitwise_and(starts_ref[tile, e], -WINDOW_ALIGN)
        return pl.multiple_of(jnp.minimum(aligned, cap - rows), WINDOW_ALIGN)

    def win_copy(sl, e, tile, rows):
        return pltpu.make_async_copy(ye_hbm.at[e, pl.ds(win_start(tile, e, rows), rows), :],
                                     buf.at[sl, e, pl.ds(0, rows), :], sem.at[sl, e])

    def for_windows(cond, rows, action):
        @pl.when(cond)
        def _():
            for e in range(N_EXPERTS):
                action(e, rows)

    first = b == 0
    for_windows(jnp.logical_and(first, short_now), SHORT_ROWS, lambda e, r: win_copy(0, e, 0, r).start())
    for_windows(jnp.logical_and(first, long_now), WINDOW_ROWS, lambda e, r: win_copy(0, e, 0, r).start())
    more = b + 1 < nb
    for_windows(jnp.logical_and(more, short_next), SHORT_ROWS, lambda e, r: win_copy(1 - slot, e, b + 1, r).start())
    for_windows(jnp.logical_and(more, jnp.logical_not(short_next)), WINDOW_ROWS,
                lambda e, r: win_copy(1 - slot, e, b + 1, r).start())

    def ranks_and_gates():
        row = lax.broadcasted_iota(jnp.int32, (t, t), 0)
        col = lax.broadcasted_iota(jnp.int32, (t, t), 1)
        ranks = _dot_nt((row == col).astype(BF16), code_ref[...].astype(BF16))
        return ranks, jnp.where(ranks >= 0.0, aff_ref[:, 0:N_EXPERTS], 0.0)

    def long_path():
        ranks, gates = ranks_and_gates()
        colf = lax.broadcasted_iota(jnp.int32, (t, t), 1).astype(F32)
        for e in range(N_EXPERTS):
            win_copy(slot, e, b, WINDOW_ROWS).wait()
        acc = jnp.zeros((t, D_MODEL), F32)
        tail = jnp.zeros((t, t), F32)
        for e in range(N_EXPERTS):
            pos = ranks[:, e:e + 1] + (starts_ref[b, e] - win_start(b, e, WINDOW_ROWS)).astype(F32)
            g = gates[:, e:e + 1]
            acc = acc + _dot(jnp.where(pos == colf, g, 0.0).astype(BF16), buf[slot, e, 0:t, :])
            in_tail = jnp.logical_and(pos >= float(t), pos - float(t - WINDOW_ALIGN * e) == colf)
            tail = tail + jnp.where(in_tail, g, 0.0)
        tail_rows = jnp.concatenate([buf[slot, e, t:WINDOW_ROWS, :] for e in range(N_EXPERTS)], axis=0)
        acc = acc + _dot(tail.astype(BF16), tail_rows)
        o_ref[...] = _rms(x2_ref[...] + acc, gfin_ref[...])

    def short_path():
        ranks, gates = ranks_and_gates()
        width = N_EXPERTS * SHORT_ROWS
        lane = lax.broadcasted_iota(jnp.int32, (1, N_EXPERTS), 1)
        shift = jnp.zeros((1, N_EXPERTS), F32)
        for e in range(N_EXPERTS):
            shift = jnp.where(lane == e, (starts_ref[b, e] - win_start(b, e, SHORT_ROWS)).astype(F32), shift)
        assert SHORT_ROWS & (SHORT_ROWS - 1) == 0
        group = jnp.right_shift(lax.broadcasted_iota(jnp.int32, (N_EXPERTS, width), 1), SHORT_ROWS.bit_length() - 1)
        spread = (group == lax.broadcasted_iota(jnp.int32, (N_EXPERTS, width), 0)).astype(BF16)
        pos = _dot((ranks + shift).astype(BF16), spread)
        g = _dot(gates.astype(BF16), spread)
        colf = jnp.bitwise_and(lax.broadcasted_iota(jnp.int32, (t, width), 1), SHORT_ROWS - 1).astype(F32)
        for e in range(N_EXPERTS):
            win_copy(slot, e, b, SHORT_ROWS).wait()
        rows = jnp.concatenate([buf[slot, e, 0:SHORT_ROWS, :] for e in range(N_EXPERTS)], axis=0)
        acc = _dot(jnp.where(pos == colf, g, 0.0).astype(BF16), rows)
        o_ref[...] = _rms(x2_ref[...] + acc, gfin_ref[...])

    pl.when(short_now)(short_path)
    pl.when(long_now)(long_path)


def _combine_call(starts, x2, aff, code, final_g, ye, cap):
    nt, ne, t = code.shape
    n, d = x2.shape
    assert WINDOW_ALIGN * ne == t and cap >= WINDOW_ROWS and cap % WINDOW_ALIGN == 0
    grid_spec = pltpu.PrefetchScalarGridSpec(
        num_scalar_prefetch=1,
        grid=(nt,),
        in_specs=[pl.BlockSpec((t, d), lambda b, s: (b, 0)),
                  pl.BlockSpec((t, LANES), lambda b, s: (b, 0)),
                  pl.BlockSpec((None, ne, t), lambda b, s: (b, 0, 0)),
                  pl.BlockSpec((1, d), lambda b, s: (0, 0)),
                  pl.BlockSpec(memory_space=pl.ANY)],
        out_specs=pl.BlockSpec((t, d), lambda b, s: (b, 0)),
        scratch_shapes=[pltpu.VMEM((2, ne, WINDOW_ROWS, d), BF16), pltpu.SemaphoreType.DMA((2, ne))],
    )
    return pl.pallas_call(
        functools.partial(_combine_kernel, cap=cap),
        grid_spec=grid_spec,
        out_shape=jax.ShapeDtypeStruct((n, d), F32),
        compiler_params=pltpu.CompilerParams(dimension_semantics=("arbitrary",), vmem_limit_bytes=VMEM_LIMIT_BYTES),
        name="combine",
    )(starts, x2, aff, code, final_g, ye)


def _rope_angles(seq, dim):
    inv = 1.0 / (ROPE_THETA ** (jnp.arange(0, dim, 2, dtype=F32) / dim))
    ang = jnp.arange(seq, dtype=F32)[:, None] * inv[None, :]
    return jnp.cos(ang), jnp.sin(ang)


def _mla_tables(seq, scale):
    c, s = _rope_angles(seq, MLA_ROPE_DIM)
    ones = jnp.ones((seq, MLA_NOPE_DIM), F32)
    zeros_n = jnp.zeros((seq, MLA_NOPE_DIM), F32)
    zeros_p = jnp.zeros((seq, HEAD_PAD - MLA_QK_DIM), F32)
    cos = jnp.concatenate([ones, c, c, zeros_p], axis=1) * scale
    sin = jnp.concatenate([zeros_n, -s, s, zeros_p], axis=1) * scale
    return cos, sin


def _ret_tables(seq, scale):
    c, s = _rope_angles(seq, RET_QK_DIM)
    cos = jnp.concatenate([c, c, c, c], axis=1) * scale
    sin = jnp.concatenate([-s, s, -s, s], axis=1) * scale
    return cos, sin


def _prepare_weights(norm_mix_g, w_in, q_norm_g, w_uq, kv_norm_g, w_ukv, ret_gn_g, w_branch_a, w_branch_b, w_out,
                     norm_ffn_g, w_router, norm_final_g):
    offs, acc = [], 0
    for width in IN_SPLITS:
        offs.append((acc, acc + width))
        acc += width
    w_cq, w_ckv, w_kr, w_rq, w_rk, w_rv, w_rg, w_ga, w_gb = [w_in[:, a:b].astype(BF16) for a, b in offs]
    w_kr = jnp.pad(w_kr, ((0, 0), (0, LANES - MLA_ROPE_DIM)))

    uq = w_uq.reshape(Q_LORA_RANK, MLA_HEADS, MLA_QK_DIM)
    wq_pad = jnp.pad(uq, ((0, 0), (0, 0), (0, HEAD_PAD - MLA_QK_DIM))).reshape(Q_LORA_RANK, MLA_PAD_WIDTH)
    ukv = w_ukv.reshape(KV_LORA_RANK, MLA_HEADS, MLA_NOPE_DIM + MLA_V_DIM)
    wk_pad = jnp.pad(ukv[:, :, :MLA_NOPE_DIM], ((0, 0), (0, 0), (0, HEAD_PAD - MLA_NOPE_DIM)))
    wk_pad = wk_pad.reshape(KV_LORA_RANK, MLA_PAD_WIDTH)
    wv = jnp.pad(ukv[:, :, MLA_NOPE_DIM:], ((0, 0), (0, 0), (0, HEAD_PAD - MLA_V_DIM)))
    wv = wv.reshape(KV_LORA_RANK, MLA_PAD_WIDTH).T
    src = jnp.arange(LANES)[:, None]
    dst = jnp.arange(MLA_PAD_WIDTH)[None, :]
    place = ((dst % HEAD_PAD) - MLA_NOPE_DIM == src) & (src < MLA_ROPE_DIM)

    pre_w = (norm_mix_g.reshape(1, -1), w_cq, w_ckv, w_kr, w_rq, w_rk, w_rv, w_rg, w_ga, w_gb,
             q_norm_g.reshape(1, -1), wq_pad.astype(BF16), kv_norm_g.reshape(1, -1), wk_pad.astype(BF16),
             place.astype(BF16), wv.astype(BF16))

    wr_hi = w_router.astype(BF16)
    wr_lo = (w_router - wr_hi.astype(F32)).astype(BF16)
    pad_to = lambda w: jnp.pad(w, ((0, 0), (0, LANES - w.shape[1])))
    post_w = (w_branch_a.astype(BF16), w_branch_b.astype(BF16), w_out.astype(BF16), norm_ffn_g.reshape(1, -1),
              pad_to(jnp.concatenate([wr_hi, wr_lo], axis=1)), pad_to(wr_hi))
    return pre_w, post_w, ret_gn_g.reshape(1, -1), norm_final_g.reshape(1, -1)


def _encode(x, pre_w, post_w, log_gamma, gn_g, final_g, wg, wu, wd):
    batch, seq, d = x.shape
    n = batch * seq
    x2d = x.reshape(n, d)
    tables = (*_mla_tables(seq, MLA_QK_DIM ** -0.5 * LOG2_E), *_mla_tables(seq, 1.0),
              *_ret_tables(seq, 1.0), *_ret_tables(seq, RET_QK_DIM ** -0.5))
    qcat, kcat, vt, rq, rk, rv, rg, ga, gb = _pre_call(x2d, seq, pre_w, tables)
    o = _attn_call(qcat, kcat, vt, batch, seq)
    r = _ret_call(log_gamma, rq, rk, rv, rg, gn_g, batch, seq)
    x2, xn, aff, afft = _post_call(o, r, ga, gb, x2d, post_w)

    cap = CAPACITY_FACTOR * n // N_EXPERTS
    code, starts = _route_call(afft, cap)
    xe = _dispatch_call(starts, code, xn, cap)
    ye = _ffn_call(xe, wg, wu, wd, cap)
    return _combine_call(starts, x2, aff, code, final_g, ye, cap).reshape(batch, seq, d)


def kernel(x_prompt, x_sample, norm_mix_g, w_in, q_norm_g, w_uq, kv_norm_g, w_ukv, ret_decay_fwd, ret_decay_bwd,
           ret_gn_g, w_branch_a, w_branch_b, w_out, norm_ffn_g, w_router, w_exp_gate, w_exp_up, w_exp_down,
           norm_final_g):
    assert norm_mix_g.shape[0] == 1, "single-layer trunk"
    pre_w, post_w, gn_g, final_g = _prepare_weights(
        norm_mix_g[0], w_in[0], q_norm_g[0], w_uq[0], kv_norm_g[0], w_ukv[0], ret_gn_g[0], w_branch_a[0],
        w_branch_b[0], w_out[0], norm_ffn_g[0], w_router[0], norm_final_g)
    log_gamma = jnp.stack([jax.nn.log_sigmoid(ret_decay_fwd[0].astype(F32)),
                           jax.nn.log_sigmoid(ret_decay_bwd[0].astype(F32))])
    wg = w_exp_gate[0].astype(BF16)
    wu = w_exp_up[0].astype(BF16)
    wd = w_exp_down[0].astype(BF16)
    enc = functools.partial(_encode, pre_w=pre_w, post_w=post_w, log_gamma=log_gamma, gn_g=gn_g, final_g=final_g,
                            wg=wg, wu=wu, wd=wd)
    return enc(x_prompt), enc(x_sample)
```

```python
import functools

import jax
import jax.numpy as jnp
from jax import lax
from jax.experimental import pallas as pl
from jax.experimental.pallas import tpu as pltpu

D_MODEL = 1024
MLA_HEADS = 8
MLA_NOPE_DIM = 64
MLA_ROPE_DIM = 32
MLA_V_DIM = 64
MLA_QK_DIM = MLA_NOPE_DIM + MLA_ROPE_DIM
Q_LORA_RANK = 384
KV_LORA_RANK = 256
RET_HEADS = 8
RET_QK_DIM = 64
RET_V_DIM = 128
RET_QK_WIDTH = RET_HEADS * RET_QK_DIM
RET_V_WIDTH = RET_HEADS * RET_V_DIM
N_EXPERTS = 16
EXPERT_FF = 2816
CAPACITY_FACTOR = 2
ROPE_THETA = 10000.0
EPS = 1e-6
GN_EPS = 1e-5
IN_SPLITS = (Q_LORA_RANK, KV_LORA_RANK, MLA_ROPE_DIM, RET_QK_WIDTH, RET_QK_WIDTH, RET_V_WIDTH, RET_V_WIDTH,
             D_MODEL, D_MODEL)

LANES = 128
HEAD_PAD = LANES
MLA_PAD_WIDTH = MLA_HEADS * HEAD_PAD
VMEM_LIMIT_BYTES = 56 * 1024 * 1024

TOKEN_TILE = 256
PRE_TILE = 512
VT_KEYS = PRE_TILE
LOG2_E = 1.4426950408889634
ROUTE_TILE = TOKEN_TILE
WINDOW_ALIGN = 16
WINDOW_ROWS = ROUTE_TILE + WINDOW_ALIGN
SHORT_SEG = 48
SHORT_ROWS = SHORT_SEG + WINDOW_ALIGN
ATTN_Q_TILE = 512
ATTN_K_TILE = 512
ATTN_UNROLL = 3
RET_CHUNK = 256
RET_BWD_UNROLL = 8
RET_FWD_UNROLL = 4
FFN_ROW_TILE = 256

F32 = jnp.float32
BF16 = jnp.bfloat16


def _dot(a, b):
    return jnp.dot(a, b, preferred_element_type=F32)


def _dot_nt(a, b):
    return lax.dot_general(a, b, (((1,), (1,)), ((), ())), preferred_element_type=F32)


def _dot_tn(a, b):
    return lax.dot_general(a, b, (((0,), (0,)), ((), ())), preferred_element_type=F32)


def _rms(x, g):
    return x * lax.rsqrt(jnp.mean(x * x, axis=-1, keepdims=True) + EPS) * g


def _sigmoid(x):
    return 1.0 / (1.0 + jnp.exp(-x))


def _rope_lanes(blk, cos, sin, half):
    lane = lax.broadcasted_iota(jnp.int32, blk.shape, 1)
    upper = (lane % (2 * half)) >= half
    partner = jnp.where(upper, pltpu.roll(blk, half, 1), pltpu.roll(blk, LANES - half, 1))
    return blk * cos + partner * sin


def _const_spec(shape):
    nd = len(shape)
    return pl.BlockSpec(shape, lambda *_: (0,) * nd, pipeline_mode=pl.Buffered(1))


def _pre_kernel(x_ref, gmix_ref, wcq_ref, wckv_ref, wkr_ref, wrq_ref, wrk_ref, wrv_ref, wrg_ref, wga_ref, wgb_ref,
                gq_ref, wq_ref, gkv_ref, wk_ref, place_ref, wv_ref,
                cosq_ref, sinq_ref, cosk_ref, sink_ref, cosrq_ref, sinrq_ref, cosrk_ref, sinrk_ref,
                q_out, k_out, vt_out, rq_out, rk_out, rv_out, rg_out, ga_out, gb_out):
    h = _rms(x_ref[...], gmix_ref[...]).astype(BF16)

    qn = _rms(_dot(h, wcq_ref[...]), gq_ref[...]).astype(BF16)
    q = _dot(qn, wq_ref[...])
    cosq, sinq = cosq_ref[...], sinq_ref[...]
    for j in range(MLA_HEADS):
        sl = slice(HEAD_PAD * j, HEAD_PAD * (j + 1))
        q_out[:, sl] = _rope_lanes(q[:, sl], cosq, sinq, MLA_ROPE_DIM // 2).astype(BF16)

    kvn = _rms(_dot(h, wckv_ref[...]), gkv_ref[...]).astype(BF16)
    kr = _dot(h, wkr_ref[...]).astype(BF16)
    k = _dot(kvn, wk_ref[...]) + _dot(kr, place_ref[...])
    cosk, sink = cosk_ref[...], sink_ref[...]
    for j in range(MLA_HEADS):
        sl = slice(HEAD_PAD * j, HEAD_PAD * (j + 1))
        k_out[:, sl] = _rope_lanes(k[:, sl], cosk, sink, MLA_ROPE_DIM // 2).astype(BF16)
    vt = _dot_nt(wv_ref[...], kvn)
    vrow = lax.broadcasted_iota(jnp.int32, vt.shape, 0)
    vt_out[...] = jnp.where(vrow % HEAD_PAD == MLA_V_DIM, 1.0, vt).astype(BF16)

    rq = _dot(h, wrq_ref[...])
    rk = _dot(h, wrk_ref[...])
    cosrq, sinrq, cosrk, sinrk = cosrq_ref[...], sinrq_ref[...], cosrk_ref[...], sinrk_ref[...]
    for j in range(RET_QK_WIDTH // LANES):
        sl = slice(LANES * j, LANES * (j + 1))
        rq_out[:, sl] = _rope_lanes(rq[:, sl], cosrq, sinrq, RET_QK_DIM // 2).astype(BF16)
        rk_out[:, sl] = _rope_lanes(rk[:, sl], cosrk, sinrk, RET_QK_DIM // 2).astype(BF16)

    rv_out[...] = _dot(h, wrv_ref[...]).astype(BF16)
    rg_out[...] = _dot(h, wrg_ref[...]).astype(BF16)
    ga_out[...] = _dot(h, wga_ref[...]).astype(BF16)
    gb_out[...] = _dot(h, wgb_ref[...]).astype(BF16)


def _pre_call(x2d, seq, weights, tables):
    n = x2d.shape[0]
    tm = PRE_TILE
    tiles_per_seq = seq // tm
    row = lambda width: pl.BlockSpec((tm, width), lambda i: (i, 0))
    tab = pl.BlockSpec((tm, LANES), lambda i: (i % tiles_per_seq, 0))
    out_widths = (MLA_PAD_WIDTH, MLA_PAD_WIDTH, None, RET_QK_WIDTH, RET_QK_WIDTH,
                  RET_V_WIDTH, RET_V_WIDTH, D_MODEL, D_MODEL)
    vt_spec = pl.BlockSpec((None, None, MLA_PAD_WIDTH, tm), lambda i: (i // tiles_per_seq, i % tiles_per_seq, 0, 0))
    vt_shape = jax.ShapeDtypeStruct((n // seq, tiles_per_seq, MLA_PAD_WIDTH, tm), BF16)
    return pl.pallas_call(
        _pre_kernel,
        grid=(n // tm,),
        in_specs=[row(D_MODEL)] + [_const_spec(w.shape) for w in weights] + [tab] * len(tables),
        out_specs=[vt_spec if w is None else row(w) for w in out_widths],
        out_shape=[vt_shape if w is None else jax.ShapeDtypeStruct((n, w), BF16) for w in out_widths],
        compiler_params=pltpu.CompilerParams(dimension_semantics=("arbitrary",), vmem_limit_bytes=VMEM_LIMIT_BYTES),
        name="pre_proj",
    )(x2d, *weights, *tables)


def _attn_kernel(q_ref, k_ref, vt_ref, o_ref, st_scr, mc_scr, m_scr, acc_scr, *, tk, nk):
    sub = tk // VT_KEYS
    heads = [slice(HEAD_PAD * hh, HEAD_PAD * (hh + 1)) for hh in range(2)]

    def scores(kb, buf):
        off = pl.multiple_of(kb * tk, tk)
        for hh, sl in enumerate(heads):
            st = _dot_nt(k_ref[pl.ds(off, tk), sl], q_ref[:, sl])
            st_scr[buf, hh] = st
            mc_scr[buf, hh] = jnp.max(st, axis=0, keepdims=True)

    def accumulate(kb, buf):
        for hh, sl in enumerate(heads):
            m_old = m_scr[hh]
            m_new = jnp.maximum(m_old, mc_scr[buf, hh])
            alpha = jnp.exp2(m_old - m_new)
            pt = jnp.exp2(st_scr[buf, hh] - m_new).astype(BF16)
            pv = _dot(vt_ref[kb * sub, sl, :], pt[0:VT_KEYS, :])
            for j in range(1, sub):
                pv = pv + _dot(vt_ref[kb * sub + j, sl, :], pt[VT_KEYS * j:VT_KEYS * (j + 1), :])
            acc_scr[hh] = alpha * acc_scr[hh] + pv
            m_scr[hh] = m_new

    m_scr[...] = jnp.full(m_scr.shape, -jnp.inf, F32)
    acc_scr[...] = jnp.zeros(acc_scr.shape, F32)
    scores(0, 0)
    if nk > 1:
        assert nk % 2 == 0

        def body(j, carry):
            kb = 2 * j
            scores(kb + 1, 1)
            accumulate(kb, 0)
            scores(kb + 2, 0)
            accumulate(kb + 1, 1)
            return carry

        trips = nk // 2 - 1
        lax.fori_loop(0, trips, body, 0, unroll=max(1, min(ATTN_UNROLL, trips)))
        scores(nk - 1, 1)
        accumulate(nk - 2, 0)
        accumulate(nk - 1, 1)
    else:
        accumulate(0, 0)
    outs = []
    for hh in range(2):
        acc = acc_scr[hh]
        outs.append(acc[0:MLA_V_DIM, :] / acc[MLA_V_DIM:MLA_V_DIM + 1, :])
    o_ref[...] = jnp.concatenate(outs, axis=0).T.astype(BF16)


def _attn_call(qcat, kcat, vt, batch, seq):
    n = qcat.shape[0]
    tq = min(ATTN_Q_TILE, seq)
    tk = min(ATTN_K_TILE, seq)
    nq = seq // tq
    pairs = MLA_HEADS // 2
    return pl.pallas_call(
        functools.partial(_attn_kernel, tk=tk, nk=seq // tk),
        grid=(batch, pairs, nq),
        in_specs=[
            pl.BlockSpec((tq, 2 * HEAD_PAD), lambda b, p, i: (b * nq + i, p)),
            pl.BlockSpec((seq, 2 * HEAD_PAD), lambda b, p, i: (b, p)),
            pl.BlockSpec((None, seq // VT_KEYS, 2 * HEAD_PAD, VT_KEYS), lambda b, p, i: (b, 0, p, 0)),
        ],
        out_specs=pl.BlockSpec((tq, 2 * MLA_V_DIM), lambda b, p, i: (b * nq + i, p)),
        out_shape=jax.ShapeDtypeStruct((n, MLA_HEADS * MLA_V_DIM), BF16),
        scratch_shapes=[pltpu.VMEM((2, 2, tk, tq), F32), pltpu.VMEM((2, 2, 1, tq), F32),
                        pltpu.VMEM((2, 1, tq), F32), pltpu.VMEM((2, HEAD_PAD, tq), F32)],
        compiler_params=pltpu.CompilerParams(dimension_semantics=("arbitrary",) * 3,
                                             vmem_limit_bytes=VMEM_LIMIT_BYTES),
        name="mla_attn",
    )(qcat, kcat, vt)


def _ret_kernel(lg_ref, q_ref, k_ref, v_ref, rg_ref, gn_ref, o_ref, rb_scr, *, chunk, nchunks):
    c_len = chunk
    pair = pl.program_id(1)
    lane = lax.broadcasted_iota(jnp.int32, (c_len, LANES), 1)
    pos = lax.broadcasted_iota(jnp.int32, (c_len, LANES), 0).astype(F32)
    ii = lax.broadcasted_iota(jnp.int32, (c_len, c_len), 0)
    jj = lax.broadcasted_iota(jnp.int32, (c_len, c_len), 1)
    diff = (ii - jj).astype(F32)
    sq = (LANES, LANES)
    two = range(2)
    lgf = [lg_ref[0, 2 * pair + hh] for hh in two]
    lgb = [lg_ref[1, 2 * pair + hh] for hh in two]
    mask = [(lane // RET_QK_DIM) == hh for hh in two]
    vsl = [slice(RET_V_DIM * hh, RET_V_DIM * (hh + 1)) for hh in two]
    zeta_f = [jnp.exp((c_len - 1.0 - pos) * lgf[hh]) for hh in two]
    xi_f = [jnp.exp((pos + 1.0) * lgf[hh]) for hh in two]
    zeta_b = [jnp.exp(pos * lgb[hh]) for hh in two]
    xi_b = [jnp.exp((c_len - pos) * lgb[hh]) for hh in two]
    gchunk_f = [jnp.exp(jnp.full(sq, c_len, F32) * lgf[hh]) for hh in two]
    gchunk_b = [jnp.exp(jnp.full(sq, c_len, F32) * lgb[hh]) for hh in two]
    decay = [jnp.where(diff >= 0.0, jnp.exp(jnp.maximum(diff, 0.0) * lgf[hh]),
                       jnp.exp(jnp.maximum(-diff, 0.0) * lgb[hh])) for hh in two]
    gn = gn_ref[...]

    def chunk_slice(c):
        return pl.ds(pl.multiple_of(c * c_len, c_len), c_len)

    def bwd_body(t, states):
        c = nchunks - 1 - t
        rows = chunk_slice(c)
        k = k_ref[rows, :].astype(F32)
        new = []
        for hh in two:
            rb_scr[hh, c] = states[hh].astype(BF16)
            kz = (jnp.where(mask[hh], k, 0.0) * zeta_b[hh]).astype(BF16)
            new.append(gchunk_b[hh] * states[hh] + _dot_tn(kz, v_ref[rows, vsl[hh]]))
        return tuple(new)

    zero_states = (jnp.zeros(sq, F32), jnp.zeros(sq, F32))
    lax.fori_loop(0, nchunks, bwd_body, zero_states, unroll=min(RET_BWD_UNROLL, nchunks))

    def fwd_body(c, states):
        rows = chunk_slice(c)
        q = q_ref[rows, :].astype(F32)
        k16 = k_ref[rows, :]
        k = k16.astype(F32)
        qm = [jnp.where(mask[hh], q, 0.0) for hh in two]
        inner = [_dot_nt(qm[hh].astype(BF16), k16) for hh in two]
        ys = []
        for hh in two:
            lhs = jnp.concatenate([(inner[hh] * decay[hh]).astype(BF16), (qm[hh] * xi_f[hh]).astype(BF16),
                                   (qm[hh] * xi_b[hh]).astype(BF16)], axis=1)
            rhs = jnp.concatenate([v_ref[rows, vsl[hh]], states[hh].astype(BF16), rb_scr[hh, c]], axis=0)
            ys.append(_dot(lhs, rhs))
        new = []
        for hh in two:
            y = ys[hh]
            mu = jnp.mean(y, axis=-1, keepdims=True)
            yc = y - mu
            var = jnp.mean(yc * yc, axis=-1, keepdims=True)
            yn = yc * lax.rsqrt(var + GN_EPS) * gn[:, vsl[hh]]
            rg = rg_ref[rows, vsl[hh]].astype(F32)
            o_ref[rows, vsl[hh]] = (rg * _sigmoid(rg) * yn).astype(BF16)
            kz = (jnp.where(mask[hh], k, 0.0) * zeta_f[hh]).astype(BF16)
            new.append(gchunk_f[hh] * states[hh] + _dot_tn(kz, v_ref[rows, vsl[hh]]))
        return tuple(new)

    lax.fori_loop(0, nchunks, fwd_body, zero_states, unroll=min(RET_FWD_UNROLL, nchunks))


def _ret_call(log_gamma, rq, rk, rv, rg, gn_g, batch, seq):
    n = rq.shape[0]
    chunk = min(RET_CHUNK, seq)
    nchunks = seq // chunk
    qk_blk = pl.BlockSpec((seq, LANES), lambda b, p: (b, p))
    v_blk = pl.BlockSpec((seq, 2 * RET_V_DIM), lambda b, p: (b, p))
    return pl.pallas_call(
        functools.partial(_ret_kernel, chunk=chunk, nchunks=nchunks),
        grid=(batch, RET_HEADS // 2),
        in_specs=[pl.BlockSpec(memory_space=pltpu.SMEM), qk_blk, qk_blk, v_blk, v_blk,
                  pl.BlockSpec((1, 2 * RET_V_DIM), lambda b, p: (0, p))],
        out_specs=v_blk,
        out_shape=jax.ShapeDtypeStruct((n, RET_V_WIDTH), BF16),
        scratch_shapes=[pltpu.VMEM((2, nchunks, LANES, LANES), BF16)],
        compiler_params=pltpu.CompilerParams(dimension_semantics=("arbitrary",) * 2,
                                             vmem_limit_bytes=VMEM_LIMIT_BYTES),
        name="retention",
    )(log_gamma, rq, rk, rv, rg, gn_g)


def _post_kernel(o_ref, r_ref, ga_ref, gb_ref, x_ref, wa_ref, wb_ref, wo_ref, gffn_ref, wr_hi_lo_ref, wr_hi_ref,
                 x2_out, xn_out, aff_out, afft_out):
    a = _dot(o_ref[...], wa_ref[...])
    r = _dot(r_ref[...], wb_ref[...])
    mixed = _sigmoid(ga_ref[...].astype(F32)) * a + _sigmoid(gb_ref[...].astype(F32)) * r
    x2 = x_ref[...] + _dot(mixed.astype(BF16), wo_ref[...])
    x2_out[...] = x2
    xn = _rms(x2, gffn_ref[...])
    xn_hi = xn.astype(BF16)
    xn_out[...] = xn_hi
    xn_lo = (xn - xn_hi.astype(F32)).astype(BF16)
    t = _dot(xn_hi, wr_hi_lo_ref[...]) + _dot(xn_lo, wr_hi_ref[...])
    logits = t + pltpu.roll(t, LANES - N_EXPERTS, 1)
    lane = lax.broadcasted_iota(jnp.int32, logits.shape, 1)
    logits = jnp.where(lane < N_EXPERTS, logits, -jnp.inf)
    e = jnp.exp(logits - jnp.max(logits, axis=-1, keepdims=True))
    aff = e / jnp.sum(e, axis=-1, keepdims=True)
    aff_out[...] = aff
    afft_out[...] = aff.T[0:N_EXPERTS, :]


def _post_call(o, r, ga, gb, x2d, weights):
    n = x2d.shape[0]
    tm = TOKEN_TILE
    row = lambda width: pl.BlockSpec((tm, width), lambda i: (i, 0))
    return pl.pallas_call(
        _post_kernel,
        grid=(n // tm,),
        in_specs=[row(MLA_HEADS * MLA_V_DIM), row(RET_V_WIDTH), row(D_MODEL), row(D_MODEL), row(D_MODEL)]
        + [_const_spec(w.shape) for w in weights],
        out_specs=[row(D_MODEL), row(D_MODEL), row(LANES), pl.BlockSpec((None, N_EXPERTS, tm), lambda i: (i, 0, 0))],
        out_shape=[jax.ShapeDtypeStruct((n, D_MODEL), F32), jax.ShapeDtypeStruct((n, D_MODEL), BF16),
                   jax.ShapeDtypeStruct((n, LANES), F32), jax.ShapeDtypeStruct((n // tm, N_EXPERTS, tm), F32)],
        compiler_params=pltpu.CompilerParams(dimension_semantics=("arbitrary",), vmem_limit_bytes=VMEM_LIMIT_BYTES),
        name="post_mix",
    )(o, r, ga, gb, x2d, *weights)


def _ffn_kernel(xe_ref, wg_ref, wu_ref, wd_ref, ye_ref):
    x = xe_ref[...]
    g = _dot(x, wg_ref[...])
    u = _dot(x, wu_ref[...])
    hid = (g * _sigmoid(g) * u).astype(BF16)
    ye_ref[...] = _dot(hid, wd_ref[...]).astype(BF16)


def _ffn_call(xe, wg, wu, wd, cap):
    n_exp, _, d = xe.shape
    tm = min(FFN_ROW_TILE, cap)
    ff = wg.shape[-1]
    return pl.pallas_call(
        _ffn_kernel,
        grid=(n_exp, cap // tm),
        in_specs=[
            pl.BlockSpec((None, tm, d), lambda e, i: (e, i, 0)),
            pl.BlockSpec((None, d, ff), lambda e, i: (e, 0, 0)),
            pl.BlockSpec((None, d, ff), lambda e, i: (e, 0, 0)),
            pl.BlockSpec((None, ff, d), lambda e, i: (e, 0, 0)),
        ],
        out_specs=pl.BlockSpec((None, tm, d), lambda e, i: (e, i, 0)),
        out_shape=jax.ShapeDtypeStruct((n_exp, cap, d), BF16),
        compiler_params=pltpu.CompilerParams(dimension_semantics=("arbitrary",) * 2,
                                             vmem_limit_bytes=VMEM_LIMIT_BYTES),
        name="expert_ffn",
    )(xe, wg, wu, wd)


def _route_kernel(afft_ref, code_ref, starts_ref, *, cap):
    nt, ne, t = afft_ref.shape
    bits = jnp.maximum(pltpu.bitcast(afft_ref[...], jnp.int32), 0)
    idx = (lax.broadcasted_iota(jnp.int32, (nt, ne, t), 0) * t + lax.broadcasted_iota(jnp.int32, (nt, ne, t), 2))
    capf = jnp.float32(cap)

    def count(flag):
        per_lane = jnp.sum(flag.astype(F32), axis=0, keepdims=True)
        return jnp.sum(per_lane, axis=2, keepdims=True)

    def thr_body(i, thr):
        cand = thr | jnp.left_shift(jnp.int32(1), 30 - i)
        return jnp.where(count(bits >= cand) >= capf, cand, thr)

    thr = lax.fori_loop(0, 31, thr_body, jnp.zeros((1, ne, 1), jnp.int32))
    above = bits > thr
    tied = bits == thr
    need = capf - count(above)
    nbits = (nt * t - 1).bit_length()

    def idx_body(i, last):
        cand = last | jnp.left_shift(jnp.int32(1), nbits - 1 - i)
        return jnp.where(count(tied & (idx < cand)) <= need - 1.0, cand, last)

    last = lax.fori_loop(0, nbits, idx_body, jnp.zeros((1, ne, 1), jnp.int32))
    sel = (above | (tied & (idx <= last))).astype(F32)

    row = lax.broadcasted_iota(jnp.int32, (t, t), 0)
    col = lax.broadcasted_iota(jnp.int32, (t, t), 1)
    earlier = (row < col).astype(BF16)
    rank = _dot(sel.reshape(nt * ne, t).astype(BF16), earlier).reshape(nt, ne, t)
    code_ref[...] = jnp.where(sel > 0.0, rank, -1.0)
    per_tile = jnp.sum(sel, axis=2, keepdims=True)
    run = jnp.zeros((ne, 1), F32)
    for b in range(nt):
        starts_ref[b] = run.astype(jnp.int32)
        run = run + per_tile[b]


def _route_call(afft, cap):
    nt, ne, t = afft.shape
    full = lambda shape: pl.BlockSpec(shape, lambda: (0,) * len(shape))
    code, starts = pl.pallas_call(
        functools.partial(_route_kernel, cap=cap),
        in_specs=[full((nt, ne, t))],
        out_specs=[full((nt, ne, t)), full((nt, ne, 1))],
        out_shape=[jax.ShapeDtypeStruct((nt, ne, t), F32), jax.ShapeDtypeStruct((nt, ne, 1), jnp.int32)],
        compiler_params=pltpu.CompilerParams(vmem_limit_bytes=VMEM_LIMIT_BYTES),
        name="route",
    )(afft)
    return code, starts.reshape(nt, ne)


def _window_start(starts_ref, tile, e):
    return pl.multiple_of(jnp.bitwise_and(starts_ref[tile, e], -WINDOW_ALIGN), WINDOW_ALIGN)


def _tile_is_short(starts_ref, tile, ntiles, cap):
    nxt = jnp.minimum(tile + 1, ntiles - 1)
    longest = jnp.int32(0)
    for e in range(N_EXPERTS):
        end = jnp.where(tile + 1 < ntiles, starts_ref[nxt, e], cap)
        longest = jnp.maximum(longest, end - starts_ref[tile, e])
    return longest <= SHORT_SEG


def _dispatch_kernel(starts_ref, xn_ref, code_ref, xe_hbm, buf, sem, *, cap):
    b = pl.program_id(0)
    nb = pl.num_programs(0)
    slot = b % 2
    t = ROUTE_TILE
    prev = jnp.maximum(b - 1, 0)
    short_now = _tile_is_short(starts_ref, b, nb, cap)
    short_prev = _tile_is_short(starts_ref, prev, nb, cap)

    def win_copy(sl, e, tile, rows):
        return pltpu.make_async_copy(buf.at[sl, e, pl.ds(0, rows), :],
                                     xe_hbm.at[e, pl.ds(_window_start(starts_ref, tile, e), rows), :],
                                     sem.at[sl, e])

    @pl.when(b == 0)
    def _():
        buf[1] = jnp.zeros(buf.shape[1:], BF16)
        fills = [pltpu.make_async_copy(buf.at[1, e], xe_hbm.at[e, pl.ds(cap, WINDOW_ROWS), :], sem.at[1, e])
                 for e in range(N_EXPERTS)]
        for f in fills:
            f.start()
        for f in fills:
            f.wait()

    def fill_windows(rows, stacked):
        code = code_ref[...]
        rowf = lax.broadcasted_iota(jnp.int32, (rows, t), 0).astype(F32)
        xn = xn_ref[...]
        takes = []
        for e in range(N_EXPERTS):
            pos = code[e:e + 1, :] + (starts_ref[b, e] - _window_start(starts_ref, b, e)).astype(F32)
            takes.append(jnp.logical_and(code[e:e + 1, :] >= 0.0, pos == rowf).astype(BF16))
        if stacked:
            wins_all = _dot(jnp.concatenate(takes, axis=0), xn)
            wins = [wins_all[rows * e:rows * (e + 1), :] for e in range(N_EXPERTS)]
        else:
            wins = [_dot(take, xn) for take in takes]
        for e in range(N_EXPERTS):
            back = pl.multiple_of(_window_start(starts_ref, b, e) - _window_start(starts_ref, prev, e), WINDOW_ALIGN)
            carry = buf[1 - slot, e, pl.ds(back, WINDOW_ALIGN), :].astype(F32)
            buf[slot, e, 0:WINDOW_ALIGN, :] = (wins[e][0:WINDOW_ALIGN, :] + carry).astype(BF16)
            buf[slot, e, WINDOW_ALIGN:rows, :] = wins[e][WINDOW_ALIGN:rows, :].astype(BF16)

    def for_windows(cond, rows, action):
        @pl.when(cond)
        def _():
            for e in range(N_EXPERTS):
                action(e, rows)

    long_now = jnp.logical_not(short_now)
    pl.when(short_now)(lambda: fill_windows(SHORT_ROWS, True))
    pl.when(long_now)(lambda: fill_windows(WINDOW_ROWS, False))
    for_windows(jnp.logical_and(b > 0, short_prev), SHORT_ROWS, lambda e, r: win_copy(1 - slot, e, b - 1, r).wait())
    for_windows(jnp.logical_and(b > 0, jnp.logical_not(short_prev)), WINDOW_ROWS,
                lambda e, r: win_copy(1 - slot, e, b - 1, r).wait())
    for_windows(short_now, SHORT_ROWS, lambda e, r: win_copy(slot, e, b, r).start())
    for_windows(long_now, WINDOW_ROWS, lambda e, r: win_copy(slot, e, b, r).start())
    last = b == nb - 1
    for_windows(jnp.logical_and(last, short_now), SHORT_ROWS, lambda e, r: win_copy(slot, e, b, r).wait())
    for_windows(jnp.logical_and(last, long_now), WINDOW_ROWS, lambda e, r: win_copy(slot, e, b, r).wait())


def _dispatch_call(starts, code, xn, cap):
    nt, ne, t = code.shape
    d = xn.shape[1]
    grid_spec = pltpu.PrefetchScalarGridSpec(
        num_scalar_prefetch=1,
        grid=(nt,),
        in_specs=[pl.BlockSpec((t, d), lambda b, s: (b, 0)),
                  pl.BlockSpec((None, ne, t), lambda b, s: (b, 0, 0))],
        out_specs=pl.BlockSpec(memory_space=pl.ANY),
        scratch_shapes=[pltpu.VMEM((2, ne, WINDOW_ROWS, d), BF16), pltpu.SemaphoreType.DMA((2, ne))],
    )
    return pl.pallas_call(
        functools.partial(_dispatch_kernel, cap=cap),
        grid_spec=grid_spec,
        out_shape=jax.ShapeDtypeStruct((ne, cap + WINDOW_ROWS, d), BF16),
        compiler_params=pltpu.CompilerParams(dimension_semantics=("arbitrary",), vmem_limit_bytes=VMEM_LIMIT_BYTES),
        name="dispatch",
    )(starts, xn, code)


def _combine_kernel(starts_ref, x2_ref, aff_ref, code_ref, gfin_ref, ye_hbm, o_ref, buf, sem, *, cap):
    b = pl.program_id(0)
    nb = pl.num_programs(0)
    slot = b % 2
    t = ROUTE_TILE
    nxt = jnp.minimum(b + 1, nb - 1)
    short_now = _tile_is_short(starts_ref, b, nb, cap)
    short_next = _tile_is_short(starts_ref, nxt, nb, cap)
    long_now = jnp.logical_not(short_now)

    def win_start(tile, e, rows):
        aligned = jnp.bitwise_and(starts_ref[tile, e], -WINDOW_ALIGN)
        return pl.multiple_of(jnp.minimum(aligned, cap - rows), WINDOW_ALIGN)

    def win_copy(sl, e, tile, rows):
        return pltpu.make_async_copy(ye_hbm.at[e, pl.ds(win_start(tile, e, rows), rows), :],
                                     buf.at[sl, e, pl.ds(0, rows), :], sem.at[sl, e])

    def for_windows(cond, rows, action):
        @pl.when(cond)
        def _():
            for e in range(N_EXPERTS):
                action(e, rows)

    first = b == 0
    for_windows(jnp.logical_and(first, short_now), SHORT_ROWS, lambda e, r: win_copy(0, e, 0, r).start())
    for_windows(jnp.logical_and(first, long_now), WINDOW_ROWS, lambda e, r: win_copy(0, e, 0, r).start())
    more = b + 1 < nb
    for_windows(jnp.logical_and(more, short_next), SHORT_ROWS, lambda e, r: win_copy(1 - slot, e, b + 1, r).start())
    for_windows(jnp.logical_and(more, jnp.logical_not(short_next)), WINDOW_ROWS,
                lambda e, r: win_copy(1 - slot, e, b + 1, r).start())

    def ranks_and_gates():
        row = lax.broadcasted_iota(jnp.int32, (t, t), 0)
        col = lax.broadcasted_iota(jnp.int32, (t, t), 1)
        ranks = _dot_nt((row == col).astype(BF16), code_ref[...].astype(BF16))
        return ranks, jnp.where(ranks >= 0.0, aff_ref[:, 0:N_EXPERTS], 0.0)

    def long_path():
        ranks, gates = ranks_and_gates()
        colf = lax.broadcasted_iota(jnp.int32, (t, t), 1).astype(F32)
        for e in range(N_EXPERTS):
            win_copy(slot, e, b, WINDOW_ROWS).wait()
        acc = jnp.zeros((t, D_MODEL), F32)
        tail = jnp.zeros((t, t), F32)
        for e in range(N_EXPERTS):
            pos = ranks[:, e:e + 1] + (starts_ref[b, e] - win_start(b, e, WINDOW_ROWS)).astype(F32)
            g = gates[:, e:e + 1]
            acc = acc + _dot(jnp.where(pos == colf, g, 0.0).astype(BF16), buf[slot, e, 0:t, :])
            in_tail = jnp.logical_and(pos >= float(t), pos - float(t - WINDOW_ALIGN * e) == colf)
            tail = tail + jnp.where(in_tail, g, 0.0)
        tail_rows = jnp.concatenate([buf[slot, e, t:WINDOW_ROWS, :] for e in range(N_EXPERTS)], axis=0)
        acc = acc + _dot(tail.astype(BF16), tail_rows)
        o_ref[...] = _rms(x2_ref[...] + acc, gfin_ref[...])

    def short_path():
        ranks, gates = ranks_and_gates()
        width = N_EXPERTS * SHORT_ROWS
        lane = lax.broadcasted_iota(jnp.int32, (1, N_EXPERTS), 1)
        shift = jnp.zeros((1, N_EXPERTS), F32)
        for e in range(N_EXPERTS):
            shift = jnp.where(lane == e, (starts_ref[b, e] - win_start(b, e, SHORT_ROWS)).astype(F32), shift)
        assert SHORT_ROWS & (SHORT_ROWS - 1) == 0
        group = jnp.right_shift(lax.broadcasted_iota(jnp.int32, (N_EXPERTS, width), 1), SHORT_ROWS.bit_length() - 1)
        spread = (group == lax.broadcasted_iota(jnp.int32, (N_EXPERTS, width), 0)).astype(BF16)
        pos = _dot((ranks + shift).astype(BF16), spread)
        g = _dot(gates.astype(BF16), spread)
        colf = jnp.bitwise_and(lax.broadcasted_iota(jnp.int32, (t, width), 1), SHORT_ROWS - 1).astype(F32)
        for e in range(N_EXPERTS):
            win_copy(slot, e, b, SHORT_ROWS).wait()
        rows = jnp.concatenate([buf[slot, e, 0:SHORT_ROWS, :] for e in range(N_EXPERTS)], axis=0)
        acc = _dot(jnp.where(pos == colf, g, 0.0).astype(BF16), rows)
        o_ref[...] = _rms(x2_ref[...] + acc, gfin_ref[...])

    pl.when(short_now)(short_path)
    pl.when(long_now)(long_path)


def _combine_call(starts, x2, aff, code, final_g, ye, cap):
    nt, ne, t = code.shape
    n, d = x2.shape
    assert WINDOW_ALIGN * ne == t and cap >= WINDOW_ROWS and cap % WINDOW_ALIGN == 0
    grid_spec = pltpu.PrefetchScalarGridSpec(
        num_scalar_prefetch=1,
        grid=(nt,),
        in_specs=[pl.BlockSpec((t, d), lambda b, s: (b, 0)),
                  pl.BlockSpec((t, LANES), lambda b, s: (b, 0)),
                  pl.BlockSpec((None, ne, t), lambda b, s: (b, 0, 0)),
                  pl.BlockSpec((1, d), lambda b, s: (0, 0)),
                  pl.BlockSpec(memory_space=pl.ANY)],
        out_specs=pl.BlockSpec((t, d), lambda b, s: (b, 0)),
        scratch_shapes=[pltpu.VMEM((2, ne, WINDOW_ROWS, d), BF16), pltpu.SemaphoreType.DMA((2, ne))],
    )
    return pl.pallas_call(
        functools.partial(_combine_kernel, cap=cap),
        grid_spec=grid_spec,
        out_shape=jax.ShapeDtypeStruct((n, d), F32),
        compiler_params=pltpu.CompilerParams(dimension_semantics=("arbitrary",), vmem_limit_bytes=VMEM_LIMIT_BYTES),
        name="combine",
    )(starts, x2, aff, code, final_g, ye)


def _rope_angles(seq, dim):
    inv = 1.0 / (ROPE_THETA ** (jnp.arange(0, dim, 2, dtype=F32) / dim))
    ang = jnp.arange(seq, dtype=F32)[:, None] * inv[None, :]
    return jnp.cos(ang), jnp.sin(ang)


def _mla_tables(seq, scale):
    c, s = _rope_angles(seq, MLA_ROPE_DIM)
    ones = jnp.ones((seq, MLA_NOPE_DIM), F32)
    zeros_n = jnp.zeros((seq, MLA_NOPE_DIM), F32)
    zeros_p = jnp.zeros((seq, HEAD_PAD - MLA_QK_DIM), F32)
    cos = jnp.concatenate([ones, c, c, zeros_p], axis=1) * scale
    sin = jnp.concatenate([zeros_n, -s, s, zeros_p], axis=1) * scale
    return cos, sin


def _ret_tables(seq, scale):
    c, s = _rope_angles(seq, RET_QK_DIM)
    cos = jnp.concatenate([c, c, c, c], axis=1) * scale
    sin = jnp.concatenate([-s, s, -s, s], axis=1) * scale
    return cos, sin


def _prepare_weights(norm_mix_g, w_in, q_norm_g, w_uq, kv_norm_g, w_ukv, ret_gn_g, w_branch_a, w_branch_b, w_out,
                     norm_ffn_g, w_router, norm_final_g):
    offs, acc = [], 0
    for width in IN_SPLITS:
        offs.append((acc, acc + width))
        acc += width
    w_cq, w_ckv, w_kr, w_rq, w_rk, w_rv, w_rg, w_ga, w_gb = [w_in[:, a:b].astype(BF16) for a, b in offs]
    w_kr = jnp.pad(w_kr, ((0, 0), (0, LANES - MLA_ROPE_DIM)))

    uq = w_uq.reshape(Q_LORA_RANK, MLA_HEADS, MLA_QK_DIM)
    wq_pad = jnp.pad(uq, ((0, 0), (0, 0), (0, HEAD_PAD - MLA_QK_DIM))).reshape(Q_LORA_RANK, MLA_PAD_WIDTH)
    ukv = w_ukv.reshape(KV_LORA_RANK, MLA_HEADS, MLA_NOPE_DIM + MLA_V_DIM)
    wk_pad = jnp.pad(ukv[:, :, :MLA_NOPE_DIM], ((0, 0), (0, 0), (0, HEAD_PAD - MLA_NOPE_DIM)))
    wk_pad = wk_pad.reshape(KV_LORA_RANK, MLA_PAD_WIDTH)
    wv = jnp.pad(ukv[:, :, MLA_NOPE_DIM:], ((0, 0), (0, 0), (0, HEAD_PAD - MLA_V_DIM)))
    wv = wv.reshape(KV_LORA_RANK, MLA_PAD_WIDTH).T
    src = jnp.arange(LANES)[:, None]
    dst = jnp.arange(MLA_PAD_WIDTH)[None, :]
    place = ((dst % HEAD_PAD) - MLA_NOPE_DIM == src) & (src < MLA_ROPE_DIM)

    pre_w = (norm_mix_g.reshape(1, -1), w_cq, w_ckv, w_kr, w_rq, w_rk, w_rv, w_rg, w_ga, w_gb,
             q_norm_g.reshape(1, -1), wq_pad.astype(BF16), kv_norm_g.reshape(1, -1), wk_pad.astype(BF16),
             place.astype(BF16), wv.astype(BF16))

    wr_hi = w_router.astype(BF16)
    wr_lo = (w_router - wr_hi.astype(F32)).astype(BF16)
    pad_to = lambda w: jnp.pad(w, ((0, 0), (0, LANES - w.shape[1])))
    post_w = (w_branch_a.astype(BF16), w_branch_b.astype(BF16), w_out.astype(BF16), norm_ffn_g.reshape(1, -1),
              pad_to(jnp.concatenate([wr_hi, wr_lo], axis=1)), pad_to(wr_hi))
    return pre_w, post_w, ret_gn_g.reshape(1, -1), norm_final_g.reshape(1, -1)


def _encode(x, pre_w, post_w, log_gamma, gn_g, final_g, wg, wu, wd):
    batch, seq, d = x.shape
    n = batch * seq
    x2d = x.reshape(n, d)
    tables = (*_mla_tables(seq, MLA_QK_DIM ** -0.5 * LOG2_E), *_mla_tables(seq, 1.0),
              *_ret_tables(seq, 1.0), *_ret_tables(seq, RET_QK_DIM ** -0.5))
    qcat, kcat, vt, rq, rk, rv, rg, ga, gb = _pre_call(x2d, seq, pre_w, tables)
    o = _attn_call(qcat, kcat, vt, batch, seq)
    r = _ret_call(log_gamma, rq, rk, rv, rg, gn_g, batch, seq)
    x2, xn, aff, afft = _post_call(o, r, ga, gb, x2d, post_w)

    cap = CAPACITY_FACTOR * n // N_EXPERTS
    code, starts = _route_call(afft, cap)
    xe = _dispatch_call(starts, code, xn, cap)
    ye = _ffn_call(xe, wg, wu, wd, cap)
    return _combine_call(starts, x2, aff, code, final_g, ye, cap).reshape(batch, seq, d)


def kernel(x_prompt, x_sample, norm_mix_g, w_in, q_norm_g, w_uq, kv_norm_g, w_ukv, ret_decay_fwd, ret_decay_bwd,
           ret_gn_g, w_branch_a, w_branch_b, w_out, norm_ffn_g, w_router, w_exp_gate, w_exp_up, w_exp_down,
           norm_final_g):
    assert norm_mix_g.shape[0] == 1, "single-layer trunk"
    pre_w, post_w, gn_g, final_g = _prepare_weights(
        norm_mix_g[0], w_in[0], q_norm_g[0], w_uq[0], kv_norm_g[0], w_ukv[0], ret_gn_g[0], w_branch_a[0],
        w_branch_b[0], w_out[0], norm_ffn_g[0], w_router[0], norm_final_g)
    log_gamma = jnp.stack([jax.nn.log_sigmoid(ret_decay_fwd[0].astype(F32)),
                           jax.nn.log_sigmoid(ret_decay_bwd[0].astype(F32))])
    wg = w_exp_gate[0].astype(BF16)
    wu = w_exp_up[0].astype(BF16)
    wd = w_exp_down[0].astype(BF16)
    enc = functools.partial(_encode, pre_w=pre_w, post_w=post_w, log_gamma=log_gamma, gn_g=gn_g, final_g=final_g,
                            wg=wg, wu=wu, wd=wd)
    return enc(x_prompt), enc(x_sample)
```

```python
import functools

import jax
import jax.numpy as jnp
from jax import lax
from jax.experimental import pallas as pl
from jax.experimental.pallas import tpu as pltpu

D_MODEL = 1024
MLA_HEADS = 8
MLA_NOPE_DIM = 64
MLA_ROPE_DIM = 32
MLA_V_DIM = 64
MLA_QK_DIM = MLA_NOPE_DIM + MLA_ROPE_DIM
Q_LORA_RANK = 384
KV_LORA_RANK = 256
RET_HEADS = 8
RET_QK_DIM = 64
RET_V_DIM = 128
RET_QK_WIDTH = RET_HEADS * RET_QK_DIM
RET_V_WIDTH = RET_HEADS * RET_V_DIM
N_EXPERTS = 16
EXPERT_FF = 2816
CAPACITY_FACTOR = 2
ROPE_THETA = 10000.0
EPS = 1e-6
GN_EPS = 1e-5
IN_SPLITS = (Q_LORA_RANK, KV_LORA_RANK, MLA_ROPE_DIM, RET_QK_WIDTH, RET_QK_WIDTH, RET_V_WIDTH, RET_V_WIDTH,
             D_MODEL, D_MODEL)

LANES = 128
HEAD_PAD = LANES
MLA_PAD_WIDTH = MLA_HEADS * HEAD_PAD
VMEM_LIMIT_BYTES = 56 * 1024 * 1024

TOKEN_TILE = 256
PRE_TILE = 512
POST_TILE = 256
VT_KEYS = PRE_TILE
LOG2_E = 1.4426950408889634
ROUTE_TILE = TOKEN_TILE
WINDOW_ALIGN = 16
WINDOW_ROWS = ROUTE_TILE + WINDOW_ALIGN
SHORT_SEG = 48
SHORT_ROWS = SHORT_SEG + WINDOW_ALIGN
ATTN_Q_TILE = 512
ATTN_K_TILE = 512
ATTN_UNROLL = 3
RET_CHUNK = 256
RET_BWD_UNROLL = 8
RET_FWD_UNROLL = 8
FFN_ROW_TILE = 256

F32 = jnp.float32
BF16 = jnp.bfloat16


def _dot(a, b):
    return jnp.dot(a, b, preferred_element_type=F32)


def _dot_nt(a, b):
    return lax.dot_general(a, b, (((1,), (1,)), ((), ())), preferred_element_type=F32)


def _dot_tn(a, b):
    return lax.dot_general(a, b, (((0,), (0,)), ((), ())), preferred_element_type=F32)


def _rms(x, g):
    return x * lax.rsqrt(jnp.mean(x * x, axis=-1, keepdims=True) + EPS) * g


def _sigmoid(x):
    return 1.0 / (1.0 + jnp.exp(-x))


def _rope_lanes(blk, cos, sin, half):
    lane = lax.broadcasted_iota(jnp.int32, blk.shape, 1)
    upper = (lane % (2 * half)) >= half
    partner = jnp.where(upper, pltpu.roll(blk, half, 1), pltpu.roll(blk, LANES - half, 1))
    return blk * cos + partner * sin


def _const_spec(shape):
    nd = len(shape)
    return pl.BlockSpec(shape, lambda *_: (0,) * nd, pipeline_mode=pl.Buffered(1))


def _pre_kernel(x_ref, gmix_ref, wcq_ref, wckv_ref, wkr_ref, wrq_ref, wrk_ref, wrv_ref, wrg_ref, wga_ref, wgb_ref,
                gq_ref, wq_ref, gkv_ref, wk_ref, place_ref, wv_ref,
                cosq_ref, sinq_ref, cosk_ref, sink_ref, cosrq_ref, sinrq_ref, cosrk_ref, sinrk_ref,
                q_out, k_out, vt_out, rq_out, rk_out, rv_out, rg_out, ga_out, gb_out):
    h = _rms(x_ref[...], gmix_ref[...]).astype(BF16)

    qn = _rms(_dot(h, wcq_ref[...]), gq_ref[...]).astype(BF16)
    q = _dot(qn, wq_ref[...])
    cosq, sinq = cosq_ref[...], sinq_ref[...]
    for j in range(MLA_HEADS):
        sl = slice(HEAD_PAD * j, HEAD_PAD * (j + 1))
        q_out[:, sl] = _rope_lanes(q[:, sl], cosq, sinq, MLA_ROPE_DIM // 2).astype(BF16)

    kvn = _rms(_dot(h, wckv_ref[...]), gkv_ref[...]).astype(BF16)
    kr = _dot(h, wkr_ref[...]).astype(BF16)
    k = _dot(kvn, wk_ref[...]) + _dot(kr, place_ref[...])
    cosk, sink = cosk_ref[...], sink_ref[...]
    for j in range(MLA_HEADS):
        sl = slice(HEAD_PAD * j, HEAD_PAD * (j + 1))
        k_out[:, sl] = _rope_lanes(k[:, sl], cosk, sink, MLA_ROPE_DIM // 2).astype(BF16)
    vt = _dot_nt(wv_ref[...], kvn)
    vrow = lax.broadcasted_iota(jnp.int32, vt.shape, 0)
    vt_out[...] = jnp.where(vrow % HEAD_PAD == MLA_V_DIM, 1.0, vt).astype(BF16)

    rq = _dot(h, wrq_ref[...])
    rk = _dot(h, wrk_ref[...])
    cosrq, sinrq, cosrk, sinrk = cosrq_ref[...], sinrq_ref[...], cosrk_ref[...], sinrk_ref[...]
    for j in range(RET_QK_WIDTH // LANES):
        sl = slice(LANES * j, LANES * (j + 1))
        rq_out[:, sl] = _rope_lanes(rq[:, sl], cosrq, sinrq, RET_QK_DIM // 2).astype(BF16)
        rk_out[:, sl] = _rope_lanes(rk[:, sl], cosrk, sinrk, RET_QK_DIM // 2).astype(BF16)

    rv_out[...] = _dot(h, wrv_ref[...]).astype(BF16)
    rg_out[...] = _dot(h, wrg_ref[...]).astype(BF16)
    ga_out[...] = _dot(h, wga_ref[...]).astype(BF16)
    gb_out[...] = _dot(h, wgb_ref[...]).astype(BF16)


def _pre_call(x2d, seq, weights, tables):
    n = x2d.shape[0]
    tm = PRE_TILE
    tiles_per_seq = seq // tm
    row = lambda width: pl.BlockSpec((tm, width), lambda i: (i, 0))
    tab = pl.BlockSpec((tm, LANES), lambda i: (i % tiles_per_seq, 0))
    out_widths = (MLA_PAD_WIDTH, MLA_PAD_WIDTH, None, RET_QK_WIDTH, RET_QK_WIDTH,
                  RET_V_WIDTH, RET_V_WIDTH, D_MODEL, D_MODEL)
    vt_spec = pl.BlockSpec((None, None, MLA_PAD_WIDTH, tm), lambda i: (i // tiles_per_seq, i % tiles_per_seq, 0, 0))
    vt_shape = jax.ShapeDtypeStruct((n // seq, tiles_per_seq, MLA_PAD_WIDTH, tm), BF16)
    return pl.pallas_call(
        _pre_kernel,
        grid=(n // tm,),
        in_specs=[row(D_MODEL)] + [_const_spec(w.shape) for w in weights] + [tab] * len(tables),
        out_specs=[vt_spec if w is None else row(w) for w in out_widths],
        out_shape=[vt_shape if w is None else jax.ShapeDtypeStruct((n, w), BF16) for w in out_widths],
        compiler_params=pltpu.CompilerParams(dimension_semantics=("arbitrary",), vmem_limit_bytes=VMEM_LIMIT_BYTES),
        name="pre_proj",
    )(x2d, *weights, *tables)


def _attn_kernel(q_ref, k_ref, vt_ref, o_ref, st_scr, mc_scr, m_scr, acc_scr, *, tk, nk):
    sub = tk // VT_KEYS
    heads = [slice(HEAD_PAD * hh, HEAD_PAD * (hh + 1)) for hh in range(2)]

    def scores(kb, buf):
        off = pl.multiple_of(kb * tk, tk)
        for hh, sl in enumerate(heads):
            st = _dot_nt(k_ref[pl.ds(off, tk), sl], q_ref[:, sl])
            st_scr[buf, hh] = st
            mc_scr[buf, hh] = jnp.max(st, axis=0, keepdims=True)

    def accumulate(kb, buf):
        for hh, sl in enumerate(heads):
            m_old = m_scr[hh]
            m_new = jnp.maximum(m_old, mc_scr[buf, hh])
            alpha = jnp.exp2(m_old - m_new)
            pt = jnp.exp2(st_scr[buf, hh] - m_new).astype(BF16)
            pv = _dot(vt_ref[kb * sub, sl, :], pt[0:VT_KEYS, :])
            for j in range(1, sub):
                pv = pv + _dot(vt_ref[kb * sub + j, sl, :], pt[VT_KEYS * j:VT_KEYS * (j + 1), :])
            acc_scr[hh] = alpha * acc_scr[hh] + pv
            m_scr[hh] = m_new

    m_scr[...] = jnp.full(m_scr.shape, -jnp.inf, F32)
    acc_scr[...] = jnp.zeros(acc_scr.shape, F32)
    scores(0, 0)
    if nk > 1:
        assert nk % 2 == 0

        def body(j, carry):
            kb = 2 * j
            scores(kb + 1, 1)
            accumulate(kb, 0)
            scores(kb + 2, 0)
            accumulate(kb + 1, 1)
            return carry

        trips = nk // 2 - 1
        lax.fori_loop(0, trips, body, 0, unroll=ATTN_UNROLL if trips >= 2 * ATTN_UNROLL else 1)
        scores(nk - 1, 1)
        accumulate(nk - 2, 0)
        accumulate(nk - 1, 1)
    else:
        accumulate(0, 0)
    outs = []
    for hh in range(2):
        acc = acc_scr[hh]
        outs.append(acc[0:MLA_V_DIM, :] / acc[MLA_V_DIM:MLA_V_DIM + 1, :])
    o_ref[...] = jnp.concatenate(outs, axis=0).T.astype(BF16)


def _attn_call(qcat, kcat, vt, batch, seq):
    n = qcat.shape[0]
    tq = min(ATTN_Q_TILE, seq)
    tk = min(ATTN_K_TILE, seq)
    nq = seq // tq
    pairs = MLA_HEADS // 2
    return pl.pallas_call(
        functools.partial(_attn_kernel, tk=tk, nk=seq // tk),
        grid=(batch, pairs, nq),
        in_specs=[
            pl.BlockSpec((tq, 2 * HEAD_PAD), lambda b, p, i: (b * nq + i, p)),
            pl.BlockSpec((seq, 2 * HEAD_PAD), lambda b, p, i: (b, p)),
            pl.BlockSpec((None, seq // VT_KEYS, 2 * HEAD_PAD, VT_KEYS), lambda b, p, i: (b, 0, p, 0)),
        ],
        out_specs=pl.BlockSpec((tq, 2 * MLA_V_DIM), lambda b, p, i: (b * nq + i, p)),
        out_shape=jax.ShapeDtypeStruct((n, MLA_HEADS * MLA_V_DIM), BF16),
        scratch_shapes=[pltpu.VMEM((2, 2, tk, tq), F32), pltpu.VMEM((2, 2, 1, tq), F32),
                        pltpu.VMEM((2, 1, tq), F32), pltpu.VMEM((2, HEAD_PAD, tq), F32)],
        compiler_params=pltpu.CompilerParams(dimension_semantics=("arbitrary",) * 3,
                                             vmem_limit_bytes=VMEM_LIMIT_BYTES),
        name="mla_attn",
    )(qcat, kcat, vt)


def _ret_kernel(lg_ref, q_ref, k_ref, v_ref, rg_ref, gn_ref, o_ref, rb_scr, *, chunk, nchunks):
    c_len = chunk
    pair = pl.program_id(1)
    lane = lax.broadcasted_iota(jnp.int32, (c_len, LANES), 1)
    pos = lax.broadcasted_iota(jnp.int32, (c_len, LANES), 0).astype(F32)
    ii = lax.broadcasted_iota(jnp.int32, (c_len, c_len), 0)
    jj = lax.broadcasted_iota(jnp.int32, (c_len, c_len), 1)
    diff = (ii - jj).astype(F32)
    sq = (LANES, LANES)
    two = range(2)
    lgf = [lg_ref[0, 2 * pair + hh] for hh in two]
    lgb = [lg_ref[1, 2 * pair + hh] for hh in two]
    mask = [(lane // RET_QK_DIM) == hh for hh in two]
    vsl = [slice(RET_V_DIM * hh, RET_V_DIM * (hh + 1)) for hh in two]
    zeta_f = [jnp.exp((c_len - 1.0 - pos) * lgf[hh]) for hh in two]
    xi_f = [jnp.exp((pos + 1.0) * lgf[hh]) for hh in two]
    zeta_b = [jnp.exp(pos * lgb[hh]) for hh in two]
    xi_b = [jnp.exp((c_len - pos) * lgb[hh]) for hh in two]
    gchunk_f = [jnp.exp(jnp.full(sq, c_len, F32) * lgf[hh]) for hh in two]
    gchunk_b = [jnp.exp(jnp.full(sq, c_len, F32) * lgb[hh]) for hh in two]
    decay = [jnp.where(diff >= 0.0, jnp.exp(jnp.maximum(diff, 0.0) * lgf[hh]),
                       jnp.exp(jnp.maximum(-diff, 0.0) * lgb[hh])) for hh in two]
    gn = gn_ref[...]

    def chunk_slice(c):
        return pl.ds(pl.multiple_of(c * c_len, c_len), c_len)

    def bwd_body(t, states):
        c = nchunks - 1 - t
        rows = chunk_slice(c)
        k = k_ref[rows, :].astype(F32)
        new = []
        for hh in two:
            rb_scr[hh, c] = states[hh].astype(BF16)
            kz = (jnp.where(mask[hh], k, 0.0) * zeta_b[hh]).astype(BF16)
            new.append(gchunk_b[hh] * states[hh] + _dot_tn(kz, v_ref[rows, vsl[hh]]))
        return tuple(new)

    zero_states = (jnp.zeros(sq, F32), jnp.zeros(sq, F32))
    lax.fori_loop(0, nchunks, bwd_body, zero_states, unroll=min(RET_BWD_UNROLL, nchunks))

    def fwd_body(c, states):
        rows = chunk_slice(c)
        q = q_ref[rows, :].astype(F32)
        k16 = k_ref[rows, :]
        k = k16.astype(F32)
        qm = [jnp.where(mask[hh], q, 0.0) for hh in two]
        inner = [_dot_nt(qm[hh].astype(BF16), k16) for hh in two]
        ys = []
        for hh in two:
            lhs = jnp.concatenate([(inner[hh] * decay[hh]).astype(BF16), (qm[hh] * xi_f[hh]).astype(BF16),
                                   (qm[hh] * xi_b[hh]).astype(BF16)], axis=1)
            rhs = jnp.concatenate([v_ref[rows, vsl[hh]], states[hh].astype(BF16), rb_scr[hh, c]], axis=0)
            ys.append(_dot(lhs, rhs))
        new = []
        for hh in two:
            y = ys[hh]
            mu = jnp.mean(y, axis=-1, keepdims=True)
            yc = y - mu
            var = jnp.mean(yc * yc, axis=-1, keepdims=True)
            yn = yc * lax.rsqrt(var + GN_EPS) * gn[:, vsl[hh]]
            rg = rg_ref[rows, vsl[hh]].astype(F32)
            o_ref[rows, vsl[hh]] = (rg * _sigmoid(rg) * yn).astype(BF16)
            kz = (jnp.where(mask[hh], k, 0.0) * zeta_f[hh]).astype(BF16)
            new.append(gchunk_f[hh] * states[hh] + _dot_tn(kz, v_ref[rows, vsl[hh]]))
        return tuple(new)

    lax.fori_loop(0, nchunks, fwd_body, zero_states, unroll=min(RET_FWD_UNROLL, nchunks))


def _ret_call(log_gamma, rq, rk, rv, rg, gn_g, batch, seq):
    n = rq.shape[0]
    chunk = min(RET_CHUNK, seq)
    nchunks = seq // chunk
    qk_blk = pl.BlockSpec((seq, LANES), lambda b, p: (b, p))
    v_blk = pl.BlockSpec((seq, 2 * RET_V_DIM), lambda b, p: (b, p))
    return pl.pallas_call(
        functools.partial(_ret_kernel, chunk=chunk, nchunks=nchunks),
        grid=(batch, RET_HEADS // 2),
        in_specs=[pl.BlockSpec(memory_space=pltpu.SMEM), qk_blk, qk_blk, v_blk, v_blk,
                  pl.BlockSpec((1, 2 * RET_V_DIM), lambda b, p: (0, p))],
        out_specs=v_blk,
        out_shape=jax.ShapeDtypeStruct((n, RET_V_WIDTH), BF16),
        scratch_shapes=[pltpu.VMEM((2, nchunks, LANES, LANES), BF16)],
        compiler_params=pltpu.CompilerParams(dimension_semantics=("arbitrary",) * 2,
                                             vmem_limit_bytes=VMEM_LIMIT_BYTES),
        name="retention",
    )(log_gamma, rq, rk, rv, rg, gn_g)


def _post_kernel(o_ref, r_ref, ga_ref, gb_ref, x_ref, wa_ref, wb_ref, wo_ref, gffn_ref, wr_hi_lo_ref, wr_hi_ref,
                 x2_out, xn_out, aff_out, afft_out):
    a = _dot(o_ref[...], wa_ref[...])
    r = _dot(r_ref[...], wb_ref[...])
    mixed = _sigmoid(ga_ref[...].astype(F32)) * a + _sigmoid(gb_ref[...].astype(F32)) * r
    x2 = x_ref[...] + _dot(mixed.astype(BF16), wo_ref[...])
    x2_out[...] = x2
    xn = _rms(x2, gffn_ref[...])
    xn_hi = xn.astype(BF16)
    xn_out[...] = xn_hi
    xn_lo = (xn - xn_hi.astype(F32)).astype(BF16)
    t = _dot(xn_hi, wr_hi_lo_ref[...]) + _dot(xn_lo, wr_hi_ref[...])
    logits = t + pltpu.roll(t, LANES - N_EXPERTS, 1)
    lane = lax.broadcasted_iota(jnp.int32, logits.shape, 1)
    logits = jnp.where(lane < N_EXPERTS, logits, -jnp.inf)
    e = jnp.exp(logits - jnp.max(logits, axis=-1, keepdims=True))
    aff = e / jnp.sum(e, axis=-1, keepdims=True)
    aff_out[...] = aff
    for r in range(afft_out.shape[0]):
        afft_out[r] = aff[ROUTE_TILE * r:ROUTE_TILE * (r + 1), :].T[0:N_EXPERTS, :]


def _post_call(o, r, ga, gb, x2d, weights):
    n = x2d.shape[0]
    tm = POST_TILE
    rt = ROUTE_TILE
    row = lambda width: pl.BlockSpec((tm, width), lambda i: (i, 0))
    return pl.pallas_call(
        _post_kernel,
        grid=(n // tm,),
        in_specs=[row(MLA_HEADS * MLA_V_DIM), row(RET_V_WIDTH), row(D_MODEL), row(D_MODEL), row(D_MODEL)]
        + [_const_spec(w.shape) for w in weights],
        out_specs=[row(D_MODEL), row(D_MODEL), row(LANES),
                   pl.BlockSpec((tm // rt, N_EXPERTS, rt), lambda i: (i, 0, 0))],
        out_shape=[jax.ShapeDtypeStruct((n, D_MODEL), F32), jax.ShapeDtypeStruct((n, D_MODEL), BF16),
                   jax.ShapeDtypeStruct((n, LANES), F32), jax.ShapeDtypeStruct((n // rt, N_EXPERTS, rt), F32)],
        compiler_params=pltpu.CompilerParams(dimension_semantics=("arbitrary",), vmem_limit_bytes=VMEM_LIMIT_BYTES),
        name="post_mix",
    )(o, r, ga, gb, x2d, *weights)


def _ffn_kernel(xe_ref, wg_ref, wu_ref, wd_ref, ye_ref):
    x = xe_ref[...]
    g = _dot(x, wg_ref[...])
    u = _dot(x, wu_ref[...])
    hid = (g * _sigmoid(g) * u).astype(BF16)
    ye_ref[...] = _dot(hid, wd_ref[...]).astype(BF16)


def _ffn_call(xe, wg, wu, wd, cap):
    n_exp, _, d = xe.shape
    tm = min(FFN_ROW_TILE, cap)
    ff = wg.shape[-1]
    return pl.pallas_call(
        _ffn_kernel,
        grid=(n_exp, cap // tm),
        in_specs=[
            pl.BlockSpec((None, tm, d), lambda e, i: (e, i, 0)),
            pl.BlockSpec((None, d, ff), lambda e, i: (e, 0, 0)),
            pl.BlockSpec((None, d, ff), lambda e, i: (e, 0, 0)),
            pl.BlockSpec((None, ff, d), lambda e, i: (e, 0, 0)),
        ],
        out_specs=pl.BlockSpec((None, tm, d), lambda e, i: (e, i, 0)),
        out_shape=jax.ShapeDtypeStruct((n_exp, cap, d), BF16),
        compiler_params=pltpu.CompilerParams(dimension_semantics=("arbitrary",) * 2,
                                             vmem_limit_bytes=VMEM_LIMIT_BYTES),
        name="expert_ffn",
    )(xe, wg, wu, wd)


def _route_kernel(afft_ref, code_ref, starts_ref, *, cap):
    nt, ne, t = afft_ref.shape
    bits = jnp.maximum(pltpu.bitcast(afft_ref[...], jnp.int32), 0)
    idx = (lax.broadcasted_iota(jnp.int32, (nt, ne, t), 0) * t + lax.broadcasted_iota(jnp.int32, (nt, ne, t), 2))
    capf = jnp.float32(cap)

    def count(flag):
        per_lane = jnp.sum(flag.astype(F32), axis=0, keepdims=True)
        return jnp.sum(per_lane, axis=2, keepdims=True)

    def thr_body(i, thr):
        cand = thr | jnp.left_shift(jnp.int32(1), 30 - i)
        return jnp.where(count(bits >= cand) >= capf, cand, thr)

    thr = lax.fori_loop(0, 31, thr_body, jnp.zeros((1, ne, 1), jnp.int32))
    above = bits > thr
    tied = bits == thr
    need = capf - count(above)
    nbits = (nt * t - 1).bit_length()

    def idx_body(i, last):
        cand = last | jnp.left_shift(jnp.int32(1), nbits - 1 - i)
        return jnp.where(count(tied & (idx < cand)) <= need - 1.0, cand, last)

    last = lax.fori_loop(0, nbits, idx_body, jnp.zeros((1, ne, 1), jnp.int32))
    sel = (above | (tied & (idx <= last))).astype(F32)

    row = lax.broadcasted_iota(jnp.int32, (t, t), 0)
    col = lax.broadcasted_iota(jnp.int32, (t, t), 1)
    earlier = (row < col).astype(BF16)
    rank = _dot(sel.reshape(nt * ne, t).astype(BF16), earlier).reshape(nt, ne, t)
    code_ref[...] = jnp.where(sel > 0.0, rank, -1.0)
    per_tile = jnp.sum(sel, axis=2, keepdims=True)
    run = jnp.zeros((ne, 1), F32)
    for b in range(nt):
        starts_ref[b] = run.astype(jnp.int32)
        run = run + per_tile[b]


def _route_call(afft, cap):
    nt, ne, t = afft.shape
    full = lambda shape: pl.BlockSpec(shape, lambda: (0,) * len(shape))
    code, starts = pl.pallas_call(
        functools.partial(_route_kernel, cap=cap),
        in_specs=[full((nt, ne, t))],
        out_specs=[full((nt, ne, t)), full((nt, ne, 1))],
        out_shape=[jax.ShapeDtypeStruct((nt, ne, t), F32), jax.ShapeDtypeStruct((nt, ne, 1), jnp.int32)],
        compiler_params=pltpu.CompilerParams(vmem_limit_bytes=VMEM_LIMIT_BYTES),
        name="route",
    )(afft)
    return code, starts.reshape(nt, ne)


def _window_start(starts_ref, tile, e):
    return pl.multiple_of(jnp.bitwise_and(starts_ref[tile, e], -WINDOW_ALIGN), WINDOW_ALIGN)


def _tile_is_short(starts_ref, tile, ntiles, cap):
    nxt = jnp.minimum(tile + 1, ntiles - 1)
    longest = jnp.int32(0)
    for e in range(N_EXPERTS):
        end = jnp.where(tile + 1 < ntiles, starts_ref[nxt, e], cap)
        longest = jnp.maximum(longest, end - starts_ref[tile, e])
    return longest <= SHORT_SEG


def _dispatch_kernel(starts_ref, xn_ref, code_ref, xe_hbm, buf, sem, *, cap):
    b = pl.program_id(0)
    nb = pl.num_programs(0)
    slot = b % 2
    t = ROUTE_TILE
    prev = jnp.maximum(b - 1, 0)
    short_now = _tile_is_short(starts_ref, b, nb, cap)
    short_prev = _tile_is_short(starts_ref, prev, nb, cap)

    def win_copy(sl, e, tile, rows):
        return pltpu.make_async_copy(buf.at[sl, e, pl.ds(0, rows), :],
                                     xe_hbm.at[e, pl.ds(_window_start(starts_ref, tile, e), rows), :],
                                     sem.at[sl, e])

    @pl.when(b == 0)
    def _():
        buf[1] = jnp.zeros(buf.shape[1:], BF16)
        fills = [pltpu.make_async_copy(buf.at[1, e], xe_hbm.at[e, pl.ds(cap, WINDOW_ROWS), :], sem.at[1, e])
                 for e in range(N_EXPERTS)]
        for f in fills:
            f.start()
        for f in fills:
            f.wait()

    def fill_windows(rows, stacked):
        code = code_ref[...]
        rowf = lax.broadcasted_iota(jnp.int32, (rows, t), 0).astype(F32)
        xn = xn_ref[...]
        takes = []
        for e in range(N_EXPERTS):
            pos = code[e:e + 1, :] + (starts_ref[b, e] - _window_start(starts_ref, b, e)).astype(F32)
            takes.append(jnp.logical_and(code[e:e + 1, :] >= 0.0, pos == rowf).astype(BF16))
        if stacked:
            wins_all = _dot(jnp.concatenate(takes, axis=0), xn)
            wins = [wins_all[rows * e:rows * (e + 1), :] for e in range(N_EXPERTS)]
        else:
            wins = [_dot(take, xn) for take in takes]
        for e in range(N_EXPERTS):
            back = pl.multiple_of(_window_start(starts_ref, b, e) - _window_start(starts_ref, prev, e), WINDOW_ALIGN)
            carry = buf[1 - slot, e, pl.ds(back, WINDOW_ALIGN), :].astype(F32)
            buf[slot, e, 0:WINDOW_ALIGN, :] = (wins[e][0:WINDOW_ALIGN, :] + carry).astype(BF16)
            buf[slot, e, WINDOW_ALIGN:rows, :] = wins[e][WINDOW_ALIGN:rows, :].astype(BF16)

    def for_windows(cond, rows, action):
        @pl.when(cond)
        def _():
            for e in range(N_EXPERTS):
                action(e, rows)

    long_now = jnp.logical_not(short_now)
    pl.when(short_now)(lambda: fill_windows(SHORT_ROWS, True))
    pl.when(long_now)(lambda: fill_windows(WINDOW_ROWS, False))
    for_windows(jnp.logical_and(b > 0, short_prev), SHORT_ROWS, lambda e, r: win_copy(1 - slot, e, b - 1, r).wait())
    for_windows(jnp.logical_and(b > 0, jnp.logical_not(short_prev)), WINDOW_ROWS,
                lambda e, r: win_copy(1 - slot, e, b - 1, r).wait())
    for_windows(short_now, SHORT_ROWS, lambda e, r: win_copy(slot, e, b, r).start())
    for_windows(long_now, WINDOW_ROWS, lambda e, r: win_copy(slot, e, b, r).start())
    last = b == nb - 1
    for_windows(jnp.logical_and(last, short_now), SHORT_ROWS, lambda e, r: win_copy(slot, e, b, r).wait())
    for_windows(jnp.logical_and(last, long_now), WINDOW_ROWS, lambda e, r: win_copy(slot, e, b, r).wait())


def _dispatch_call(starts, code, xn, cap):
    nt, ne, t = code.shape
    d = xn.shape[1]
    grid_spec = pltpu.PrefetchScalarGridSpec(
        num_scalar_prefetch=1,
        grid=(nt,),
        in_specs=[pl.BlockSpec((t, d), lambda b, s: (b, 0)),
                  pl.BlockSpec((None, ne, t), lambda b, s: (b, 0, 0))],
        out_specs=pl.BlockSpec(memory_space=pl.ANY),
        scratch_shapes=[pltpu.VMEM((2, ne, WINDOW_ROWS, d), BF16), pltpu.SemaphoreType.DMA((2, ne))],
    )
    return pl.pallas_call(
        functools.partial(_dispatch_kernel, cap=cap),
        grid_spec=grid_spec,
        out_shape=jax.ShapeDtypeStruct((ne, cap + WINDOW_ROWS, d), BF16),
        compiler_params=pltpu.CompilerParams(dimension_semantics=("arbitrary",), vmem_limit_bytes=VMEM_LIMIT_BYTES),
        name="dispatch",
    )(starts, xn, code)


def _combine_kernel(starts_ref, x2_ref, aff_ref, code_ref, gfin_ref, ye_hbm, o_ref, buf, sem, *, cap):
    b = pl.program_id(0)
    nb = pl.num_programs(0)
    slot = b % 2
    t = ROUTE_TILE
    nxt = jnp.minimum(b + 1, nb - 1)
    short_now = _tile_is_short(starts_ref, b, nb, cap)
    short_next = _tile_is_short(starts_ref, nxt, nb, cap)
    long_now = jnp.logical_not(short_now)

    def win_start(tile, e, rows):
        aligned = jnp.bitwise_and(starts_ref[tile, e], -WINDOW_ALIGN)
        return pl.multiple_of(jnp.minimum(aligned, cap - rows), WINDOW_ALIGN)

    def win_copy(sl, e, tile, rows):
        return pltpu.make_async_copy(ye_hbm.at[e, pl.ds(win_start(tile, e, rows), rows), :],
                                     buf.at[sl, e, pl.ds(0, rows), :], sem.at[sl, e])

    def for_windows(cond, rows, action):
        @pl.when(cond)
        def _():
            for e in range(N_EXPERTS):
                action(e, rows)

    first = b == 0
    for_windows(jnp.logical_and(first, short_now), SHORT_ROWS, lambda e, r: win_copy(0, e, 0, r).start())
    for_windows(jnp.logical_and(first, long_now), WINDOW_ROWS, lambda e, r: win_copy(0, e, 0, r).start())
    more = b + 1 < nb
    for_windows(jnp.logical_and(more, short_next), SHORT_ROWS, lambda e, r: win_copy(1 - slot, e, b + 1, r).start())
    for_windows(jnp.logical_and(more, jnp.logical_not(short_next)), WINDOW_ROWS,
                lambda e, r: win_copy(1 - slot, e, b + 1, r).start())

    def ranks_and_gates():
        row = lax.broadcasted_iota(jnp.int32, (t, t), 0)
        col = lax.broadcasted_iota(jnp.int32, (t, t), 1)
        ranks = _dot_nt((row == col).astype(BF16), code_ref[...].astype(BF16))
        return ranks, jnp.where(ranks >= 0.0, aff_ref[:, 0:N_EXPERTS], 0.0)

    def long_path():
        ranks, gates = ranks_and_gates()
        colf = lax.broadcasted_iota(jnp.int32, (t, t), 1).astype(F32)
        for e in range(N_EXPERTS):
            win_copy(slot, e, b, WINDOW_ROWS).wait()
        acc = jnp.zeros((t, D_MODEL), F32)
        tail = jnp.zeros((t, t), F32)
        for e in range(N_EXPERTS):
            pos = ranks[:, e:e + 1] + (starts_ref[b, e] - win_start(b, e, WINDOW_ROWS)).astype(F32)
            g = gates[:, e:e + 1]
            acc = acc + _dot(jnp.where(pos == colf, g, 0.0).astype(BF16), buf[slot, e, 0:t, :])
            in_tail = jnp.logical_and(pos >= float(t), pos - float(t - WINDOW_ALIGN * e) == colf)
            tail = tail + jnp.where(in_tail, g, 0.0)
        tail_rows = jnp.concatenate([buf[slot, e, t:WINDOW_ROWS, :] for e in range(N_EXPERTS)], axis=0)
        acc = acc + _dot(tail.astype(BF16), tail_rows)
        o_ref[...] = _rms(x2_ref[...] + acc, gfin_ref[...])

    def short_path():
        ranks, gates = ranks_and_gates()
        width = N_EXPERTS * SHORT_ROWS
        lane = lax.broadcasted_iota(jnp.int32, (1, N_EXPERTS), 1)
        shift = jnp.zeros((1, N_EXPERTS), F32)
        for e in range(N_EXPERTS):
            shift = jnp.where(lane == e, (starts_ref[b, e] - win_start(b, e, SHORT_ROWS)).astype(F32), shift)
        assert SHORT_ROWS & (SHORT_ROWS - 1) == 0
        group = jnp.right_shift(lax.broadcasted_iota(jnp.int32, (N_EXPERTS, width), 1), SHORT_ROWS.bit_length() - 1)
        spread = (group == lax.broadcasted_iota(jnp.int32, (N_EXPERTS, width), 0)).astype(BF16)
        pos = _dot((ranks + shift).astype(BF16), spread)
        g = _dot(gates.astype(BF16), spread)
        colf = jnp.bitwise_and(lax.broadcasted_iota(jnp.int32, (t, width), 1), SHORT_ROWS - 1).astype(F32)
        for e in range(N_EXPERTS):
            win_copy(slot, e, b, SHORT_ROWS).wait()
        rows = jnp.concatenate([buf[slot, e, 0:SHORT_ROWS, :] for e in range(N_EXPERTS)], axis=0)
        acc = _dot(jnp.where(pos == colf, g, 0.0).astype(BF16), rows)
        o_ref[...] = _rms(x2_ref[...] + acc, gfin_ref[...])

    pl.when(short_now)(short_path)
    pl.when(long_now)(long_path)


def _combine_call(starts, x2, aff, code, final_g, ye, cap):
    nt, ne, t = code.shape
    n, d = x2.shape
    assert WINDOW_ALIGN * ne == t and cap >= WINDOW_ROWS and cap % WINDOW_ALIGN == 0
    grid_spec = pltpu.PrefetchScalarGridSpec(
        num_scalar_prefetch=1,
        grid=(nt,),
        in_specs=[pl.BlockSpec((t, d), lambda b, s: (b, 0)),
                  pl.BlockSpec((t, LANES), lambda b, s: (b, 0)),
                  pl.BlockSpec((None, ne, t), lambda b, s: (b, 0, 0)),
                  pl.BlockSpec((1, d), lambda b, s: (0, 0)),
                  pl.BlockSpec(memory_space=pl.ANY)],
        out_specs=pl.BlockSpec((t, d), lambda b, s: (b, 0)),
        scratch_shapes=[pltpu.VMEM((2, ne, WINDOW_ROWS, d), BF16), pltpu.SemaphoreType.DMA((2, ne))],
    )
    return pl.pallas_call(
        functools.partial(_combine_kernel, cap=cap),
        grid_spec=grid_spec,
        out_shape=jax.ShapeDtypeStruct((n, d), F32),
        compiler_params=pltpu.CompilerParams(dimension_semantics=("arbitrary",), vmem_limit_bytes=VMEM_LIMIT_BYTES),
        name="combine",
    )(starts, x2, aff, code, final_g, ye)


def _rope_angles(seq, dim):
    inv = 1.0 / (ROPE_THETA ** (jnp.arange(0, dim, 2, dtype=F32) / dim))
    ang = jnp.arange(seq, dtype=F32)[:, None] * inv[None, :]
    return jnp.cos(ang), jnp.sin(ang)


def _mla_tables(seq, scale):
    c, s = _rope_angles(seq, MLA_ROPE_DIM)
    ones = jnp.ones((seq, MLA_NOPE_DIM), F32)
    zeros_n = jnp.zeros((seq, MLA_NOPE_DIM), F32)
    zeros_p = jnp.zeros((seq, HEAD_PAD - MLA_QK_DIM), F32)
    cos = jnp.concatenate([ones, c, c, zeros_p], axis=1) * scale
    sin = jnp.concatenate([zeros_n, -s, s, zeros_p], axis=1) * scale
    return cos, sin


def _ret_tables(seq, scale):
    c, s = _rope_angles(seq, RET_QK_DIM)
    cos = jnp.concatenate([c, c, c, c], axis=1) * scale
    sin = jnp.concatenate([-s, s, -s, s], axis=1) * scale
    return cos, sin


def _prepare_weights(norm_mix_g, w_in, q_norm_g, w_uq, kv_norm_g, w_ukv, ret_gn_g, w_branch_a, w_branch_b, w_out,
                     norm_ffn_g, w_router, norm_final_g):
    offs, acc = [], 0
    for width in IN_SPLITS:
        offs.append((acc, acc + width))
        acc += width
    w_cq, w_ckv, w_kr, w_rq, w_rk, w_rv, w_rg, w_ga, w_gb = [w_in[:, a:b].astype(BF16) for a, b in offs]
    w_kr = jnp.pad(w_kr, ((0, 0), (0, LANES - MLA_ROPE_DIM)))

    uq = w_uq.reshape(Q_LORA_RANK, MLA_HEADS, MLA_QK_DIM)
    wq_pad = jnp.pad(uq, ((0, 0), (0, 0), (0, HEAD_PAD - MLA_QK_DIM))).reshape(Q_LORA_RANK, MLA_PAD_WIDTH)
    ukv = w_ukv.reshape(KV_LORA_RANK, MLA_HEADS, MLA_NOPE_DIM + MLA_V_DIM)
    wk_pad = jnp.pad(ukv[:, :, :MLA_NOPE_DIM], ((0, 0), (0, 0), (0, HEAD_PAD - MLA_NOPE_DIM)))
    wk_pad = wk_pad.reshape(KV_LORA_RANK, MLA_PAD_WIDTH)
    wv = jnp.pad(ukv[:, :, MLA_NOPE_DIM:], ((0, 0), (0, 0), (0, HEAD_PAD - MLA_V_DIM)))
    wv = wv.reshape(KV_LORA_RANK, MLA_PAD_WIDTH).T
    src = jnp.arange(LANES)[:, None]
    dst = jnp.arange(MLA_PAD_WIDTH)[None, :]
    place = ((dst % HEAD_PAD) - MLA_NOPE_DIM == src) & (src < MLA_ROPE_DIM)

    pre_w = (norm_mix_g.reshape(1, -1), w_cq, w_ckv, w_kr, w_rq, w_rk, w_rv, w_rg, w_ga, w_gb,
             q_norm_g.reshape(1, -1), wq_pad.astype(BF16), kv_norm_g.reshape(1, -1), wk_pad.astype(BF16),
             place.astype(BF16), wv.astype(BF16))

    wr_hi = w_router.astype(BF16)
    wr_lo = (w_router - wr_hi.astype(F32)).astype(BF16)
    pad_to = lambda w: jnp.pad(w, ((0, 0), (0, LANES - w.shape[1])))
    post_w = (w_branch_a.astype(BF16), w_branch_b.astype(BF16), w_out.astype(BF16), norm_ffn_g.reshape(1, -1),
              pad_to(jnp.concatenate([wr_hi, wr_lo], axis=1)), pad_to(wr_hi))
    return pre_w, post_w, ret_gn_g.reshape(1, -1), norm_final_g.reshape(1, -1)


def _encode(x, pre_w, post_w, log_gamma, gn_g, final_g, wg, wu, wd):
    batch, seq, d = x.shape
    n = batch * seq
    x2d = x.reshape(n, d)
    tables = (*_mla_tables(seq, MLA_QK_DIM ** -0.5 * LOG2_E), *_mla_tables(seq, 1.0),
              *_ret_tables(seq, 1.0), *_ret_tables(seq, RET_QK_DIM ** -0.5))
    qcat, kcat, vt, rq, rk, rv, rg, ga, gb = _pre_call(x2d, seq, pre_w, tables)
    o = _attn_call(qcat, kcat, vt, batch, seq)
    r = _ret_call(log_gamma, rq, rk, rv, rg, gn_g, batch, seq)
    x2, xn, aff, afft = _post_call(o, r, ga, gb, x2d, post_w)

    cap = CAPACITY_FACTOR * n // N_EXPERTS
    code, starts = _route_call(afft, cap)
    xe = _dispatch_call(starts, code, xn, cap)
    ye = _ffn_call(xe, wg, wu, wd, cap)
    return _combine_call(starts, x2, aff, code, final_g, ye, cap).reshape(batch, seq, d)


def kernel(x_prompt, x_sample, norm_mix_g, w_in, q_norm_g, w_uq, kv_norm_g, w_ukv, ret_decay_fwd, ret_decay_bwd,
           ret_gn_g, w_branch_a, w_branch_b, w_out, norm_ffn_g, w_router, w_exp_gate, w_exp_up, w_exp_down,
           norm_final_g):
    assert norm_mix_g.shape[0] == 1, "single-layer trunk"
    pre_w, post_w, gn_g, final_g = _prepare_weights(
        norm_mix_g[0], w_in[0], q_norm_g[0], w_uq[0], kv_norm_g[0], w_ukv[0], ret_gn_g[0], w_branch_a[0],
        w_branch_b[0], w_out[0], norm_ffn_g[0], w_router[0], norm_final_g)
    log_gamma = jnp.stack([jax.nn.log_sigmoid(ret_decay_fwd[0].astype(F32)),
                           jax.nn.log_sigmoid(ret_decay_bwd[0].astype(F32))])
    wg = w_exp_gate[0].astype(BF16)
    wu = w_exp_up[0].astype(BF16)
    wd = w_exp_down[0].astype(BF16)
    enc = functools.partial(_encode, pre_w=pre_w, post_w=post_w, log_gamma=log_gamma, gn_g=gn_g, final_g=final_g,
                            wg=wg, wu=wu, wd=wd)
    return enc(x_prompt), enc(x_sample)
```

```python
import functools

import jax
import jax.numpy as jnp
from jax import lax
from jax.experimental import pallas as pl
from jax.experimental.pallas import tpu as pltpu

D_MODEL = 1024
MLA_HEADS = 8
MLA_NOPE_DIM = 64
MLA_ROPE_DIM = 32
MLA_V_DIM = 64
MLA_QK_DIM = MLA_NOPE_DIM + MLA_ROPE_DIM
Q_LORA_RANK = 384
KV_LORA_RANK = 256
RET_HEADS = 8
RET_QK_DIM = 64
RET_V_DIM = 128
RET_QK_WIDTH = RET_HEADS * RET_QK_DIM
RET_V_WIDTH = RET_HEADS * RET_V_DIM
N_EXPERTS = 16
EXPERT_FF = 2816
CAPACITY_FACTOR = 2
ROPE_THETA = 10000.0
EPS = 1e-6
GN_EPS = 1e-5
IN_SPLITS = (Q_LORA_RANK, KV_LORA_RANK, MLA_ROPE_DIM, RET_QK_WIDTH, RET_QK_WIDTH, RET_V_WIDTH, RET_V_WIDTH,
             D_MODEL, D_MODEL)

LANES = 128
HEAD_PAD = LANES
MLA_PAD_WIDTH = MLA_HEADS * HEAD_PAD
VMEM_LIMIT_BYTES = 56 * 1024 * 1024

TOKEN_TILE = 256
PRE_TILE = 512
POST_TILE = 256
VT_KEYS = PRE_TILE
LOG2_E = 1.4426950408889634
ROUTE_TILE = TOKEN_TILE
WINDOW_ALIGN = 16
WINDOW_ROWS = ROUTE_TILE + WINDOW_ALIGN
SHORT_SEG = 48
SHORT_ROWS = SHORT_SEG + WINDOW_ALIGN
ATTN_Q_TILE = 512
ATTN_K_TILE = 512
ATTN_UNROLL = 3
ATTN_JUMP_LIMIT = 60.0
RET_CHUNK = 256
RET_BWD_UNROLL = 8
RET_FWD_UNROLL = 8
FFN_ROW_TILE = 256

F32 = jnp.float32
BF16 = jnp.bfloat16


def _dot(a, b):
    return jnp.dot(a, b, preferred_element_type=F32)


def _dot_nt(a, b):
    return lax.dot_general(a, b, (((1,), (1,)), ((), ())), preferred_element_type=F32)


def _dot_tn(a, b):
    return lax.dot_general(a, b, (((0,), (0,)), ((), ())), preferred_element_type=F32)


def _rms(x, g):
    return x * lax.rsqrt(jnp.mean(x * x, axis=-1, keepdims=True) + EPS) * g


def _sigmoid(x):
    return 1.0 / (1.0 + jnp.exp(-x))


def _rope_lanes(blk, cos, sin, half):
    lane = lax.broadcasted_iota(jnp.int32, blk.shape, 1)
    upper = (lane % (2 * half)) >= half
    partner = jnp.where(upper, pltpu.roll(blk, half, 1), pltpu.roll(blk, LANES - half, 1))
    return blk * cos + partner * sin


def _const_spec(shape):
    nd = len(shape)
    return pl.BlockSpec(shape, lambda *_: (0,) * nd, pipeline_mode=pl.Buffered(1))


def _pre_kernel(x_ref, gmix_ref, wcq_ref, wckv_ref, wkr_ref, wrq_ref, wrk_ref, wrv_ref, wrg_ref, wga_ref, wgb_ref,
                gq_ref, wq_ref, gkv_ref, wk_ref, place_ref, wv_ref,
                cosq_ref, sinq_ref, cosk_ref, sink_ref, cosrq_ref, sinrq_ref, cosrk_ref, sinrk_ref,
                q_out, k_out, vt_out, rq_out, rk_out, rv_out, rg_out, ga_out, gb_out):
    h = _rms(x_ref[...], gmix_ref[...]).astype(BF16)

    qn = _rms(_dot(h, wcq_ref[...]), gq_ref[...]).astype(BF16)
    q = _dot(qn, wq_ref[...])
    cosq, sinq = cosq_ref[...], sinq_ref[...]
    for j in range(MLA_HEADS):
        sl = slice(HEAD_PAD * j, HEAD_PAD * (j + 1))
        q_out[:, sl] = _rope_lanes(q[:, sl], cosq, sinq, MLA_ROPE_DIM // 2).astype(BF16)

    kvn = _rms(_dot(h, wckv_ref[...]), gkv_ref[...]).astype(BF16)
    kr = _dot(h, wkr_ref[...]).astype(BF16)
    k = _dot(kvn, wk_ref[...]) + _dot(kr, place_ref[...])
    cosk, sink = cosk_ref[...], sink_ref[...]
    for j in range(MLA_HEADS):
        sl = slice(HEAD_PAD * j, HEAD_PAD * (j + 1))
        k_out[:, sl] = _rope_lanes(k[:, sl], cosk, sink, MLA_ROPE_DIM // 2).astype(BF16)
    vt = _dot_nt(wv_ref[...], kvn)
    vrow = lax.broadcasted_iota(jnp.int32, vt.shape, 0)
    vt_out[...] = jnp.where(vrow % HEAD_PAD == MLA_V_DIM, 1.0, vt).astype(BF16)

    rq = _dot(h, wrq_ref[...])
    rk = _dot(h, wrk_ref[...])
    cosrq, sinrq, cosrk, sinrk = cosrq_ref[...], sinrq_ref[...], cosrk_ref[...], sinrk_ref[...]
    for j in range(RET_QK_WIDTH // LANES):
        sl = slice(LANES * j, LANES * (j + 1))
        rq_out[:, sl] = _rope_lanes(rq[:, sl], cosrq, sinrq, RET_QK_DIM // 2).astype(BF16)
        rk_out[:, sl] = _rope_lanes(rk[:, sl], cosrk, sinrk, RET_QK_DIM // 2).astype(BF16)

    rv_out[...] = _dot(h, wrv_ref[...]).astype(BF16)
    rg_out[...] = _dot(h, wrg_ref[...]).astype(BF16)
    ga_out[...] = _dot(h, wga_ref[...]).astype(BF16)
    gb_out[...] = _dot(h, wgb_ref[...]).astype(BF16)


def _pre_call(x2d, seq, weights, tables):
    n = x2d.shape[0]
    tm = PRE_TILE
    tiles_per_seq = seq // tm
    row = lambda width: pl.BlockSpec((tm, width), lambda i: (i, 0))
    tab = pl.BlockSpec((tm, LANES), lambda i: (i % tiles_per_seq, 0))
    out_widths = (MLA_PAD_WIDTH, MLA_PAD_WIDTH, None, RET_QK_WIDTH, RET_QK_WIDTH,
                  RET_V_WIDTH, RET_V_WIDTH, D_MODEL, D_MODEL)
    vt_spec = pl.BlockSpec((None, None, MLA_PAD_WIDTH, tm), lambda i: (i // tiles_per_seq, i % tiles_per_seq, 0, 0))
    vt_shape = jax.ShapeDtypeStruct((n // seq, tiles_per_seq, MLA_PAD_WIDTH, tm), BF16)
    return pl.pallas_call(
        _pre_kernel,
        grid=(n // tm,),
        in_specs=[row(D_MODEL)] + [_const_spec(w.shape) for w in weights] + [tab] * len(tables),
        out_specs=[vt_spec if w is None else row(w) for w in out_widths],
        out_shape=[vt_shape if w is None else jax.ShapeDtypeStruct((n, w), BF16) for w in out_widths],
        compiler_params=pltpu.CompilerParams(dimension_semantics=("arbitrary",), vmem_limit_bytes=VMEM_LIMIT_BYTES),
        name="pre_proj",
    )(x2d, *weights, *tables)


def _attn_kernel(q_ref, k_ref, vt_ref, o_ref, st_scr, pt_scr, mc_scr, m_scr, acc_scr, jump_scr, *, tk, nk):
    sub = tk // VT_KEYS
    heads = [slice(HEAD_PAD * hh, HEAD_PAD * (hh + 1)) for hh in range(2)]

    def qk(kb, sl):
        off = pl.multiple_of(kb * tk, tk)
        return _dot_nt(k_ref[pl.ds(off, tk), sl], q_ref[:, sl])

    def value_product(kb, sl, pt):
        pv = _dot(vt_ref[kb * sub, sl, :], pt[0:VT_KEYS, :])
        for j in range(1, sub):
            pv = pv + _dot(vt_ref[kb * sub + j, sl, :], pt[VT_KEYS * j:VT_KEYS * (j + 1), :])
        return pv

    def pipeline(first, scores, accumulate):
        first()
        if nk > 1:
            assert nk % 2 == 0

            def body(j, carry):
                kb = 2 * j
                scores(kb + 1, 1)
                accumulate(kb, 0)
                scores(kb + 2, 0)
                accumulate(kb + 1, 1)
                return carry

            trips = nk // 2 - 1
            lax.fori_loop(0, trips, body, 0, unroll=ATTN_UNROLL if trips >= 2 * ATTN_UNROLL else 1)
            scores(nk - 1, 1)
            accumulate(nk - 2, 0)
            accumulate(nk - 1, 1)
        else:
            accumulate(0, 0)
        outs = []
        for hh in range(2):
            acc = acc_scr[hh]
            outs.append(acc[0:MLA_V_DIM, :] / acc[MLA_V_DIM:MLA_V_DIM + 1, :])
        o_ref[...] = jnp.concatenate(outs, axis=0).T.astype(BF16)

    def fast_first():
        acc_scr[...] = jnp.zeros(acc_scr.shape, F32)
        jump_scr[...] = jnp.zeros(jump_scr.shape, F32)
        for hh, sl in enumerate(heads):
            st = qk(0, sl)
            bmax = jnp.max(st, axis=0, keepdims=True)
            pt_scr[0, hh] = jnp.exp2(st - bmax).astype(BF16)
            mc_scr[0, hh] = bmax
            m_scr[hh] = bmax

    def fast_scores(kb, buf):
        for hh, sl in enumerate(heads):
            st = qk(kb, sl)
            ref = jnp.maximum(m_scr[hh], mc_scr[1 - buf, hh])
            pt_scr[buf, hh] = jnp.exp2(st - ref).astype(BF16)
            mc_scr[buf, hh] = jnp.max(st, axis=0, keepdims=True)

    def fast_accumulate(kb, buf):
        for hh, sl in enumerate(heads):
            m_old = m_scr[hh]
            bmax = mc_scr[buf, hh]
            m_new = jnp.maximum(m_old, bmax)
            jump_scr[hh] = jnp.maximum(jump_scr[hh], bmax - m_old)
            acc_scr[hh] = (acc_scr[hh] + value_product(kb, sl, pt_scr[buf, hh])) * jnp.exp2(m_old - m_new)
            m_scr[hh] = m_new

    def safe_first():
        m_scr[...] = jnp.full(m_scr.shape, -jnp.inf, F32)
        acc_scr[...] = jnp.zeros(acc_scr.shape, F32)
        safe_scores(0, 0)

    def safe_scores(kb, buf):
        for hh, sl in enumerate(heads):
            st = qk(kb, sl)
            st_scr[buf, hh] = st
            mc_scr[buf, hh] = jnp.max(st, axis=0, keepdims=True)

    def safe_accumulate(kb, buf):
        for hh, sl in enumerate(heads):
            m_old = m_scr[hh]
            m_new = jnp.maximum(m_old, mc_scr[buf, hh])
            pt = jnp.exp2(st_scr[buf, hh] - m_new).astype(BF16)
            acc_scr[hh] = jnp.exp2(m_old - m_new) * acc_scr[hh] + value_product(kb, sl, pt)
            m_scr[hh] = m_new

    pipeline(fast_first, fast_scores, fast_accumulate)
    jump = jnp.max(jnp.maximum(jump_scr[0], jump_scr[1]))
    pl.when(jump > ATTN_JUMP_LIMIT)(lambda: pipeline(safe_first, safe_scores, safe_accumulate))


def _attn_call(qcat, kcat, vt, batch, seq):
    n = qcat.shape[0]
    tq = min(ATTN_Q_TILE, seq)
    tk = min(ATTN_K_TILE, seq)
    nq = seq // tq
    pairs = MLA_HEADS // 2
    return pl.pallas_call(
        functools.partial(_attn_kernel, tk=tk, nk=seq // tk),
        grid=(batch, pairs, nq),
        in_specs=[
            pl.BlockSpec((tq, 2 * HEAD_PAD), lambda b, p, i: (b * nq + i, p)),
            pl.BlockSpec((seq, 2 * HEAD_PAD), lambda b, p, i: (b, p)),
            pl.BlockSpec((None, seq // VT_KEYS, 2 * HEAD_PAD, VT_KEYS), lambda b, p, i: (b, 0, p, 0)),
        ],
        out_specs=pl.BlockSpec((tq, 2 * MLA_V_DIM), lambda b, p, i: (b * nq + i, p)),
        out_shape=jax.ShapeDtypeStruct((n, MLA_HEADS * MLA_V_DIM), BF16),
        scratch_shapes=[pltpu.VMEM((2, 2, tk, tq), F32), pltpu.VMEM((2, 2, tk, tq), BF16),
                        pltpu.VMEM((2, 2, 1, tq), F32), pltpu.VMEM((2, 1, tq), F32),
                        pltpu.VMEM((2, HEAD_PAD, tq), F32), pltpu.VMEM((2, 1, tq), F32)],
        compiler_params=pltpu.CompilerParams(dimension_semantics=("arbitrary",) * 3,
                                             vmem_limit_bytes=VMEM_LIMIT_BYTES),
        name="mla_attn",
    )(qcat, kcat, vt)


def _ret_kernel(lg_ref, q_ref, k_ref, v_ref, rg_ref, gn_ref, o_ref, rb_scr, *, chunk, nchunks):
    c_len = chunk
    pair = pl.program_id(1)
    lane = lax.broadcasted_iota(jnp.int32, (c_len, LANES), 1)
    pos = lax.broadcasted_iota(jnp.int32, (c_len, LANES), 0).astype(F32)
    ii = lax.broadcasted_iota(jnp.int32, (c_len, c_len), 0)
    jj = lax.broadcasted_iota(jnp.int32, (c_len, c_len), 1)
    diff = (ii - jj).astype(F32)
    sq = (LANES, LANES)
    two = range(2)
    lgf = [lg_ref[0, 2 * pair + hh] for hh in two]
    lgb = [lg_ref[1, 2 * pair + hh] for hh in two]
    mask = [(lane // RET_QK_DIM) == hh for hh in two]
    vsl = [slice(RET_V_DIM * hh, RET_V_DIM * (hh + 1)) for hh in two]
    zeta_f = [jnp.exp((c_len - 1.0 - pos) * lgf[hh]) for hh in two]
    xi_f = [jnp.exp((pos + 1.0) * lgf[hh]) for hh in two]
    zeta_b = [jnp.exp(pos * lgb[hh]) for hh in two]
    xi_b = [jnp.exp((c_len - pos) * lgb[hh]) for hh in two]
    gchunk_f = [jnp.exp(jnp.full(sq, c_len, F32) * lgf[hh]) for hh in two]
    gchunk_b = [jnp.exp(jnp.full(sq, c_len, F32) * lgb[hh]) for hh in two]
    decay = [jnp.where(diff >= 0.0, jnp.exp(jnp.maximum(diff, 0.0) * lgf[hh]),
                       jnp.exp(jnp.maximum(-diff, 0.0) * lgb[hh])) for hh in two]
    gn = gn_ref[...]

    def chunk_slice(c):
        return pl.ds(pl.multiple_of(c * c_len, c_len), c_len)

    def bwd_body(t, states):
        c = nchunks - 1 - t
        rows = chunk_slice(c)
        k = k_ref[rows, :].astype(F32)
        new = []
        for hh in two:
            rb_scr[hh, c] = states[hh].astype(BF16)
            kz = (jnp.where(mask[hh], k, 0.0) * zeta_b[hh]).astype(BF16)
            new.append(gchunk_b[hh] * states[hh] + _dot_tn(kz, v_ref[rows, vsl[hh]]))
        return tuple(new)

    zero_states = (jnp.zeros(sq, F32), jnp.zeros(sq, F32))
    lax.fori_loop(0, nchunks, bwd_body, zero_states, unroll=min(RET_BWD_UNROLL, nchunks))

    def fwd_body(c, states):
        rows = chunk_slice(c)
        q = q_ref[rows, :].astype(F32)
        k16 = k_ref[rows, :]
        k = k16.astype(F32)
        qm = [jnp.where(mask[hh], q, 0.0) for hh in two]
        inner = [_dot_nt(qm[hh].astype(BF16), k16) for hh in two]
        ys = []
        for hh in two:
            lhs = jnp.concatenate([(inner[hh] * decay[hh]).astype(BF16), (qm[hh] * xi_f[hh]).astype(BF16),
                                   (qm[hh] * xi_b[hh]).astype(BF16)], axis=1)
            rhs = jnp.concatenate([v_ref[rows, vsl[hh]], states[hh].astype(BF16), rb_scr[hh, c]], axis=0)
            ys.append(_dot(lhs, rhs))
        new = []
        for hh in two:
            y = ys[hh]
            mu = jnp.mean(y, axis=-1, keepdims=True)
            yc = y - mu
            var = jnp.mean(yc * yc, axis=-1, keepdims=True)
            yn = yc * lax.rsqrt(var + GN_EPS) * gn[:, vsl[hh]]
            rg = rg_ref[rows, vsl[hh]].astype(F32)
            o_ref[rows, vsl[hh]] = (rg * _sigmoid(rg) * yn).astype(BF16)
            kz = (jnp.where(mask[hh], k, 0.0) * zeta_f[hh]).astype(BF16)
            new.append(gchunk_f[hh] * states[hh] + _dot_tn(kz, v_ref[rows, vsl[hh]]))
        return tuple(new)

    lax.fori_loop(0, nchunks, fwd_body, zero_states, unroll=min(RET_FWD_UNROLL, nchunks))


def _ret_call(log_gamma, rq, rk, rv, rg, gn_g, batch, seq):
    n = rq.shape[0]
    chunk = min(RET_CHUNK, seq)
    nchunks = seq // chunk
    qk_blk = pl.BlockSpec((seq, LANES), lambda b, p: (b, p))
    v_blk = pl.BlockSpec((seq, 2 * RET_V_DIM), lambda b, p: (b, p))
    return pl.pallas_call(
        functools.partial(_ret_kernel, chunk=chunk, nchunks=nchunks),
        grid=(batch, RET_HEADS // 2),
        in_specs=[pl.BlockSpec(memory_space=pltpu.SMEM), qk_blk, qk_blk, v_blk, v_blk,
                  pl.BlockSpec((1, 2 * RET_V_DIM), lambda b, p: (0, p))],
        out_specs=v_blk,
        out_shape=jax.ShapeDtypeStruct((n, RET_V_WIDTH), BF16),
        scratch_shapes=[pltpu.VMEM((2, nchunks, LANES, LANES), BF16)],
        compiler_params=pltpu.CompilerParams(dimension_semantics=("arbitrary",) * 2,
                                             vmem_limit_bytes=VMEM_LIMIT_BYTES),
        name="retention",
    )(log_gamma, rq, rk, rv, rg, gn_g)


def _post_kernel(o_ref, r_ref, ga_ref, gb_ref, x_ref, wa_ref, wb_ref, wo_ref, gffn_ref, wr_hi_lo_ref, wr_hi_ref,
                 x2_out, xn_out, aff_out, afft_out):
    a = _dot(o_ref[...], wa_ref[...])
    r = _dot(r_ref[...], wb_ref[...])
    mixed = _sigmoid(ga_ref[...].astype(F32)) * a + _sigmoid(gb_ref[...].astype(F32)) * r
    x2 = x_ref[...] + _dot(mixed.astype(BF16), wo_ref[...])
    x2_out[...] = x2
    xn = _rms(x2, gffn_ref[...])
    xn_hi = xn.astype(BF16)
    xn_out[...] = xn_hi
    xn_lo = (xn - xn_hi.astype(F32)).astype(BF16)
    t = _dot(xn_hi, wr_hi_lo_ref[...]) + _dot(xn_lo, wr_hi_ref[...])
    logits = t + pltpu.roll(t, LANES - N_EXPERTS, 1)
    lane = lax.broadcasted_iota(jnp.int32, logits.shape, 1)
    logits = jnp.where(lane < N_EXPERTS, logits, -jnp.inf)
    e = jnp.exp(logits - jnp.max(logits, axis=-1, keepdims=True))
    aff = e / jnp.sum(e, axis=-1, keepdims=True)
    aff_out[...] = aff
    for r in range(afft_out.shape[0]):
        afft_out[r] = aff[ROUTE_TILE * r:ROUTE_TILE * (r + 1), :].T[0:N_EXPERTS, :]


def _post_call(o, r, ga, gb, x2d, weights):
    n = x2d.shape[0]
    tm = POST_TILE
    rt = ROUTE_TILE
    row = lambda width: pl.BlockSpec((tm, width), lambda i: (i, 0))
    return pl.pallas_call(
        _post_kernel,
        grid=(n // tm,),
        in_specs=[row(MLA_HEADS * MLA_V_DIM), row(RET_V_WIDTH), row(D_MODEL), row(D_MODEL), row(D_MODEL)]
        + [_const_spec(w.shape) for w in weights],
        out_specs=[row(D_MODEL), row(D_MODEL), row(LANES),
                   pl.BlockSpec((tm // rt, N_EXPERTS, rt), lambda i: (i, 0, 0))],
        out_shape=[jax.ShapeDtypeStruct((n, D_MODEL), F32), jax.ShapeDtypeStruct((n, D_MODEL), BF16),
                   jax.ShapeDtypeStruct((n, LANES), F32), jax.ShapeDtypeStruct((n // rt, N_EXPERTS, rt), F32)],
        compiler_params=pltpu.CompilerParams(dimension_semantics=("arbitrary",), vmem_limit_bytes=VMEM_LIMIT_BYTES),
        name="post_mix",
    )(o, r, ga, gb, x2d, *weights)


def _ffn_kernel(xe_ref, wg_ref, wu_ref, wd_ref, ye_ref):
    x = xe_ref[...]
    g = _dot(x, wg_ref[...])
    u = _dot(x, wu_ref[...])
    hid = (g * _sigmoid(g) * u).astype(BF16)
    ye_ref[...] = _dot(hid, wd_ref[...]).astype(BF16)


def _ffn_call(xe, wg, wu, wd, cap):
    n_exp, _, d = xe.shape
    tm = min(FFN_ROW_TILE, cap)
    ff = wg.shape[-1]
    return pl.pallas_call(
        _ffn_kernel,
        grid=(n_exp, cap // tm),
        in_specs=[
            pl.BlockSpec((None, tm, d), lambda e, i: (e, i, 0)),
            pl.BlockSpec((None, d, ff), lambda e, i: (e, 0, 0)),
            pl.BlockSpec((None, d, ff), lambda e, i: (e, 0, 0)),
            pl.BlockSpec((None, ff, d), lambda e, i: (e, 0, 0)),
        ],
        out_specs=pl.BlockSpec((None, tm, d), lambda e, i: (e, i, 0)),
        out_shape=jax.ShapeDtypeStruct((n_exp, cap, d), BF16),
        compiler_params=pltpu.CompilerParams(dimension_semantics=("arbitrary",) * 2,
                                             vmem_limit_bytes=VMEM_LIMIT_BYTES),
        name="expert_ffn",
    )(xe, wg, wu, wd)


def _route_kernel(afft_ref, code_ref, starts_ref, *, cap):
    nt, ne, t = afft_ref.shape
    bits = jnp.maximum(pltpu.bitcast(afft_ref[...], jnp.int32), 0)
    idx = (lax.broadcasted_iota(jnp.int32, (nt, ne, t), 0) * t + lax.broadcasted_iota(jnp.int32, (nt, ne, t), 2))
    capf = jnp.float32(cap)

    def count(flag):
        per_lane = jnp.sum(flag.astype(F32), axis=0, keepdims=True)
        return jnp.sum(per_lane, axis=2, keepdims=True)

    def thr_body(i, thr):
        cand = thr | jnp.left_shift(jnp.int32(1), 30 - i)
        return jnp.where(count(bits >= cand) >= capf, cand, thr)

    thr = lax.fori_loop(0, 31, thr_body, jnp.zeros((1, ne, 1), jnp.int32))
    above = bits > thr
    tied = bits == thr
    need = capf - count(above)
    nbits = (nt * t - 1).bit_length()

    def idx_body(i, last):
        cand = last | jnp.left_shift(jnp.int32(1), nbits - 1 - i)
        return jnp.where(count(tied & (idx < cand)) <= need - 1.0, cand, last)

    last = lax.fori_loop(0, nbits, idx_body, jnp.zeros((1, ne, 1), jnp.int32))
    sel = (above | (tied & (idx <= last))).astype(F32)

    row = lax.broadcasted_iota(jnp.int32, (t, t), 0)
    col = lax.broadcasted_iota(jnp.int32, (t, t), 1)
    earlier = (row < col).astype(BF16)
    rank = _dot(sel.reshape(nt * ne, t).astype(BF16), earlier).reshape(nt, ne, t)
    code_ref[...] = jnp.where(sel > 0.0, rank, -1.0)
    per_tile = jnp.sum(sel, axis=2, keepdims=True)
    run = jnp.zeros((ne, 1), F32)
    for b in range(nt):
        starts_ref[b] = run.astype(jnp.int32)
        run = run + per_tile[b]


def _route_call(afft, cap):
    nt, ne, t = afft.shape
    full = lambda shape: pl.BlockSpec(shape, lambda: (0,) * len(shape))
    code, starts = pl.pallas_call(
        functools.partial(_route_kernel, cap=cap),
        in_specs=[full((nt, ne, t))],
        out_specs=[full((nt, ne, t)), full((nt, ne, 1))],
        out_shape=[jax.ShapeDtypeStruct((nt, ne, t), F32), jax.ShapeDtypeStruct((nt, ne, 1), jnp.int32)],
        compiler_params=pltpu.CompilerParams(vmem_limit_bytes=VMEM_LIMIT_BYTES),
        name="route",
    )(afft)
    return code, starts.reshape(nt, ne)


def _window_start(starts_ref, tile, e):
    return pl.multiple_of(jnp.bitwise_and(starts_ref[tile, e], -WINDOW_ALIGN), WINDOW_ALIGN)


def _tile_is_short(starts_ref, tile, ntiles, cap):
    nxt = jnp.minimum(tile + 1, ntiles - 1)
    longest = jnp.int32(0)
    for e in range(N_EXPERTS):
        end = jnp.where(tile + 1 < ntiles, starts_ref[nxt, e], cap)
        longest = jnp.maximum(longest, end - starts_ref[tile, e])
    return longest <= SHORT_SEG


def _dispatch_kernel(starts_ref, xn_ref, code_ref, xe_hbm, buf, sem, *, cap):
    b = pl.program_id(0)
    nb = pl.num_programs(0)
    slot = b % 2
    t = ROUTE_TILE
    prev = jnp.maximum(b - 1, 0)
    short_now = _tile_is_short(starts_ref, b, nb, cap)
    short_prev = _tile_is_short(starts_ref, prev, nb, cap)

    def win_copy(sl, e, tile, rows):
        return pltpu.make_async_copy(buf.at[sl, e, pl.ds(0, rows), :],
                                     xe_hbm.at[e, pl.ds(_window_start(starts_ref, tile, e), rows), :],
                                     sem.at[sl, e])

    @pl.when(b == 0)
    def _():
        buf[1] = jnp.zeros(buf.shape[1:], BF16)
        fills = [pltpu.make_async_copy(buf.at[1, e], xe_hbm.at[e, pl.ds(cap, WINDOW_ROWS), :], sem.at[1, e])
                 for e in range(N_EXPERTS)]
        for f in fills:
            f.start()
        for f in fills:
            f.wait()

    def fill_windows(rows, stacked):
        code = code_ref[...]
        rowf = lax.broadcasted_iota(jnp.int32, (rows, t), 0).astype(F32)
        xn = xn_ref[...]
        takes = []
        for e in range(N_EXPERTS):
            pos = code[e:e + 1, :] + (starts_ref[b, e] - _window_start(starts_ref, b, e)).astype(F32)
            takes.append(jnp.logical_and(code[e:e + 1, :] >= 0.0, pos == rowf).astype(BF16))
        if stacked:
            wins_all = _dot(jnp.concatenate(takes, axis=0), xn)
            wins = [wins_all[rows * e:rows * (e + 1), :] for e in range(N_EXPERTS)]
        else:
            wins = [_dot(take, xn) for take in takes]
        for e in range(N_EXPERTS):
            back = pl.multiple_of(_window_start(starts_ref, b, e) - _window_start(starts_ref, prev, e), WINDOW_ALIGN)
            carry = buf[1 - slot, e, pl.ds(back, WINDOW_ALIGN), :].astype(F32)
            buf[slot, e, 0:WINDOW_ALIGN, :] = (wins[e][0:WINDOW_ALIGN, :] + carry).astype(BF16)
            buf[slot, e, WINDOW_ALIGN:rows, :] = wins[e][WINDOW_ALIGN:rows, :].astype(BF16)

    def for_windows(cond, rows, action):
        @pl.when(cond)
        def _():
            for e in range(N_EXPERTS):
                action(e, rows)

    long_now = jnp.logical_not(short_now)
    pl.when(short_now)(lambda: fill_windows(SHORT_ROWS, True))
    pl.when(long_now)(lambda: fill_windows(WINDOW_ROWS, False))
    for_windows(jnp.logical_and(b > 0, short_prev), SHORT_ROWS, lambda e, r: win_copy(1 - slot, e, b - 1, r).wait())
    for_windows(jnp.logical_and(b > 0, jnp.logical_not(short_prev)), WINDOW_ROWS,
                lambda e, r: win_copy(1 - slot, e, b - 1, r).wait())
    for_windows(short_now, SHORT_ROWS, lambda e, r: win_copy(slot, e, b, r).start())
    for_windows(long_now, WINDOW_ROWS, lambda e, r: win_copy(slot, e, b, r).start())
    last = b == nb - 1
    for_windows(jnp.logical_and(last, short_now), SHORT_ROWS, lambda e, r: win_copy(slot, e, b, r).wait())
    for_windows(jnp.logical_and(last, long_now), WINDOW_ROWS, lambda e, r: win_copy(slot, e, b, r).wait())


def _dispatch_call(starts, code, xn, cap):
    nt, ne, t = code.shape
    d = xn.shape[1]
    grid_spec = pltpu.PrefetchScalarGridSpec(
        num_scalar_prefetch=1,
        grid=(nt,),
        in_specs=[pl.BlockSpec((t, d), lambda b, s: (b, 0)),
                  pl.BlockSpec((None, ne, t), lambda b, s: (b, 0, 0))],
        out_specs=pl.BlockSpec(memory_space=pl.ANY),
        scratch_shapes=[pltpu.VMEM((2, ne, WINDOW_ROWS, d), BF16), pltpu.SemaphoreType.DMA((2, ne))],
    )
    return pl.pallas_call(
        functools.partial(_dispatch_kernel, cap=cap),
        grid_spec=grid_spec,
        out_shape=jax.ShapeDtypeStruct((ne, cap + WINDOW_ROWS, d), BF16),
        compiler_params=pltpu.CompilerParams(dimension_semantics=("arbitrary",), vmem_limit_bytes=VMEM_LIMIT_BYTES),
        name="dispatch",
    )(starts, xn, code)


def _combine_kernel(starts_ref, x2_ref, aff_ref, code_ref, gfin_ref, ye_hbm, o_ref, buf, sem, *, cap):
    b = pl.program_id(0)
    nb = pl.num_programs(0)
    slot = b % 2
    t = ROUTE_TILE
    nxt = jnp.minimum(b + 1, nb - 1)
    short_now = _tile_is_short(starts_ref, b, nb, cap)
    short_next = _tile_is_short(starts_ref, nxt, nb, cap)
    long_now = jnp.logical_not(short_now)

    def win_start(tile, e, rows):
        aligned = jnp.bitwise_and(starts_ref[tile, e], -WINDOW_ALIGN)
        return pl.multiple_of(jnp.minimum(aligned, cap - rows), WINDOW_ALIGN)

    def win_copy(sl, e, tile, rows):
        return pltpu.make_async_copy(ye_hbm.at[e, pl.ds(win_start(tile, e, rows), rows), :],
                                     buf.at[sl, e, pl.ds(0, rows), :], sem.at[sl, e])

    def for_windows(cond, rows, action):
        @pl.when(cond)
        def _():
            for e in range(N_EXPERTS):
                action(e, rows)

    first = b == 0
    for_windows(jnp.logical_and(first, short_now), SHORT_ROWS, lambda e, r: win_copy(0, e, 0, r).start())
    for_windows(jnp.logical_and(first, long_now), WINDOW_ROWS, lambda e, r: win_copy(0, e, 0, r).start())
    more = b + 1 < nb
    for_windows(jnp.logical_and(more, short_next), SHORT_ROWS, lambda e, r: win_copy(1 - slot, e, b + 1, r).start())
    for_windows(jnp.logical_and(more, jnp.logical_not(short_next)), WINDOW_ROWS,
                lambda e, r: win_copy(1 - slot, e, b + 1, r).start())

    def ranks_and_gates():
        row = lax.broadcasted_iota(jnp.int32, (t, t), 0)
        col = lax.broadcasted_iota(jnp.int32, (t, t), 1)
        ranks = _dot_nt((row == col).astype(BF16), code_ref[...].astype(BF16))
        return ranks, jnp.where(ranks >= 0.0, aff_ref[:, 0:N_EXPERTS], 0.0)

    def long_path():
        ranks, gates = ranks_and_gates()
        colf = lax.broadcasted_iota(jnp.int32, (t, t), 1).astype(F32)
        for e in range(N_EXPERTS):
            win_copy(slot, e, b, WINDOW_ROWS).wait()
        acc = jnp.zeros((t, D_MODEL), F32)
        tail = jnp.zeros((t, t), F32)
        for e in range(N_EXPERTS):
            pos = ranks[:, e:e + 1] + (starts_ref[b, e] - win_start(b, e, WINDOW_ROWS)).astype(F32)
            g = gates[:, e:e + 1]
            acc = acc + _dot(jnp.where(pos == colf, g, 0.0).astype(BF16), buf[slot, e, 0:t, :])
            in_tail = jnp.logical_and(pos >= float(t), pos - float(t - WINDOW_ALIGN * e) == colf)
            tail = tail + jnp.where(in_tail, g, 0.0)
        tail_rows = jnp.concatenate([buf[slot, e, t:WINDOW_ROWS, :] for e in range(N_EXPERTS)], axis=0)
        acc = acc + _dot(tail.astype(BF16), tail_rows)
        o_ref[...] = _rms(x2_ref[...] + acc, gfin_ref[...])

    def short_path():
        ranks, gates = ranks_and_gates()
        width = N_EXPERTS * SHORT_ROWS
        lane = lax.broadcasted_iota(jnp.int32, (1, N_EXPERTS), 1)
        shift = jnp.zeros((1, N_EXPERTS), F32)
        for e in range(N_EXPERTS):
            shift = jnp.where(lane == e, (starts_ref[b, e] - win_start(b, e, SHORT_ROWS)).astype(F32), shift)
        assert SHORT_ROWS & (SHORT_ROWS - 1) == 0
        group = jnp.right_shift(lax.broadcasted_iota(jnp.int32, (N_EXPERTS, width), 1), SHORT_ROWS.bit_length() - 1)
        spread = (group == lax.broadcasted_iota(jnp.int32, (N_EXPERTS, width), 0)).astype(BF16)
        pos = _dot((ranks + shift).astype(BF16), spread)
        g = _dot(gates.astype(BF16), spread)
        colf = jnp.bitwise_and(lax.broadcasted_iota(jnp.int32, (t, width), 1), SHORT_ROWS - 1).astype(F32)
        for e in range(N_EXPERTS):
            win_copy(slot, e, b, SHORT_ROWS).wait()
        rows = jnp.concatenate([buf[slot, e, 0:SHORT_ROWS, :] for e in range(N_EXPERTS)], axis=0)
        acc = _dot(jnp.where(pos == colf, g, 0.0).astype(BF16), rows)
        o_ref[...] = _rms(x2_ref[...] + acc, gfin_ref[...])

    pl.when(short_now)(short_path)
    pl.when(long_now)(long_path)


def _combine_call(starts, x2, aff, code, final_g, ye, cap):
    nt, ne, t = code.shape
    n, d = x2.shape
    assert WINDOW_ALIGN * ne == t and cap >= WINDOW_ROWS and cap % WINDOW_ALIGN == 0
    grid_spec = pltpu.PrefetchScalarGridSpec(
        num_scalar_prefetch=1,
        grid=(nt,),
        in_specs=[pl.BlockSpec((t, d), lambda b, s: (b, 0)),
                  pl.BlockSpec((t, LANES), lambda b, s: (b, 0)),
                  pl.BlockSpec((None, ne, t), lambda b, s: (b, 0, 0)),
                  pl.BlockSpec((1, d), lambda b, s: (0, 0)),
                  pl.BlockSpec(memory_space=pl.ANY)],
        out_specs=pl.BlockSpec((t, d), lambda b, s: (b, 0)),
        scratch_shapes=[pltpu.VMEM((2, ne, WINDOW_ROWS, d), BF16), pltpu.SemaphoreType.DMA((2, ne))],
    )
    return pl.pallas_call(
        functools.partial(_combine_kernel, cap=cap),
        grid_spec=grid_spec,
        out_shape=jax.ShapeDtypeStruct((n, d), F32),
        compiler_params=pltpu.CompilerParams(dimension_semantics=("arbitrary",), vmem_limit_bytes=VMEM_LIMIT_BYTES),
        name="combine",
    )(starts, x2, aff, code, final_g, ye)


def _rope_angles(seq, dim):
    inv = 1.0 / (ROPE_THETA ** (jnp.arange(0, dim, 2, dtype=F32) / dim))
    ang = jnp.arange(seq, dtype=F32)[:, None] * inv[None, :]
    return jnp.cos(ang), jnp.sin(ang)


def _mla_tables(seq, scale):
    c, s = _rope_angles(seq, MLA_ROPE_DIM)
    ones = jnp.ones((seq, MLA_NOPE_DIM), F32)
    zeros_n = jnp.zeros((seq, MLA_NOPE_DIM), F32)
    zeros_p = jnp.zeros((seq, HEAD_PAD - MLA_QK_DIM), F32)
    cos = jnp.concatenate([ones, c, c, zeros_p], axis=1) * scale
    sin = jnp.concatenate([zeros_n, -s, s, zeros_p], axis=1) * scale
    return cos, sin


def _ret_tables(seq, scale):
    c, s = _rope_angles(seq, RET_QK_DIM)
    cos = jnp.concatenate([c, c, c, c], axis=1) * scale
    sin = jnp.concatenate([-s, s, -s, s], axis=1) * scale
    return cos, sin


def _prepare_weights(norm_mix_g, w_in, q_norm_g, w_uq, kv_norm_g, w_ukv, ret_gn_g, w_branch_a, w_branch_b, w_out,
                     norm_ffn_g, w_router, norm_final_g):
    offs, acc = [], 0
    for width in IN_SPLITS:
        offs.append((acc, acc + width))
        acc += width
    w_cq, w_ckv, w_kr, w_rq, w_rk, w_rv, w_rg, w_ga, w_gb = [w_in[:, a:b].astype(BF16) for a, b in offs]
    w_kr = jnp.pad(w_kr, ((0, 0), (0, LANES - MLA_ROPE_DIM)))

    uq = w_uq.reshape(Q_LORA_RANK, MLA_HEADS, MLA_QK_DIM)
    wq_pad = jnp.pad(uq, ((0, 0), (0, 0), (0, HEAD_PAD - MLA_QK_DIM))).reshape(Q_LORA_RANK, MLA_PAD_WIDTH)
    ukv = w_ukv.reshape(KV_LORA_RANK, MLA_HEADS, MLA_NOPE_DIM + MLA_V_DIM)
    wk_pad = jnp.pad(ukv[:, :, :MLA_NOPE_DIM], ((0, 0), (0, 0), (0, HEAD_PAD - MLA_NOPE_DIM)))
    wk_pad = wk_pad.reshape(KV_LORA_RANK, MLA_PAD_WIDTH)
    wv = jnp.pad(ukv[:, :, MLA_NOPE_DIM:], ((0, 0), (0, 0), (0, HEAD_PAD - MLA_V_DIM)))
    wv = wv.reshape(KV_LORA_RANK, MLA_PAD_WIDTH).T
    src = jnp.arange(LANES)[:, None]
    dst = jnp.arange(MLA_PAD_WIDTH)[None, :]
    place = ((dst % HEAD_PAD) - MLA_NOPE_DIM == src) & (src < MLA_ROPE_DIM)

    pre_w = (norm_mix_g.reshape(1, -1), w_cq, w_ckv, w_kr, w_rq, w_rk, w_rv, w_rg, w_ga, w_gb,
             q_norm_g.reshape(1, -1), wq_pad.astype(BF16), kv_norm_g.reshape(1, -1), wk_pad.astype(BF16),
             place.astype(BF16), wv.astype(BF16))

    wr_hi = w_router.astype(BF16)
    wr_lo = (w_router - wr_hi.astype(F32)).astype(BF16)
    pad_to = lambda w: jnp.pad(w, ((0, 0), (0, LANES - w.shape[1])))
    post_w = (w_branch_a.astype(BF16), w_branch_b.astype(BF16), w_out.astype(BF16), norm_ffn_g.reshape(1, -1),
              pad_to(jnp.concatenate([wr_hi, wr_lo], axis=1)), pad_to(wr_hi))
    return pre_w, post_w, ret_gn_g.reshape(1, -1), norm_final_g.reshape(1, -1)


def _encode(x, pre_w, post_w, log_gamma, gn_g, final_g, wg, wu, wd):
    batch, seq, d = x.shape
    n = batch * seq
    x2d = x.reshape(n, d)
    tables = (*_mla_tables(seq, MLA_QK_DIM ** -0.5 * LOG2_E), *_mla_tables(seq, 1.0),
              *_ret_tables(seq, 1.0), *_ret_tables(seq, RET_QK_DIM ** -0.5))
    qcat, kcat, vt, rq, rk, rv, rg, ga, gb = _pre_call(x2d, seq, pre_w, tables)
    o = _attn_call(qcat, kcat, vt, batch, seq)
    r = _ret_call(log_gamma, rq, rk, rv, rg, gn_g, batch, seq)
    x2, xn, aff, afft = _post_call(o, r, ga, gb, x2d, post_w)

    cap = CAPACITY_FACTOR * n // N_EXPERTS
    code, starts = _route_call(afft, cap)
    xe = _dispatch_call(starts, code, xn, cap)
    ye = _ffn_call(xe, wg, wu, wd, cap)
    return _combine_call(starts, x2, aff, code, final_g, ye, cap).reshape(batch, seq, d)


def kernel(x_prompt, x_sample, norm_mix_g, w_in, q_norm_g, w_uq, kv_norm_g, w_ukv, ret_decay_fwd, ret_decay_bwd,
           ret_gn_g, w_branch_a, w_branch_b, w_out, norm_ffn_g, w_router, w_exp_gate, w_exp_up, w_exp_down,
           norm_final_g):
    assert norm_mix_g.shape[0] == 1, "single-layer trunk"
    pre_w, post_w, gn_g, final_g = _prepare_weights(
        norm_mix_g[0], w_in[0], q_norm_g[0], w_uq[0], kv_norm_g[0], w_ukv[0], ret_gn_g[0], w_branch_a[0],
        w_branch_b[0], w_out[0], norm_ffn_g[0], w_router[0], norm_final_g)
    log_gamma = jnp.stack([jax.nn.log_sigmoid(ret_decay_fwd[0].astype(F32)),
                           jax.nn.log_sigmoid(ret_decay_bwd[0].astype(F32))])
    wg = w_exp_gate[0].astype(BF16)
    wu = w_exp_up[0].astype(BF16)
    wd = w_exp_down[0].astype(BF16)
    enc = functools.partial(_encode, pre_w=pre_w, post_w=post_w, log_gamma=log_gamma, gn_g=gn_g, final_g=final_g,
                            wg=wg, wu=wu, wd=wd)
    return enc(x_prompt), enc(x_sample)
```

```python
import functools

import jax
import jax.numpy as jnp
from jax import lax
from jax.experimental import pallas as pl
from jax.experimental.pallas import tpu as pltpu

D_MODEL = 1024
MLA_HEADS = 8
MLA_NOPE_DIM = 64
MLA_ROPE_DIM = 32
MLA_V_DIM = 64
MLA_QK_DIM = MLA_NOPE_DIM + MLA_ROPE_DIM
Q_LORA_RANK = 384
KV_LORA_RANK = 256
RET_HEADS = 8
RET_QK_DIM = 64
RET_V_DIM = 128
RET_QK_WIDTH = RET_HEADS * RET_QK_DIM
RET_V_WIDTH = RET_HEADS * RET_V_DIM
N_EXPERTS = 16
EXPERT_FF = 2816
CAPACITY_FACTOR = 2
ROPE_THETA = 10000.0
EPS = 1e-6
GN_EPS = 1e-5
IN_SPLITS = (Q_LORA_RANK, KV_LORA_RANK, MLA_ROPE_DIM, RET_QK_WIDTH, RET_QK_WIDTH, RET_V_WIDTH, RET_V_WIDTH,
             D_MODEL, D_MODEL)

LANES = 128
SUBLANES = 8
HEAD_PAD = LANES
MLA_PAD_WIDTH = MLA_HEADS * HEAD_PAD
VMEM_LIMIT_BYTES = 56 * 1024 * 1024

TOKEN_TILE = 256
PRE_TILE = 512
POST_TILE = 256
VT_KEYS = PRE_TILE
LOG2_E = 1.4426950408889634
ROUTE_TILE = TOKEN_TILE
WINDOW_ALIGN = 16
WINDOW_ROWS = ROUTE_TILE + WINDOW_ALIGN
SHORT_SEG = 48
SHORT_ROWS = SHORT_SEG + WINDOW_ALIGN
ATTN_Q_TILE = 1024
ATTN_K_TILE = 512
ATTN_UNROLL = 3
ATTN_JUMP_LIMIT = 60.0
RET_CHUNK = 256
RET_BWD_UNROLL = 8
RET_FWD_UNROLL = 8
FFN_ROW_TILE = 256

F32 = jnp.float32
BF16 = jnp.bfloat16


def _dot(a, b):
    return jnp.dot(a, b, preferred_element_type=F32)


def _dot_nt(a, b):
    return lax.dot_general(a, b, (((1,), (1,)), ((), ())), preferred_element_type=F32)


def _dot_tn(a, b):
    return lax.dot_general(a, b, (((0,), (0,)), ((), ())), preferred_element_type=F32)


def _rms(x, g):
    return x * lax.rsqrt(jnp.mean(x * x, axis=-1, keepdims=True) + EPS) * g


def _sigmoid(x):
    return 1.0 / (1.0 + jnp.exp(-x))


def _rope_lanes(blk, cos, sin, half):
    lane = lax.broadcasted_iota(jnp.int32, blk.shape, 1)
    upper = (lane % (2 * half)) >= half
    partner = jnp.where(upper, pltpu.roll(blk, half, 1), pltpu.roll(blk, LANES - half, 1))
    return blk * cos + partner * sin


def _const_spec(shape):
    nd = len(shape)
    return pl.BlockSpec(shape, lambda *_: (0,) * nd, pipeline_mode=pl.Buffered(1))


def _pre_kernel(x_ref, gmix_ref, wcq_ref, wckv_ref, wkr_ref, wrq_ref, wrk_ref, wrv_ref, wrg_ref, wga_ref, wgb_ref,
                gq_ref, wq_ref, gkv_ref, wk_ref, place_ref, wv_ref,
                cosq_ref, sinq_ref, cosk_ref, sink_ref, cosrq_ref, sinrq_ref, cosrk_ref, sinrk_ref,
                q_out, k_out, vt_out, rq_out, rk_out, rv_out, rg_out, ga_out, gb_out):
    h = _rms(x_ref[...], gmix_ref[...]).astype(BF16)

    qn = _rms(_dot(h, wcq_ref[...]), gq_ref[...]).astype(BF16)
    q = _dot(qn, wq_ref[...])
    cosq, sinq = cosq_ref[...], sinq_ref[...]
    for j in range(MLA_HEADS):
        sl = slice(HEAD_PAD * j, HEAD_PAD * (j + 1))
        q_out[:, sl] = _rope_lanes(q[:, sl], cosq, sinq, MLA_ROPE_DIM // 2).astype(BF16)

    kvn = _rms(_dot(h, wckv_ref[...]), gkv_ref[...]).astype(BF16)
    kr = _dot(h, wkr_ref[...]).astype(BF16)
    k = _dot(kvn, wk_ref[...]) + _dot(kr, place_ref[...])
    cosk, sink = cosk_ref[...], sink_ref[...]
    for j in range(MLA_HEADS):
        sl = slice(HEAD_PAD * j, HEAD_PAD * (j + 1))
        k_out[:, sl] = _rope_lanes(k[:, sl], cosk, sink, MLA_ROPE_DIM // 2).astype(BF16)
    vt = _dot_nt(wv_ref[...], kvn)
    vrow = lax.broadcasted_iota(jnp.int32, vt.shape, 0)
    vt_out[...] = jnp.where(vrow % HEAD_PAD == MLA_V_DIM, 1.0, vt).astype(BF16)

    rq = _dot(h, wrq_ref[...])
    rk = _dot(h, wrk_ref[...])
    cosrq, sinrq, cosrk, sinrk = cosrq_ref[...], sinrq_ref[...], cosrk_ref[...], sinrk_ref[...]
    for j in range(RET_QK_WIDTH // LANES):
        sl = slice(LANES * j, LANES * (j + 1))
        rq_out[:, sl] = _rope_lanes(rq[:, sl], cosrq, sinrq, RET_QK_DIM // 2).astype(BF16)
        rk_out[:, sl] = _rope_lanes(rk[:, sl], cosrk, sinrk, RET_QK_DIM // 2).astype(BF16)

    rv_out[...] = _dot(h, wrv_ref[...]).astype(BF16)
    rg_out[...] = _dot(h, wrg_ref[...]).astype(BF16)
    ga_out[...] = _dot(h, wga_ref[...]).astype(BF16)
    gb_out[...] = _dot(h, wgb_ref[...]).astype(BF16)


def _pre_call(x2d, seq, weights, tables):
    n = x2d.shape[0]
    tm = PRE_TILE
    tiles_per_seq = seq // tm
    row = lambda width: pl.BlockSpec((tm, width), lambda i: (i, 0))
    tab = pl.BlockSpec((tm, LANES), lambda i: (i % tiles_per_seq, 0))
    out_widths = (MLA_PAD_WIDTH, MLA_PAD_WIDTH, None, RET_QK_WIDTH, RET_QK_WIDTH,
                  RET_V_WIDTH, RET_V_WIDTH, D_MODEL, D_MODEL)
    vt_spec = pl.BlockSpec((None, None, MLA_PAD_WIDTH, tm), lambda i: (i // tiles_per_seq, i % tiles_per_seq, 0, 0))
    vt_shape = jax.ShapeDtypeStruct((n // seq, tiles_per_seq, MLA_PAD_WIDTH, tm), BF16)
    return pl.pallas_call(
        _pre_kernel,
        grid=(n // tm,),
        in_specs=[row(D_MODEL)] + [_const_spec(w.shape) for w in weights] + [tab] * len(tables),
        out_specs=[vt_spec if w is None else row(w) for w in out_widths],
        out_shape=[vt_shape if w is None else jax.ShapeDtypeStruct((n, w), BF16) for w in out_widths],
        compiler_params=pltpu.CompilerParams(dimension_semantics=("arbitrary",), vmem_limit_bytes=VMEM_LIMIT_BYTES),
        name="pre_proj",
    )(x2d, *weights, *tables)


def _attn_kernel(q_ref, k_ref, vt_ref, o_ref, st_scr, pt_scr, mc_scr, m_scr, acc_scr, jump_scr, *, tk, nk):
    sub = tk // VT_KEYS
    heads = [slice(HEAD_PAD * hh, HEAD_PAD * (hh + 1)) for hh in range(2)]

    def qk(kb, sl):
        off = pl.multiple_of(kb * tk, tk)
        return _dot_nt(k_ref[pl.ds(off, tk), sl], q_ref[:, sl])

    def value_product(kb, sl, pt):
        pv = _dot(vt_ref[kb * sub, sl, :], pt[0:VT_KEYS, :])
        for j in range(1, sub):
            pv = pv + _dot(vt_ref[kb * sub + j, sl, :], pt[VT_KEYS * j:VT_KEYS * (j + 1), :])
        return pv

    def pipeline(first, scores, accumulate):
        first()
        if nk > 1:
            assert nk % 2 == 0

            def body(j, carry):
                kb = 2 * j
                scores(kb + 1, 1)
                accumulate(kb, 0)
                scores(kb + 2, 0)
                accumulate(kb + 1, 1)
                return carry

            trips = nk // 2 - 1
            lax.fori_loop(0, trips, body, 0, unroll=ATTN_UNROLL if trips >= 2 * ATTN_UNROLL else 1)
            scores(nk - 1, 1)
            accumulate(nk - 2, 0)
            accumulate(nk - 1, 1)
        else:
            accumulate(0, 0)
        outs = []
        for hh in range(2):
            acc = acc_scr[hh]
            outs.append(acc[0:MLA_V_DIM, :] / acc[MLA_V_DIM:MLA_V_DIM + 1, :])
        o_ref[...] = jnp.concatenate(outs, axis=0).T.astype(BF16)

    def fast_first():
        acc_scr[...] = jnp.zeros(acc_scr.shape, F32)
        jump_scr[...] = jnp.zeros(jump_scr.shape, F32)
        for hh, sl in enumerate(heads):
            st = qk(0, sl)
            ref = jnp.max(st[0:SUBLANES, :], axis=0, keepdims=True)
            pt_scr[0, hh] = jnp.exp2(st - ref).astype(BF16)
            mc_scr[0, hh] = jnp.max(st, axis=0, keepdims=True)
            m_scr[hh] = ref

    def fast_scores(kb, buf):
        for hh, sl in enumerate(heads):
            st = qk(kb, sl)
            ref = jnp.maximum(m_scr[hh], mc_scr[1 - buf, hh])
            pt_scr[buf, hh] = jnp.exp2(st - ref).astype(BF16)
            mc_scr[buf, hh] = jnp.max(st, axis=0, keepdims=True)

    def fast_accumulate(kb, buf):
        for hh, sl in enumerate(heads):
            m_old = m_scr[hh]
            bmax = mc_scr[buf, hh]
            m_new = jnp.maximum(m_old, bmax)
            jump_scr[hh] = jnp.maximum(jump_scr[hh], bmax - m_old)
            acc_scr[hh] = (acc_scr[hh] + value_product(kb, sl, pt_scr[buf, hh])) * jnp.exp2(m_old - m_new)
            m_scr[hh] = m_new

    def safe_first():
        m_scr[...] = jnp.full(m_scr.shape, -jnp.inf, F32)
        acc_scr[...] = jnp.zeros(acc_scr.shape, F32)
        safe_scores(0, 0)

    def safe_scores(kb, buf):
        for hh, sl in enumerate(heads):
            st = qk(kb, sl)
            st_scr[buf, hh] = st
            mc_scr[buf, hh] = jnp.max(st, axis=0, keepdims=True)

    def safe_accumulate(kb, buf):
        for hh, sl in enumerate(heads):
            m_old = m_scr[hh]
            m_new = jnp.maximum(m_old, mc_scr[buf, hh])
            pt = jnp.exp2(st_scr[buf, hh] - m_new).astype(BF16)
            acc_scr[hh] = jnp.exp2(m_old - m_new) * acc_scr[hh] + value_product(kb, sl, pt)
            m_scr[hh] = m_new

    pipeline(fast_first, fast_scores, fast_accumulate)
    jump = jnp.max(jnp.maximum(jump_scr[0], jump_scr[1]))
    pl.when(jump > ATTN_JUMP_LIMIT)(lambda: pipeline(safe_first, safe_scores, safe_accumulate))


def _attn_call(qcat, kcat, vt, batch, seq):
    n = qcat.shape[0]
    tq = min(ATTN_Q_TILE, seq)
    tk = min(ATTN_K_TILE, seq)
    nq = seq // tq
    pairs = MLA_HEADS // 2
    return pl.pallas_call(
        functools.partial(_attn_kernel, tk=tk, nk=seq // tk),
        grid=(batch, pairs, nq),
        in_specs=[
            pl.BlockSpec((tq, 2 * HEAD_PAD), lambda b, p, i: (b * nq + i, p)),
            pl.BlockSpec((seq, 2 * HEAD_PAD), lambda b, p, i: (b, p)),
            pl.BlockSpec((None, seq // VT_KEYS, 2 * HEAD_PAD, VT_KEYS), lambda b, p, i: (b, 0, p, 0)),
        ],
        out_specs=pl.BlockSpec((tq, 2 * MLA_V_DIM), lambda b, p, i: (b * nq + i, p)),
        out_shape=jax.ShapeDtypeStruct((n, MLA_HEADS * MLA_V_DIM), BF16),
        scratch_shapes=[pltpu.VMEM((2, 2, tk, tq), F32), pltpu.VMEM((2, 2, tk, tq), BF16),
                        pltpu.VMEM((2, 2, 1, tq), F32), pltpu.VMEM((2, 1, tq), F32),
                        pltpu.VMEM((2, HEAD_PAD, tq), F32), pltpu.VMEM((2, 1, tq), F32)],
        compiler_params=pltpu.CompilerParams(dimension_semantics=("arbitrary",) * 3,
                                             vmem_limit_bytes=VMEM_LIMIT_BYTES),
        name="mla_attn",
    )(qcat, kcat, vt)


def _ret_kernel(lg_ref, q_ref, k_ref, v_ref, rg_ref, gn_ref, o_ref, rb_scr, *, chunk, nchunks):
    c_len = chunk
    pair = pl.program_id(1)
    lane = lax.broadcasted_iota(jnp.int32, (c_len, LANES), 1)
    pos = lax.broadcasted_iota(jnp.int32, (c_len, LANES), 0).astype(F32)
    ii = lax.broadcasted_iota(jnp.int32, (c_len, c_len), 0)
    jj = lax.broadcasted_iota(jnp.int32, (c_len, c_len), 1)
    diff = (ii - jj).astype(F32)
    sq = (LANES, LANES)
    two = range(2)
    lgf = [lg_ref[0, 2 * pair + hh] for hh in two]
    lgb = [lg_ref[1, 2 * pair + hh] for hh in two]
    mask = [(lane // RET_QK_DIM) == hh for hh in two]
    vsl = [slice(RET_V_DIM * hh, RET_V_DIM * (hh + 1)) for hh in two]
    zeta_f = [jnp.exp((c_len - 1.0 - pos) * lgf[hh]) for hh in two]
    xi_f = [jnp.exp((pos + 1.0) * lgf[hh]) for hh in two]
    zeta_b = [jnp.exp(pos * lgb[hh]) for hh in two]
    xi_b = [jnp.exp((c_len - pos) * lgb[hh]) for hh in two]
    gchunk_f = [jnp.exp(jnp.full(sq, c_len, F32) * lgf[hh]) for hh in two]
    gchunk_b = [jnp.exp(jnp.full(sq, c_len, F32) * lgb[hh]) for hh in two]
    decay = [jnp.where(diff >= 0.0, jnp.exp(jnp.maximum(diff, 0.0) * lgf[hh]),
                       jnp.exp(jnp.maximum(-diff, 0.0) * lgb[hh])) for hh in two]
    gn = gn_ref[...]

    def chunk_slice(c):
        return pl.ds(pl.multiple_of(c * c_len, c_len), c_len)

    def bwd_body(t, states):
        c = nchunks - 1 - t
        rows = chunk_slice(c)
        k = k_ref[rows, :].astype(F32)
        new = []
        for hh in two:
            rb_scr[hh, c] = states[hh].astype(BF16)
            kz = (jnp.where(mask[hh], k, 0.0) * zeta_b[hh]).astype(BF16)
            new.append(gchunk_b[hh] * states[hh] + _dot_tn(kz, v_ref[rows, vsl[hh]]))
        return tuple(new)

    zero_states = (jnp.zeros(sq, F32), jnp.zeros(sq, F32))
    lax.fori_loop(0, nchunks, bwd_body, zero_states, unroll=min(RET_BWD_UNROLL, nchunks))

    def fwd_body(c, states):
        rows = chunk_slice(c)
        q = q_ref[rows, :].astype(F32)
        k16 = k_ref[rows, :]
        k = k16.astype(F32)
        qm = [jnp.where(mask[hh], q, 0.0) for hh in two]
        inner = [_dot_nt(qm[hh].astype(BF16), k16) for hh in two]
        ys = []
        for hh in two:
            lhs = jnp.concatenate([(inner[hh] * decay[hh]).astype(BF16), (qm[hh] * xi_f[hh]).astype(BF16),
                                   (qm[hh] * xi_b[hh]).astype(BF16)], axis=1)
            rhs = jnp.concatenate([v_ref[rows, vsl[hh]], states[hh].astype(BF16), rb_scr[hh, c]], axis=0)
            ys.append(_dot(lhs, rhs))
        new = []
        for hh in two:
            y = ys[hh]
            mu = jnp.mean(y, axis=-1, keepdims=True)
            yc = y - mu
            var = jnp.mean(yc * yc, axis=-1, keepdims=True)
            yn = yc * lax.rsqrt(var + GN_EPS) * gn[:, vsl[hh]]
            rg = rg_ref[rows, vsl[hh]].astype(F32)
            o_ref[rows, vsl[hh]] = (rg * _sigmoid(rg) * yn).astype(BF16)
            kz = (jnp.where(mask[hh], k, 0.0) * zeta_f[hh]).astype(BF16)
            new.append(gchunk_f[hh] * states[hh] + _dot_tn(kz, v_ref[rows, vsl[hh]]))
        return tuple(new)

    lax.fori_loop(0, nchunks, fwd_body, zero_states, unroll=min(RET_FWD_UNROLL, nchunks))


def _ret_call(log_gamma, rq, rk, rv, rg, gn_g, batch, seq):
    n = rq.shape[0]
    chunk = min(RET_CHUNK, seq)
    nchunks = seq // chunk
    qk_blk = pl.BlockSpec((seq, LANES), lambda b, p: (b, p))
    v_blk = pl.BlockSpec((seq, 2 * RET_V_DIM), lambda b, p: (b, p))
    return pl.pallas_call(
        functools.partial(_ret_kernel, chunk=chunk, nchunks=nchunks),
        grid=(batch, RET_HEADS // 2),
        in_specs=[pl.BlockSpec(memory_space=pltpu.SMEM), qk_blk, qk_blk, v_blk, v_blk,
                  pl.BlockSpec((1, 2 * RET_V_DIM), lambda b, p: (0, p))],
        out_specs=v_blk,
        out_shape=jax.ShapeDtypeStruct((n, RET_V_WIDTH), BF16),
        scratch_shapes=[pltpu.VMEM((2, nchunks, LANES, LANES), BF16)],
        compiler_params=pltpu.CompilerParams(dimension_semantics=("arbitrary",) * 2,
                                             vmem_limit_bytes=VMEM_LIMIT_BYTES),
        name="retention",
    )(log_gamma, rq, rk, rv, rg, gn_g)


def _post_kernel(o_ref, r_ref, ga_ref, gb_ref, x_ref, wa_ref, wb_ref, wo_ref, gffn_ref, wr_hi_lo_ref, wr_hi_ref,
                 x2_out, xn_out, aff_out, afft_out):
    a = _dot(o_ref[...], wa_ref[...])
    r = _dot(r_ref[...], wb_ref[...])
    mixed = _sigmoid(ga_ref[...].astype(F32)) * a + _sigmoid(gb_ref[...].astype(F32)) * r
    x2 = x_ref[...] + _dot(mixed.astype(BF16), wo_ref[...])
    x2_out[...] = x2
    xn = _rms(x2, gffn_ref[...])
    xn_hi = xn.astype(BF16)
    xn_out[...] = xn_hi
    xn_lo = (xn - xn_hi.astype(F32)).astype(BF16)
    t = _dot(xn_hi, wr_hi_lo_ref[...]) + _dot(xn_lo, wr_hi_ref[...])
    logits = t + pltpu.roll(t, LANES - N_EXPERTS, 1)
    lane = lax.broadcasted_iota(jnp.int32, logits.shape, 1)
    logits = jnp.where(lane < N_EXPERTS, logits, -jnp.inf)
    e = jnp.exp(logits - jnp.max(logits, axis=-1, keepdims=True))
    aff = e / jnp.sum(e, axis=-1, keepdims=True)
    aff_out[...] = aff
    for r in range(afft_out.shape[0]):
        afft_out[r] = aff[ROUTE_TILE * r:ROUTE_TILE * (r + 1), :].T[0:N_EXPERTS, :]


def _post_call(o, r, ga, gb, x2d, weights):
    n = x2d.shape[0]
    tm = POST_TILE
    rt = ROUTE_TILE
    row = lambda width: pl.BlockSpec((tm, width), lambda i: (i, 0))
    return pl.pallas_call(
        _post_kernel,
        grid=(n // tm,),
        in_specs=[row(MLA_HEADS * MLA_V_DIM), row(RET_V_WIDTH), row(D_MODEL), row(D_MODEL), row(D_MODEL)]
        + [_const_spec(w.shape) for w in weights],
        out_specs=[row(D_MODEL), row(D_MODEL), row(LANES),
                   pl.BlockSpec((tm // rt, N_EXPERTS, rt), lambda i: (i, 0, 0))],
        out_shape=[jax.ShapeDtypeStruct((n, D_MODEL), F32), jax.ShapeDtypeStruct((n, D_MODEL), BF16),
                   jax.ShapeDtypeStruct((n, LANES), F32), jax.ShapeDtypeStruct((n // rt, N_EXPERTS, rt), F32)],
        compiler_params=pltpu.CompilerParams(dimension_semantics=("arbitrary",), vmem_limit_bytes=VMEM_LIMIT_BYTES),
        name="post_mix",
    )(o, r, ga, gb, x2d, *weights)


def _ffn_kernel(xe_ref, wg_ref, wu_ref, wd_ref, ye_ref):
    x = xe_ref[...]
    g = _dot(x, wg_ref[...])
    u = _dot(x, wu_ref[...])
    hid = (g * _sigmoid(g) * u).astype(BF16)
    ye_ref[...] = _dot(hid, wd_ref[...]).astype(BF16)


def _ffn_call(xe, wg, wu, wd, cap):
    n_exp, _, d = xe.shape
    tm = min(FFN_ROW_TILE, cap)
    ff = wg.shape[-1]
    return pl.pallas_call(
        _ffn_kernel,
        grid=(n_exp, cap // tm),
        in_specs=[
            pl.BlockSpec((None, tm, d), lambda e, i: (e, i, 0)),
            pl.BlockSpec((None, d, ff), lambda e, i: (e, 0, 0)),
            pl.BlockSpec((None, d, ff), lambda e, i: (e, 0, 0)),
            pl.BlockSpec((None, ff, d), lambda e, i: (e, 0, 0)),
        ],
        out_specs=pl.BlockSpec((None, tm, d), lambda e, i: (e, i, 0)),
        out_shape=jax.ShapeDtypeStruct((n_exp, cap, d), BF16),
        compiler_params=pltpu.CompilerParams(dimension_semantics=("arbitrary",) * 2,
                                             vmem_limit_bytes=VMEM_LIMIT_BYTES),
        name="expert_ffn",
    )(xe, wg, wu, wd)


def _route_kernel(afft_ref, code_ref, starts_ref, *, cap):
    nt, ne, t = afft_ref.shape
    bits = jnp.maximum(pltpu.bitcast(afft_ref[...], jnp.int32), 0)
    idx = (lax.broadcasted_iota(jnp.int32, (nt, ne, t), 0) * t + lax.broadcasted_iota(jnp.int32, (nt, ne, t), 2))
    capf = jnp.float32(cap)

    def count(flag):
        per_lane = jnp.sum(flag.astype(F32), axis=0, keepdims=True)
        return jnp.sum(per_lane, axis=2, keepdims=True)

    def thr_body(i, thr):
        cand = thr | jnp.left_shift(jnp.int32(1), 30 - i)
        return jnp.where(count(bits >= cand) >= capf, cand, thr)

    thr = lax.fori_loop(0, 31, thr_body, jnp.zeros((1, ne, 1), jnp.int32))
    above = bits > thr
    tied = bits == thr
    need = capf - count(above)
    nbits = (nt * t - 1).bit_length()

    def idx_body(i, last):
        cand = last | jnp.left_shift(jnp.int32(1), nbits - 1 - i)
        return jnp.where(count(tied & (idx < cand)) <= need - 1.0, cand, last)

    last = lax.fori_loop(0, nbits, idx_body, jnp.zeros((1, ne, 1), jnp.int32))
    sel = (above | (tied & (idx <= last))).astype(F32)

    row = lax.broadcasted_iota(jnp.int32, (t, t), 0)
    col = lax.broadcasted_iota(jnp.int32, (t, t), 1)
    earlier = (row < col).astype(BF16)
    rank = _dot(sel.reshape(nt * ne, t).astype(BF16), earlier).reshape(nt, ne, t)
    code_ref[...] = jnp.where(sel > 0.0, rank, -1.0)
    per_tile = jnp.sum(sel, axis=2, keepdims=True)
    run = jnp.zeros((ne, 1), F32)
    for b in range(nt):
        starts_ref[b] = run.astype(jnp.int32)
        run = run + per_tile[b]


def _route_call(afft, cap):
    nt, ne, t = afft.shape
    full = lambda shape: pl.BlockSpec(shape, lambda: (0,) * len(shape))
    code, starts = pl.pallas_call(
        functools.partial(_route_kernel, cap=cap),
        in_specs=[full((nt, ne, t))],
        out_specs=[full((nt, ne, t)), full((nt, ne, 1))],
        out_shape=[jax.ShapeDtypeStruct((nt, ne, t), F32), jax.ShapeDtypeStruct((nt, ne, 1), jnp.int32)],
        compiler_params=pltpu.CompilerParams(vmem_limit_bytes=VMEM_LIMIT_BYTES),
        name="route",
    )(afft)
    return code, starts.reshape(nt, ne)


def _window_start(starts_ref, tile, e):
    return pl.multiple_of(jnp.bitwise_and(starts_ref[tile, e], -WINDOW_ALIGN), WINDOW_ALIGN)


def _tile_is_short(starts_ref, tile, ntiles, cap):
    nxt = jnp.minimum(tile + 1, ntiles - 1)
    longest = jnp.int32(0)
    for e in range(N_EXPERTS):
        end = jnp.where(tile + 1 < ntiles, starts_ref[nxt, e], cap)
        longest = jnp.maximum(longest, end - starts_ref[tile, e])
    return longest <= SHORT_SEG


def _dispatch_kernel(starts_ref, xn_ref, code_ref, xe_hbm, buf, sem, *, cap):
    b = pl.program_id(0)
    nb = pl.num_programs(0)
    slot = b % 2
    t = ROUTE_TILE
    prev = jnp.maximum(b - 1, 0)
    short_now = _tile_is_short(starts_ref, b, nb, cap)
    short_prev = _tile_is_short(starts_ref, prev, nb, cap)

    def win_copy(sl, e, tile, rows):
        return pltpu.make_async_copy(buf.at[sl, e, pl.ds(0, rows), :],
                                     xe_hbm.at[e, pl.ds(_window_start(starts_ref, tile, e), rows), :],
                                     sem.at[sl, e])

    @pl.when(b == 0)
    def _():
        buf[1] = jnp.zeros(buf.shape[1:], BF16)
        fills = [pltpu.make_async_copy(buf.at[1, e], xe_hbm.at[e, pl.ds(cap, WINDOW_ROWS), :], sem.at[1, e])
                 for e in range(N_EXPERTS)]
        for f in fills:
            f.start()
        for f in fills:
            f.wait()

    def fill_windows(rows, stacked):
        code = code_ref[...]
        rowf = lax.broadcasted_iota(jnp.int32, (rows, t), 0).astype(F32)
        xn = xn_ref[...]
        takes = []
        for e in range(N_EXPERTS):
            pos = code[e:e + 1, :] + (starts_ref[b, e] - _window_start(starts_ref, b, e)).astype(F32)
            takes.append(jnp.logical_and(code[e:e + 1, :] >= 0.0, pos == rowf).astype(BF16))
        if stacked:
            wins_all = _dot(jnp.concatenate(takes, axis=0), xn)
            wins = [wins_all[rows * e:rows * (e + 1), :] for e in range(N_EXPERTS)]
        else:
            wins = [_dot(take, xn) for take in takes]
        for e in range(N_EXPERTS):
            back = pl.multiple_of(_window_start(starts_ref, b, e) - _window_start(starts_ref, prev, e), WINDOW_ALIGN)
            carry = buf[1 - slot, e, pl.ds(back, WINDOW_ALIGN), :].astype(F32)
            buf[slot, e, 0:WINDOW_ALIGN, :] = (wins[e][0:WINDOW_ALIGN, :] + carry).astype(BF16)
            buf[slot, e, WINDOW_ALIGN:rows, :] = wins[e][WINDOW_ALIGN:rows, :].astype(BF16)

    def for_windows(cond, rows, action):
        @pl.when(cond)
        def _():
            for e in range(N_EXPERTS):
                action(e, rows)

    long_now = jnp.logical_not(short_now)
    pl.when(short_now)(lambda: fill_windows(SHORT_ROWS, True))
    pl.when(long_now)(lambda: fill_windows(WINDOW_ROWS, False))
    for_windows(jnp.logical_and(b > 0, short_prev), SHORT_ROWS, lambda e, r: win_copy(1 - slot, e, b - 1, r).wait())
    for_windows(jnp.logical_and(b > 0, jnp.logical_not(short_prev)), WINDOW_ROWS,
                lambda e, r: win_copy(1 - slot, e, b - 1, r).wait())
    for_windows(short_now, SHORT_ROWS, lambda e, r: win_copy(slot, e, b, r).start())
    for_windows(long_now, WINDOW_ROWS, lambda e, r: win_copy(slot, e, b, r).start())
    last = b == nb - 1
    for_windows(jnp.logical_and(last, short_now), SHORT_ROWS, lambda e, r: win_copy(slot, e, b, r).wait())
    for_windows(jnp.logical_and(last, long_now), WINDOW_ROWS, lambda e, r: win_copy(slot, e, b, r).wait())


def _dispatch_call(starts, code, xn, cap):
    nt, ne, t = code.shape
    d = xn.shape[1]
    grid_spec = pltpu.PrefetchScalarGridSpec(
        num_scalar_prefetch=1,
        grid=(nt,),
        in_specs=[pl.BlockSpec((t, d), lambda b, s: (b, 0)),
                  pl.BlockSpec((None, ne, t), lambda b, s: (b, 0, 0))],
        out_specs=pl.BlockSpec(memory_space=pl.ANY),
        scratch_shapes=[pltpu.VMEM((2, ne, WINDOW_ROWS, d), BF16), pltpu.SemaphoreType.DMA((2, ne))],
    )
    return pl.pallas_call(
        functools.partial(_dispatch_kernel, cap=cap),
        grid_spec=grid_spec,
        out_shape=jax.ShapeDtypeStruct((ne, cap + WINDOW_ROWS, d), BF16),
        compiler_params=pltpu.CompilerParams(dimension_semantics=("arbitrary",), vmem_limit_bytes=VMEM_LIMIT_BYTES),
        name="dispatch",
    )(starts, xn, code)


def _combine_kernel(starts_ref, x2_ref, aff_ref, code_ref, gfin_ref, ye_hbm, o_ref, buf, sem, *, cap):
    b = pl.program_id(0)
    nb = pl.num_programs(0)
    slot = b % 2
    t = ROUTE_TILE
    nxt = jnp.minimum(b + 1, nb - 1)
    short_now = _tile_is_short(starts_ref, b, nb, cap)
    short_next = _tile_is_short(starts_ref, nxt, nb, cap)
    long_now = jnp.logical_not(short_now)

    def win_start(tile, e, rows):
        aligned = jnp.bitwise_and(starts_ref[tile, e], -WINDOW_ALIGN)
        return pl.multiple_of(jnp.minimum(aligned, cap - rows), WINDOW_ALIGN)

    def win_copy(sl, e, tile, rows):
        return pltpu.make_async_copy(ye_hbm.at[e, pl.ds(win_start(tile, e, rows), rows), :],
                                     buf.at[sl, e, pl.ds(0, rows), :], sem.at[sl, e])

    def for_windows(cond, rows, action):
        @pl.when(cond)
        def _():
            for e in range(N_EXPERTS):
                action(e, rows)

    first = b == 0
    for_windows(jnp.logical_and(first, short_now), SHORT_ROWS, lambda e, r: win_copy(0, e, 0, r).start())
    for_windows(jnp.logical_and(first, long_now), WINDOW_ROWS, lambda e, r: win_copy(0, e, 0, r).start())
    more = b + 1 < nb
    for_windows(jnp.logical_and(more, short_next), SHORT_ROWS, lambda e, r: win_copy(1 - slot, e, b + 1, r).start())
    for_windows(jnp.logical_and(more, jnp.logical_not(short_next)), WINDOW_ROWS,
                lambda e, r: win_copy(1 - slot, e, b + 1, r).start())

    def ranks_and_gates():
        row = lax.broadcasted_iota(jnp.int32, (t, t), 0)
        col = lax.broadcasted_iota(jnp.int32, (t, t), 1)
        ranks = _dot_nt((row == col).astype(BF16), code_ref[...].astype(BF16))
        return ranks, jnp.where(ranks >= 0.0, aff_ref[:, 0:N_EXPERTS], 0.0)

    def long_path():
        ranks, gates = ranks_and_gates()
        colf = lax.broadcasted_iota(jnp.int32, (t, t), 1).astype(F32)
        for e in range(N_EXPERTS):
            win_copy(slot, e, b, WINDOW_ROWS).wait()
        acc = jnp.zeros((t, D_MODEL), F32)
        tail = jnp.zeros((t, t), F32)
        for e in range(N_EXPERTS):
            pos = ranks[:, e:e + 1] + (starts_ref[b, e] - win_start(b, e, WINDOW_ROWS)).astype(F32)
            g = gates[:, e:e + 1]
            acc = acc + _dot(jnp.where(pos == colf, g, 0.0).astype(BF16), buf[slot, e, 0:t, :])
            in_tail = jnp.logical_and(pos >= float(t), pos - float(t - WINDOW_ALIGN * e) == colf)
            tail = tail + jnp.where(in_tail, g, 0.0)
        tail_rows = jnp.concatenate([buf[slot, e, t:WINDOW_ROWS, :] for e in range(N_EXPERTS)], axis=0)
        acc = acc + _dot(tail.astype(BF16), tail_rows)
        o_ref[...] = _rms(x2_ref[...] + acc, gfin_ref[...])

    def short_path():
        ranks, gates = ranks_and_gates()
        width = N_EXPERTS * SHORT_ROWS
        lane = lax.broadcasted_iota(jnp.int32, (1, N_EXPERTS), 1)
        shift = jnp.zeros((1, N_EXPERTS), F32)
        for e in range(N_EXPERTS):
            shift = jnp.where(lane == e, (starts_ref[b, e] - win_start(b, e, SHORT_ROWS)).astype(F32), shift)
        assert SHORT_ROWS & (SHORT_ROWS - 1) == 0
        group = jnp.right_shift(lax.broadcasted_iota(jnp.int32, (N_EXPERTS, width), 1), SHORT_ROWS.bit_length() - 1)
        spread = (group == lax.broadcasted_iota(jnp.int32, (N_EXPERTS, width), 0)).astype(BF16)
        pos = _dot((ranks + shift).astype(BF16), spread)
        g = _dot(gates.astype(BF16), spread)
        colf = jnp.bitwise_and(lax.broadcasted_iota(jnp.int32, (t, width), 1), SHORT_ROWS - 1).astype(F32)
        for e in range(N_EXPERTS):
            win_copy(slot, e, b, SHORT_ROWS).wait()
        rows = jnp.concatenate([buf[slot, e, 0:SHORT_ROWS, :] for e in range(N_EXPERTS)], axis=0)
        acc = _dot(jnp.where(pos == colf, g, 0.0).astype(BF16), rows)
        o_ref[...] = _rms(x2_ref[...] + acc, gfin_ref[...])

    pl.when(short_now)(short_path)
    pl.when(long_now)(long_path)


def _combine_call(starts, x2, aff, code, final_g, ye, cap):
    nt, ne, t = code.shape
    n, d = x2.shape
    assert WINDOW_ALIGN * ne == t and cap >= WINDOW_ROWS and cap % WINDOW_ALIGN == 0
    grid_spec = pltpu.PrefetchScalarGridSpec(
        num_scalar_prefetch=1,
        grid=(nt,),
        in_specs=[pl.BlockSpec((t, d), lambda b, s: (b, 0)),
                  pl.BlockSpec((t, LANES), lambda b, s: (b, 0)),
                  pl.BlockSpec((None, ne, t), lambda b, s: (b, 0, 0)),
                  pl.BlockSpec((1, d), lambda b, s: (0, 0)),
                  pl.BlockSpec(memory_space=pl.ANY)],
        out_specs=pl.BlockSpec((t, d), lambda b, s: (b, 0)),
        scratch_shapes=[pltpu.VMEM((2, ne, WINDOW_ROWS, d), BF16), pltpu.SemaphoreType.DMA((2, ne))],
    )
    return pl.pallas_call(
        functools.partial(_combine_kernel, cap=cap),
        grid_spec=grid_spec,
        out_shape=jax.ShapeDtypeStruct((n, d), F32),
        compiler_params=pltpu.CompilerParams(dimension_semantics=("arbitrary",), vmem_limit_bytes=VMEM_LIMIT_BYTES),
        name="combine",
    )(starts, x2, aff, code, final_g, ye)


def _rope_angles(seq, dim):
    inv = 1.0 / (ROPE_THETA ** (jnp.arange(0, dim, 2, dtype=F32) / dim))
    ang = jnp.arange(seq, dtype=F32)[:, None] * inv[None, :]
    return jnp.cos(ang), jnp.sin(ang)


def _mla_tables(seq, scale):
    c, s = _rope_angles(seq, MLA_ROPE_DIM)
    ones = jnp.ones((seq, MLA_NOPE_DIM), F32)
    zeros_n = jnp.zeros((seq, MLA_NOPE_DIM), F32)
    zeros_p = jnp.zeros((seq, HEAD_PAD - MLA_QK_DIM), F32)
    cos = jnp.concatenate([ones, c, c, zeros_p], axis=1) * scale
    sin = jnp.concatenate([zeros_n, -s, s, zeros_p], axis=1) * scale
    return cos, sin


def _ret_tables(seq, scale):
    c, s = _rope_angles(seq, RET_QK_DIM)
    cos = jnp.concatenate([c, c, c, c], axis=1) * scale
    sin = jnp.concatenate([-s, s, -s, s], axis=1) * scale
    return cos, sin


def _prepare_weights(norm_mix_g, w_in, q_norm_g, w_uq, kv_norm_g, w_ukv, ret_gn_g, w_branch_a, w_branch_b, w_out,
                     norm_ffn_g, w_router, norm_final_g):
    offs, acc = [], 0
    for width in IN_SPLITS:
        offs.append((acc, acc + width))
        acc += width
    w_cq, w_ckv, w_kr, w_rq, w_rk, w_rv, w_rg, w_ga, w_gb = [w_in[:, a:b].astype(BF16) for a, b in offs]
    w_kr = jnp.pad(w_kr, ((0, 0), (0, LANES - MLA_ROPE_DIM)))

    uq = w_uq.reshape(Q_LORA_RANK, MLA_HEADS, MLA_QK_DIM)
    wq_pad = jnp.pad(uq, ((0, 0), (0, 0), (0, HEAD_PAD - MLA_QK_DIM))).reshape(Q_LORA_RANK, MLA_PAD_WIDTH)
    ukv = w_ukv.reshape(KV_LORA_RANK, MLA_HEADS, MLA_NOPE_DIM + MLA_V_DIM)
    wk_pad = jnp.pad(ukv[:, :, :MLA_NOPE_DIM], ((0, 0), (0, 0), (0, HEAD_PAD - MLA_NOPE_DIM)))
    wk_pad = wk_pad.reshape(KV_LORA_RANK, MLA_PAD_WIDTH)
    wv = jnp.pad(ukv[:, :, MLA_NOPE_DIM:], ((0, 0), (0, 0), (0, HEAD_PAD - MLA_V_DIM)))
    wv = wv.reshape(KV_LORA_RANK, MLA_PAD_WIDTH).T
    src = jnp.arange(LANES)[:, None]
    dst = jnp.arange(MLA_PAD_WIDTH)[None, :]
    place = ((dst % HEAD_PAD) - MLA_NOPE_DIM == src) & (src < MLA_ROPE_DIM)

    pre_w = (norm_mix_g.reshape(1, -1), w_cq, w_ckv, w_kr, w_rq, w_rk, w_rv, w_rg, w_ga, w_gb,
             q_norm_g.reshape(1, -1), wq_pad.astype(BF16), kv_norm_g.reshape(1, -1), wk_pad.astype(BF16),
             place.astype(BF16), wv.astype(BF16))

    wr_hi = w_router.astype(BF16)
    wr_lo = (w_router - wr_hi.astype(F32)).astype(BF16)
    pad_to = lambda w: jnp.pad(w, ((0, 0), (0, LANES - w.shape[1])))
    post_w = (w_branch_a.astype(BF16), w_branch_b.astype(BF16), w_out.astype(BF16), norm_ffn_g.reshape(1, -1),
              pad_to(jnp.concatenate([wr_hi, wr_lo], axis=1)), pad_to(wr_hi))
    return pre_w, post_w, ret_gn_g.reshape(1, -1), norm_final_g.reshape(1, -1)


def _encode(x, pre_w, post_w, log_gamma, gn_g, final_g, wg, wu, wd):
    batch, seq, d = x.shape
    n = batch * seq
    x2d = x.reshape(n, d)
    tables = (*_mla_tables(seq, MLA_QK_DIM ** -0.5 * LOG2_E), *_mla_tables(seq, 1.0),
              *_ret_tables(seq, 1.0), *_ret_tables(seq, RET_QK_DIM ** -0.5))
    qcat, kcat, vt, rq, rk, rv, rg, ga, gb = _pre_call(x2d, seq, pre_w, tables)
    o = _attn_call(qcat, kcat, vt, batch, seq)
    r = _ret_call(log_gamma, rq, rk, rv, rg, gn_g, batch, seq)
    x2, xn, aff, afft = _post_call(o, r, ga, gb, x2d, post_w)

    cap = CAPACITY_FACTOR * n // N_EXPERTS
    code, starts = _route_call(afft, cap)
    xe = _dispatch_call(starts, code, xn, cap)
    ye = _ffn_call(xe, wg, wu, wd, cap)
    return _combine_call(starts, x2, aff, code, final_g, ye, cap).reshape(batch, seq, d)


def kernel(x_prompt, x_sample, norm_mix_g, w_in, q_norm_g, w_uq, kv_norm_g, w_ukv, ret_decay_fwd, ret_decay_bwd,
           ret_gn_g, w_branch_a, w_branch_b, w_out, norm_ffn_g, w_router, w_exp_gate, w_exp_up, w_exp_down,
           norm_final_g):
    assert norm_mix_g.shape[0] == 1, "single-layer trunk"
    pre_w, post_w, gn_g, final_g = _prepare_weights(
        norm_mix_g[0], w_in[0], q_norm_g[0], w_uq[0], kv_norm_g[0], w_ukv[0], ret_gn_g[0], w_branch_a[0],
        w_branch_b[0], w_out[0], norm_ffn_g[0], w_router[0], norm_final_g)
    log_gamma = jnp.stack([jax.nn.log_sigmoid(ret_decay_fwd[0].astype(F32)),
                           jax.nn.log_sigmoid(ret_decay_bwd[0].astype(F32))])
    wg = w_exp_gate[0].astype(BF16)
    wu = w_exp_up[0].astype(BF16)
    wd = w_exp_down[0].astype(BF16)
    enc = functools.partial(_encode, pre_w=pre_w, post_w=post_w, log_gamma=log_gamma, gn_g=gn_g, final_g=final_g,
                            wg=wg, wu=wu, wd=wd)
    return enc(x_prompt), enc(x_sample)
```

```python
import functools

import jax
import jax.numpy as jnp
from jax import lax
from jax.experimental import pallas as pl
from jax.experimental.pallas import tpu as pltpu

D_MODEL = 1024
MLA_HEADS = 8
MLA_NOPE_DIM = 64
MLA_ROPE_DIM = 32
MLA_V_DIM = 64
MLA_QK_DIM = MLA_NOPE_DIM + MLA_ROPE_DIM
Q_LORA_RANK = 384
KV_LORA_RANK = 256
RET_HEADS = 8
RET_QK_DIM = 64
RET_V_DIM = 128
RET_QK_WIDTH = RET_HEADS * RET_QK_DIM
RET_V_WIDTH = RET_HEADS * RET_V_DIM
N_EXPERTS = 16
CAPACITY_FACTOR = 2
ROPE_THETA = 10000.0
EPS = 1e-6
GN_EPS = 1e-5
IN_SPLITS = (Q_LORA_RANK, KV_LORA_RANK, MLA_ROPE_DIM, RET_QK_WIDTH, RET_QK_WIDTH, RET_V_WIDTH, RET_V_WIDTH,
             D_MODEL, D_MODEL)

LANES = 128
SUBLANES = 8
HEAD_PAD = LANES
MLA_PAD_WIDTH = MLA_HEADS * HEAD_PAD
VMEM_LIMIT_BYTES = 56 * 1024 * 1024

ROUTE_TILE = 256
PRE_TILE = 512
POST_TILE = ROUTE_TILE
VT_KEYS = PRE_TILE
LOG2_E = 1.4426950408889634
WINDOW_ALIGN = 16
WINDOW_ROWS = ROUTE_TILE + WINDOW_ALIGN
SHORT_SEG = 48
SHORT_ROWS = SHORT_SEG + WINDOW_ALIGN
ATTN_Q_TILE = 1024
ATTN_K_TILE = 512
ATTN_UNROLL = 3
ATTN_JUMP_LIMIT = 60.0
RET_CHUNK = 256
RET_BWD_UNROLL = 8
RET_FWD_UNROLL = 8
FFN_ROW_TILE = 256

F32 = jnp.float32
BF16 = jnp.bfloat16


def _dot(a, b):
    return jnp.dot(a, b, preferred_element_type=F32)


def _dot_nt(a, b):
    return lax.dot_general(a, b, (((1,), (1,)), ((), ())), preferred_element_type=F32)


def _dot_tn(a, b):
    return lax.dot_general(a, b, (((0,), (0,)), ((), ())), preferred_element_type=F32)


def _rms(x, g):
    return x * lax.rsqrt(jnp.mean(x * x, axis=-1, keepdims=True) + EPS) * g


def _sigmoid(x):
    return 1.0 / (1.0 + jnp.exp(-x))


def _rope_lanes(blk, cos, sin, half):
    lane = lax.broadcasted_iota(jnp.int32, blk.shape, 1)
    upper = (lane % (2 * half)) >= half
    partner = jnp.where(upper, pltpu.roll(blk, half, 1), pltpu.roll(blk, LANES - half, 1))
    return blk * cos + partner * sin


def _const_spec(shape):
    nd = len(shape)
    return pl.BlockSpec(shape, lambda *_: (0,) * nd, pipeline_mode=pl.Buffered(1))


def _pre_kernel(x_ref, gmix_ref, wcq_ref, wckv_ref, wkr_ref, wrq_ref, wrk_ref, wrv_ref, wrg_ref, wga_ref, wgb_ref,
                gq_ref, wq_ref, gkv_ref, wk_ref, place_ref, wv_ref,
                cosq_ref, sinq_ref, cosk_ref, sink_ref, cosrq_ref, sinrq_ref, cosrk_ref, sinrk_ref,
                q_out, k_out, vt_out, rq_out, rk_out, rv_out, rg_out, ga_out, gb_out):
    h = _rms(x_ref[...], gmix_ref[...]).astype(BF16)

    qn = _rms(_dot(h, wcq_ref[...]), gq_ref[...]).astype(BF16)
    q = _dot(qn, wq_ref[...])
    cosq, sinq = cosq_ref[...], sinq_ref[...]
    for j in range(MLA_HEADS):
        sl = slice(HEAD_PAD * j, HEAD_PAD * (j + 1))
        q_out[:, sl] = _rope_lanes(q[:, sl], cosq, sinq, MLA_ROPE_DIM // 2).astype(BF16)

    kvn = _rms(_dot(h, wckv_ref[...]), gkv_ref[...]).astype(BF16)
    kr = _dot(h, wkr_ref[...]).astype(BF16)
    k = _dot(kvn, wk_ref[...]) + _dot(kr, place_ref[...])
    cosk, sink = cosk_ref[...], sink_ref[...]
    for j in range(MLA_HEADS):
        sl = slice(HEAD_PAD * j, HEAD_PAD * (j + 1))
        k_out[:, sl] = _rope_lanes(k[:, sl], cosk, sink, MLA_ROPE_DIM // 2).astype(BF16)
    vt = _dot_nt(wv_ref[...], kvn)
    vrow = lax.broadcasted_iota(jnp.int32, vt.shape, 0)
    vt_out[...] = jnp.where(vrow % HEAD_PAD == MLA_V_DIM, 1.0, vt).astype(BF16)

    rq = _dot(h, wrq_ref[...])
    rk = _dot(h, wrk_ref[...])
    cosrq, sinrq, cosrk, sinrk = cosrq_ref[...], sinrq_ref[...], cosrk_ref[...], sinrk_ref[...]
    for j in range(RET_QK_WIDTH // LANES):
        sl = slice(LANES * j, LANES * (j + 1))
        rq_out[:, sl] = _rope_lanes(rq[:, sl], cosrq, sinrq, RET_QK_DIM // 2).astype(BF16)
        rk_out[:, sl] = _rope_lanes(rk[:, sl], cosrk, sinrk, RET_QK_DIM // 2).astype(BF16)

    rv_out[...] = _dot(h, wrv_ref[...]).astype(BF16)
    rg_out[...] = _dot(h, wrg_ref[...]).astype(BF16)
    ga_out[...] = _dot(h, wga_ref[...]).astype(BF16)
    gb_out[...] = _dot(h, wgb_ref[...]).astype(BF16)


def _pre_call(x2d, seq, weights, tables):
    n = x2d.shape[0]
    tm = PRE_TILE
    tiles_per_seq = seq // tm
    row = lambda width: pl.BlockSpec((tm, width), lambda i: (i, 0))
    tab = pl.BlockSpec((tm, LANES), lambda i: (i % tiles_per_seq, 0))
    out_widths = (MLA_PAD_WIDTH, MLA_PAD_WIDTH, None, RET_QK_WIDTH, RET_QK_WIDTH,
                  RET_V_WIDTH, RET_V_WIDTH, D_MODEL, D_MODEL)
    vt_spec = pl.BlockSpec((None, None, MLA_PAD_WIDTH, tm), lambda i: (i // tiles_per_seq, i % tiles_per_seq, 0, 0))
    vt_shape = jax.ShapeDtypeStruct((n // seq, tiles_per_seq, MLA_PAD_WIDTH, tm), BF16)
    return pl.pallas_call(
        _pre_kernel,
        grid=(n // tm,),
        in_specs=[row(D_MODEL)] + [_const_spec(w.shape) for w in weights] + [tab] * len(tables),
        out_specs=[vt_spec if w is None else row(w) for w in out_widths],
        out_shape=[vt_shape if w is None else jax.ShapeDtypeStruct((n, w), BF16) for w in out_widths],
        compiler_params=pltpu.CompilerParams(dimension_semantics=("arbitrary",), vmem_limit_bytes=VMEM_LIMIT_BYTES),
        name="pre_proj",
    )(x2d, *weights, *tables)


def _attn_kernel(q_ref, k_ref, vt_ref, o_ref, st_scr, pt_scr, mc_scr, m_scr, acc_scr, jump_scr, *, tk, nk):
    sub = tk // VT_KEYS
    heads = [slice(HEAD_PAD * hh, HEAD_PAD * (hh + 1)) for hh in range(2)]

    def qk(kb, sl):
        off = pl.multiple_of(kb * tk, tk)
        return _dot_nt(k_ref[pl.ds(off, tk), sl], q_ref[:, sl])

    def value_product(kb, sl, pt):
        pv = _dot(vt_ref[kb * sub, sl, :], pt[0:VT_KEYS, :])
        for j in range(1, sub):
            pv = pv + _dot(vt_ref[kb * sub + j, sl, :], pt[VT_KEYS * j:VT_KEYS * (j + 1), :])
        return pv

    def pipeline(first, scores, accumulate):
        first()
        if nk > 1:
            assert nk % 2 == 0

            def body(j, carry):
                kb = 2 * j
                scores(kb + 1, 1)
                accumulate(kb, 0)
                scores(kb + 2, 0)
                accumulate(kb + 1, 1)
                return carry

            trips = nk // 2 - 1
            lax.fori_loop(0, trips, body, 0, unroll=ATTN_UNROLL if trips >= 2 * ATTN_UNROLL else 1)
            scores(nk - 1, 1)
            accumulate(nk - 2, 0)
            accumulate(nk - 1, 1)
        else:
            accumulate(0, 0)
        outs = []
        for hh in range(2):
            acc = acc_scr[hh]
            outs.append(acc[0:MLA_V_DIM, :] / acc[MLA_V_DIM:MLA_V_DIM + 1, :])
        o_ref[...] = jnp.concatenate(outs, axis=0).T.astype(BF16)

    def fast_first():
        acc_scr[...] = jnp.zeros(acc_scr.shape, F32)
        jump_scr[...] = jnp.zeros(jump_scr.shape, F32)
        for hh, sl in enumerate(heads):
            st = qk(0, sl)
            ref = jnp.max(st[0:SUBLANES, :], axis=0, keepdims=True)
            pt_scr[0, hh] = jnp.exp2(st - ref).astype(BF16)
            mc_scr[0, hh] = jnp.max(st, axis=0, keepdims=True)
            m_scr[hh] = ref

    def fast_scores(kb, buf):
        for hh, sl in enumerate(heads):
            st = qk(kb, sl)
            ref = jnp.maximum(m_scr[hh], mc_scr[1 - buf, hh])
            pt_scr[buf, hh] = jnp.exp2(st - ref).astype(BF16)
            mc_scr[buf, hh] = jnp.max(st, axis=0, keepdims=True)

    def fast_accumulate(kb, buf):
        for hh, sl in enumerate(heads):
            m_old = m_scr[hh]
            bmax = mc_scr[buf, hh]
            m_new = jnp.maximum(m_old, bmax)
            jump_scr[hh] = jnp.maximum(jump_scr[hh], bmax - m_old)
            acc_scr[hh] = (acc_scr[hh] + value_product(kb, sl, pt_scr[buf, hh])) * jnp.exp2(m_old - m_new)
            m_scr[hh] = m_new

    def safe_first():
        m_scr[...] = jnp.full(m_scr.shape, -jnp.inf, F32)
        acc_scr[...] = jnp.zeros(acc_scr.shape, F32)
        safe_scores(0, 0)

    def safe_scores(kb, buf):
        for hh, sl in enumerate(heads):
            st = qk(kb, sl)
            st_scr[buf, hh] = st
            mc_scr[buf, hh] = jnp.max(st, axis=0, keepdims=True)

    def safe_accumulate(kb, buf):
        for hh, sl in enumerate(heads):
            m_old = m_scr[hh]
            m_new = jnp.maximum(m_old, mc_scr[buf, hh])
            pt = jnp.exp2(st_scr[buf, hh] - m_new).astype(BF16)
            acc_scr[hh] = jnp.exp2(m_old - m_new) * acc_scr[hh] + value_product(kb, sl, pt)
            m_scr[hh] = m_new

    pipeline(fast_first, fast_scores, fast_accumulate)
    jump = jnp.max(jnp.maximum(jump_scr[0], jump_scr[1]))
    pl.when(jump > ATTN_JUMP_LIMIT)(lambda: pipeline(safe_first, safe_scores, safe_accumulate))


def _attn_call(qcat, kcat, vt, batch, seq):
    n = qcat.shape[0]
    tq = min(ATTN_Q_TILE, seq)
    tk = min(ATTN_K_TILE, seq)
    nq = seq // tq
    pairs = MLA_HEADS // 2
    return pl.pallas_call(
        functools.partial(_attn_kernel, tk=tk, nk=seq // tk),
        grid=(batch, pairs, nq),
        in_specs=[
            pl.BlockSpec((tq, 2 * HEAD_PAD), lambda b, p, i: (b * nq + i, p)),
            pl.BlockSpec((seq, 2 * HEAD_PAD), lambda b, p, i: (b, p)),
            pl.BlockSpec((None, seq // VT_KEYS, 2 * HEAD_PAD, VT_KEYS), lambda b, p, i: (b, 0, p, 0)),
        ],
        out_specs=pl.BlockSpec((tq, 2 * MLA_V_DIM), lambda b, p, i: (b * nq + i, p)),
        out_shape=jax.ShapeDtypeStruct((n, MLA_HEADS * MLA_V_DIM), BF16),
        scratch_shapes=[pltpu.VMEM((2, 2, tk, tq), F32), pltpu.VMEM((2, 2, tk, tq), BF16),
                        pltpu.VMEM((2, 2, 1, tq), F32), pltpu.VMEM((2, 1, tq), F32),
                        pltpu.VMEM((2, HEAD_PAD, tq), F32), pltpu.VMEM((2, 1, tq), F32)],
        compiler_params=pltpu.CompilerParams(dimension_semantics=("arbitrary",) * 3,
                                             vmem_limit_bytes=VMEM_LIMIT_BYTES),
        name="mla_attn",
    )(qcat, kcat, vt)


def _ret_kernel(lg_ref, q_ref, k_ref, v_ref, rg_ref, gn_ref, o_ref, rb_scr, *, chunk, nchunks):
    c_len = chunk
    pair = pl.program_id(1)
    lane = lax.broadcasted_iota(jnp.int32, (c_len, LANES), 1)
    pos = lax.broadcasted_iota(jnp.int32, (c_len, LANES), 0).astype(F32)
    ii = lax.broadcasted_iota(jnp.int32, (c_len, c_len), 0)
    jj = lax.broadcasted_iota(jnp.int32, (c_len, c_len), 1)
    diff = (ii - jj).astype(F32)
    sq = (LANES, LANES)
    two = range(2)
    lgf = [lg_ref[0, 2 * pair + hh] for hh in two]
    lgb = [lg_ref[1, 2 * pair + hh] for hh in two]
    mask = [(lane // RET_QK_DIM) == hh for hh in two]
    vsl = [slice(RET_V_DIM * hh, RET_V_DIM * (hh + 1)) for hh in two]
    zeta_f = [jnp.exp((c_len - 1.0 - pos) * lgf[hh]) for hh in two]
    xi_f = [jnp.exp((pos + 1.0) * lgf[hh]) for hh in two]
    zeta_b = [jnp.exp(pos * lgb[hh]) for hh in two]
    xi_b = [jnp.exp((c_len - pos) * lgb[hh]) for hh in two]
    gchunk_f = [jnp.exp(jnp.full(sq, c_len, F32) * lgf[hh]) for hh in two]
    gchunk_b = [jnp.exp(jnp.full(sq, c_len, F32) * lgb[hh]) for hh in two]
    decay = [jnp.where(diff >= 0.0, jnp.exp(jnp.maximum(diff, 0.0) * lgf[hh]),
                       jnp.exp(jnp.maximum(-diff, 0.0) * lgb[hh])) for hh in two]
    gn = gn_ref[...]

    def chunk_slice(c):
        return pl.ds(pl.multiple_of(c * c_len, c_len), c_len)

    def bwd_body(t, states):
        c = nchunks - 1 - t
        rows = chunk_slice(c)
        k = k_ref[rows, :].astype(F32)
        new = []
        for hh in two:
            rb_scr[hh, c] = states[hh].astype(BF16)
            kz = (jnp.where(mask[hh], k, 0.0) * zeta_b[hh]).astype(BF16)
            new.append(gchunk_b[hh] * states[hh] + _dot_tn(kz, v_ref[rows, vsl[hh]]))
        return tuple(new)

    zero_states = (jnp.zeros(sq, F32), jnp.zeros(sq, F32))
    lax.fori_loop(0, nchunks, bwd_body, zero_states, unroll=min(RET_BWD_UNROLL, nchunks))

    def fwd_body(c, states):
        rows = chunk_slice(c)
        q = q_ref[rows, :].astype(F32)
        k16 = k_ref[rows, :]
        k = k16.astype(F32)
        qm = [jnp.where(mask[hh], q, 0.0) for hh in two]
        inner = [_dot_nt(qm[hh].astype(BF16), k16) for hh in two]
        ys = []
        for hh in two:
            lhs = jnp.concatenate([(inner[hh] * decay[hh]).astype(BF16), (qm[hh] * xi_f[hh]).astype(BF16),
                                   (qm[hh] * xi_b[hh]).astype(BF16)], axis=1)
            rhs = jnp.concatenate([v_ref[rows, vsl[hh]], states[hh].astype(BF16), rb_scr[hh, c]], axis=0)
            ys.append(_dot(lhs, rhs))
        new = []
        for hh in two:
            y = ys[hh]
            mu = jnp.mean(y, axis=-1, keepdims=True)
            yc = y - mu
            var = jnp.mean(yc * yc, axis=-1, keepdims=True)
            yn = yc * lax.rsqrt(var + GN_EPS) * gn[:, vsl[hh]]
            rg = rg_ref[rows, vsl[hh]].astype(F32)
            o_ref[rows, vsl[hh]] = (rg * _sigmoid(rg) * yn).astype(BF16)
            kz = (jnp.where(mask[hh], k, 0.0) * zeta_f[hh]).astype(BF16)
            new.append(gchunk_f[hh] * states[hh] + _dot_tn(kz, v_ref[rows, vsl[hh]]))
        return tuple(new)

    lax.fori_loop(0, nchunks, fwd_body, zero_states, unroll=min(RET_FWD_UNROLL, nchunks))


def _ret_call(log_gamma, rq, rk, rv, rg, gn_g, batch, seq):
    n = rq.shape[0]
    chunk = min(RET_CHUNK, seq)
    nchunks = seq // chunk
    qk_blk = pl.BlockSpec((seq, LANES), lambda b, p: (b, p))
    v_blk = pl.BlockSpec((seq, 2 * RET_V_DIM), lambda b, p: (b, p))
    return pl.pallas_call(
        functools.partial(_ret_kernel, chunk=chunk, nchunks=nchunks),
        grid=(batch, RET_HEADS // 2),
        in_specs=[pl.BlockSpec(memory_space=pltpu.SMEM), qk_blk, qk_blk, v_blk, v_blk,
                  pl.BlockSpec((1, 2 * RET_V_DIM), lambda b, p: (0, p))],
        out_specs=v_blk,
        out_shape=jax.ShapeDtypeStruct((n, RET_V_WIDTH), BF16),
        scratch_shapes=[pltpu.VMEM((2, nchunks, LANES, LANES), BF16)],
        compiler_params=pltpu.CompilerParams(dimension_semantics=("arbitrary",) * 2,
                                             vmem_limit_bytes=VMEM_LIMIT_BYTES),
        name="retention",
    )(log_gamma, rq, rk, rv, rg, gn_g)


def _post_kernel(o_ref, r_ref, ga_ref, gb_ref, x_ref, wa_ref, wb_ref, wo_ref, gffn_ref, wr_hi_lo_ref, wr_hi_ref,
                 x2_out, xn_out, aff_out, afft_out):
    a = _dot(o_ref[...], wa_ref[...])
    r = _dot(r_ref[...], wb_ref[...])
    mixed = _sigmoid(ga_ref[...].astype(F32)) * a + _sigmoid(gb_ref[...].astype(F32)) * r
    x2 = x_ref[...] + _dot(mixed.astype(BF16), wo_ref[...])
    x2_out[...] = x2
    xn = _rms(x2, gffn_ref[...])
    xn_hi = xn.astype(BF16)
    xn_out[...] = xn_hi
    xn_lo = (xn - xn_hi.astype(F32)).astype(BF16)
    t = _dot(xn_hi, wr_hi_lo_ref[...]) + _dot(xn_lo, wr_hi_ref[...])
    logits = t + pltpu.roll(t, LANES - N_EXPERTS, 1)
    lane = lax.broadcasted_iota(jnp.int32, logits.shape, 1)
    logits = jnp.where(lane < N_EXPERTS, logits, -jnp.inf)
    e = jnp.exp(logits - jnp.max(logits, axis=-1, keepdims=True))
    aff = e / jnp.sum(e, axis=-1, keepdims=True)
    aff_out[...] = aff
    for r in range(afft_out.shape[0]):
        afft_out[r] = aff[ROUTE_TILE * r:ROUTE_TILE * (r + 1), :].T[0:N_EXPERTS, :]


def _post_call(o, r, ga, gb, x2d, weights):
    n = x2d.shape[0]
    tm = POST_TILE
    rt = ROUTE_TILE
    row = lambda width: pl.BlockSpec((tm, width), lambda i: (i, 0))
    return pl.pallas_call(
        _post_kernel,
        grid=(n // tm,),
        in_specs=[row(MLA_HEADS * MLA_V_DIM), row(RET_V_WIDTH), row(D_MODEL), row(D_MODEL), row(D_MODEL)]
        + [_const_spec(w.shape) for w in weights],
        out_specs=[row(D_MODEL), row(D_MODEL), row(LANES),
                   pl.BlockSpec((tm // rt, N_EXPERTS, rt), lambda i: (i, 0, 0))],
        out_shape=[jax.ShapeDtypeStruct((n, D_MODEL), F32), jax.ShapeDtypeStruct((n, D_MODEL), BF16),
                   jax.ShapeDtypeStruct((n, LANES), F32), jax.ShapeDtypeStruct((n // rt, N_EXPERTS, rt), F32)],
        compiler_params=pltpu.CompilerParams(dimension_semantics=("arbitrary",), vmem_limit_bytes=VMEM_LIMIT_BYTES),
        name="post_mix",
    )(o, r, ga, gb, x2d, *weights)


def _ffn_kernel(xe_ref, wg_ref, wu_ref, wd_ref, ye_ref):
    x = xe_ref[...]
    g = _dot(x, wg_ref[...])
    u = _dot(x, wu_ref[...])
    hid = (g * _sigmoid(g) * u).astype(BF16)
    ye_ref[...] = _dot(hid, wd_ref[...]).astype(BF16)


def _ffn_call(xe, wg, wu, wd, cap):
    n_exp, _, d = xe.shape
    tm = min(FFN_ROW_TILE, cap)
    ff = wg.shape[-1]
    return pl.pallas_call(
        _ffn_kernel,
        grid=(n_exp, cap // tm),
        in_specs=[
            pl.BlockSpec((None, tm, d), lambda e, i: (e, i, 0)),
            pl.BlockSpec((None, d, ff), lambda e, i: (e, 0, 0)),
            pl.BlockSpec((None, d, ff), lambda e, i: (e, 0, 0)),
            pl.BlockSpec((None, ff, d), lambda e, i: (e, 0, 0)),
        ],
        out_specs=pl.BlockSpec((None, tm, d), lambda e, i: (e, i, 0)),
        out_shape=jax.ShapeDtypeStruct((n_exp, cap, d), BF16),
        compiler_params=pltpu.CompilerParams(dimension_semantics=("arbitrary",) * 2,
                                             vmem_limit_bytes=VMEM_LIMIT_BYTES),
        name="expert_ffn",
    )(xe, wg, wu, wd)


def _route_kernel(afft_ref, code_ref, starts_ref, *, cap):
    nt, ne, t = afft_ref.shape
    bits = jnp.maximum(pltpu.bitcast(afft_ref[...], jnp.int32), 0)
    idx = (lax.broadcasted_iota(jnp.int32, (nt, ne, t), 0) * t + lax.broadcasted_iota(jnp.int32, (nt, ne, t), 2))
    capf = jnp.float32(cap)

    def count(flag):
        per_lane = jnp.sum(flag.astype(F32), axis=0, keepdims=True)
        return jnp.sum(per_lane, axis=2, keepdims=True)

    def thr_body(i, thr):
        cand = thr | jnp.left_shift(jnp.int32(1), 30 - i)
        return jnp.where(count(bits >= cand) >= capf, cand, thr)

    thr = lax.fori_loop(0, 31, thr_body, jnp.zeros((1, ne, 1), jnp.int32))
    above = bits > thr
    tied = bits == thr
    need = capf - count(above)
    nbits = (nt * t - 1).bit_length()

    def idx_body(i, last):
        cand = last | jnp.left_shift(jnp.int32(1), nbits - 1 - i)
        return jnp.where(count(tied & (idx < cand)) <= need - 1.0, cand, last)

    last = lax.fori_loop(0, nbits, idx_body, jnp.zeros((1, ne, 1), jnp.int32))
    sel = (above | (tied & (idx <= last))).astype(F32)

    row = lax.broadcasted_iota(jnp.int32, (t, t), 0)
    col = lax.broadcasted_iota(jnp.int32, (t, t), 1)
    earlier = (row < col).astype(BF16)
    rank = _dot(sel.reshape(nt * ne, t).astype(BF16), earlier).reshape(nt, ne, t)
    code_ref[...] = jnp.where(sel > 0.0, rank, -1.0)
    per_tile = jnp.sum(sel, axis=2, keepdims=True)
    run = jnp.zeros((ne, 1), F32)
    for b in range(nt):
        starts_ref[b] = run.astype(jnp.int32)
        run = run + per_tile[b]


def _route_call(afft, cap):
    nt, ne, t = afft.shape
    full = lambda shape: pl.BlockSpec(shape, lambda: (0,) * len(shape))
    code, starts = pl.pallas_call(
        functools.partial(_route_kernel, cap=cap),
        in_specs=[full((nt, ne, t))],
        out_specs=[full((nt, ne, t)), full((nt, ne, 1))],
        out_shape=[jax.ShapeDtypeStruct((nt, ne, t), F32), jax.ShapeDtypeStruct((nt, ne, 1), jnp.int32)],
        compiler_params=pltpu.CompilerParams(vmem_limit_bytes=VMEM_LIMIT_BYTES),
        name="route",
    )(afft)
    return code, starts.reshape(nt, ne)


def _window_start(starts_ref, tile, e):
    return pl.multiple_of(jnp.bitwise_and(starts_ref[tile, e], -WINDOW_ALIGN), WINDOW_ALIGN)


def _tile_is_short(starts_ref, tile, ntiles, cap):
    nxt = jnp.minimum(tile + 1, ntiles - 1)
    longest = jnp.int32(0)
    for e in range(N_EXPERTS):
        end = jnp.where(tile + 1 < ntiles, starts_ref[nxt, e], cap)
        longest = jnp.maximum(longest, end - starts_ref[tile, e])
    return longest <= SHORT_SEG


def _dispatch_kernel(starts_ref, xn_ref, code_ref, xe_hbm, buf, sem, *, cap):
    b = pl.program_id(0)
    nb = pl.num_programs(0)
    slot = b % 2
    t = ROUTE_TILE
    prev = jnp.maximum(b - 1, 0)
    short_now = _tile_is_short(starts_ref, b, nb, cap)
    short_prev = _tile_is_short(starts_ref, prev, nb, cap)

    def win_copy(sl, e, tile, rows):
        return pltpu.make_async_copy(buf.at[sl, e, pl.ds(0, rows), :],
                                     xe_hbm.at[e, pl.ds(_window_start(starts_ref, tile, e), rows), :],
                                     sem.at[sl, e])

    @pl.when(b == 0)
    def _():
        buf[1] = jnp.zeros(buf.shape[1:], BF16)
        fills = [pltpu.make_async_copy(buf.at[1, e], xe_hbm.at[e, pl.ds(cap, WINDOW_ROWS), :], sem.at[1, e])
                 for e in range(N_EXPERTS)]
        for f in fills:
            f.start()
        for f in fills:
            f.wait()

    def fill_windows(rows, stacked):
        code = code_ref[...]
        rowf = lax.broadcasted_iota(jnp.int32, (rows, t), 0).astype(F32)
        xn = xn_ref[...]
        takes = []
        for e in range(N_EXPERTS):
            pos = code[e:e + 1, :] + (starts_ref[b, e] - _window_start(starts_ref, b, e)).astype(F32)
            takes.append(jnp.logical_and(code[e:e + 1, :] >= 0.0, pos == rowf).astype(BF16))
        if stacked:
            wins_all = _dot(jnp.concatenate(takes, axis=0), xn)
            wins = [wins_all[rows * e:rows * (e + 1), :] for e in range(N_EXPERTS)]
        else:
            wins = [_dot(take, xn) for take in takes]
        for e in range(N_EXPERTS):
            back = pl.multiple_of(_window_start(starts_ref, b, e) - _window_start(starts_ref, prev, e), WINDOW_ALIGN)
            carry = buf[1 - slot, e, pl.ds(back, WINDOW_ALIGN), :].astype(F32)
            buf[slot, e, 0:WINDOW_ALIGN, :] = (wins[e][0:WINDOW_ALIGN, :] + carry).astype(BF16)
            buf[slot, e, WINDOW_ALIGN:rows, :] = wins[e][WINDOW_ALIGN:rows, :].astype(BF16)

    def for_windows(cond, rows, action):
        @pl.when(cond)
        def _():
            for e in range(N_EXPERTS):
                action(e, rows)

    long_now = jnp.logical_not(short_now)
    pl.when(short_now)(lambda: fill_windows(SHORT_ROWS, True))
    pl.when(long_now)(lambda: fill_windows(WINDOW_ROWS, False))
    for_windows(jnp.logical_and(b > 0, short_prev), SHORT_ROWS, lambda e, r: win_copy(1 - slot, e, b - 1, r).wait())
    for_windows(jnp.logical_and(b > 0, jnp.logical_not(short_prev)), WINDOW_ROWS,
                lambda e, r: win_copy(1 - slot, e, b - 1, r).wait())
    for_windows(short_now, SHORT_ROWS, lambda e, r: win_copy(slot, e, b, r).start())
    for_windows(long_now, WINDOW_ROWS, lambda e, r: win_copy(slot, e, b, r).start())
    last = b == nb - 1
    for_windows(jnp.logical_and(last, short_now), SHORT_ROWS, lambda e, r: win_copy(slot, e, b, r).wait())
    for_windows(jnp.logical_and(last, long_now), WINDOW_ROWS, lambda e, r: win_copy(slot, e, b, r).wait())


def _dispatch_call(starts, code, xn, cap):
    nt, ne, t = code.shape
    d = xn.shape[1]
    grid_spec = pltpu.PrefetchScalarGridSpec(
        num_scalar_prefetch=1,
        grid=(nt,),
        in_specs=[pl.BlockSpec((t, d), lambda b, s: (b, 0)),
                  pl.BlockSpec((None, ne, t), lambda b, s: (b, 0, 0))],
        out_specs=pl.BlockSpec(memory_space=pl.ANY),
        scratch_shapes=[pltpu.VMEM((2, ne, WINDOW_ROWS, d), BF16), pltpu.SemaphoreType.DMA((2, ne))],
    )
    return pl.pallas_call(
        functools.partial(_dispatch_kernel, cap=cap),
        grid_spec=grid_spec,
        out_shape=jax.ShapeDtypeStruct((ne, cap + WINDOW_ROWS, d), BF16),
        compiler_params=pltpu.CompilerParams(dimension_semantics=("arbitrary",), vmem_limit_bytes=VMEM_LIMIT_BYTES),
        name="dispatch",
    )(starts, xn, code)


def _combine_kernel(starts_ref, x2_ref, aff_ref, code_ref, gfin_ref, ye_hbm, o_ref, buf, sem, *, cap):
    b = pl.program_id(0)
    nb = pl.num_programs(0)
    slot = b % 2
    t = ROUTE_TILE
    nxt = jnp.minimum(b + 1, nb - 1)
    short_now = _tile_is_short(starts_ref, b, nb, cap)
    short_next = _tile_is_short(starts_ref, nxt, nb, cap)
    long_now = jnp.logical_not(short_now)

    def win_start(tile, e, rows):
        aligned = jnp.bitwise_and(starts_ref[tile, e], -WINDOW_ALIGN)
        return pl.multiple_of(jnp.minimum(aligned, cap - rows), WINDOW_ALIGN)

    def win_copy(sl, e, tile, rows):
        return pltpu.make_async_copy(ye_hbm.at[e, pl.ds(win_start(tile, e, rows), rows), :],
                                     buf.at[sl, e, pl.ds(0, rows), :], sem.at[sl, e])

    def for_windows(cond, rows, action):
        @pl.when(cond)
        def _():
            for e in range(N_EXPERTS):
                action(e, rows)

    first = b == 0
    for_windows(jnp.logical_and(first, short_now), SHORT_ROWS, lambda e, r: win_copy(0, e, 0, r).start())
    for_windows(jnp.logical_and(first, long_now), WINDOW_ROWS, lambda e, r: win_copy(0, e, 0, r).start())
    more = b + 1 < nb
    for_windows(jnp.logical_and(more, short_next), SHORT_ROWS, lambda e, r: win_copy(1 - slot, e, b + 1, r).start())
    for_windows(jnp.logical_and(more, jnp.logical_not(short_next)), WINDOW_ROWS,
                lambda e, r: win_copy(1 - slot, e, b + 1, r).start())

    def ranks_and_gates():
        row = lax.broadcasted_iota(jnp.int32, (t, t), 0)
        col = lax.broadcasted_iota(jnp.int32, (t, t), 1)
        ranks = _dot_nt((row == col).astype(BF16), code_ref[...].astype(BF16))
        return ranks, jnp.where(ranks >= 0.0, aff_ref[:, 0:N_EXPERTS], 0.0)

    def long_path():
        ranks, gates = ranks_and_gates()
        colf = lax.broadcasted_iota(jnp.int32, (t, t), 1).astype(F32)
        for e in range(N_EXPERTS):
            win_copy(slot, e, b, WINDOW_ROWS).wait()
        acc = jnp.zeros((t, D_MODEL), F32)
        tail = jnp.zeros((t, t), F32)
        for e in range(N_EXPERTS):
            pos = ranks[:, e:e + 1] + (starts_ref[b, e] - win_start(b, e, WINDOW_ROWS)).astype(F32)
            g = gates[:, e:e + 1]
            acc = acc + _dot(jnp.where(pos == colf, g, 0.0).astype(BF16), buf[slot, e, 0:t, :])
            in_tail = jnp.logical_and(pos >= float(t), pos - float(t - WINDOW_ALIGN * e) == colf)
            tail = tail + jnp.where(in_tail, g, 0.0)
        tail_rows = jnp.concatenate([buf[slot, e, t:WINDOW_ROWS, :] for e in range(N_EXPERTS)], axis=0)
        acc = acc + _dot(tail.astype(BF16), tail_rows)
        o_ref[...] = _rms(x2_ref[...] + acc, gfin_ref[...])

    def short_path():
        ranks, gates = ranks_and_gates()
        width = N_EXPERTS * SHORT_ROWS
        lane = lax.broadcasted_iota(jnp.int32, (1, N_EXPERTS), 1)
        shift = jnp.zeros((1, N_EXPERTS), F32)
        for e in range(N_EXPERTS):
            shift = jnp.where(lane == e, (starts_ref[b, e] - win_start(b, e, SHORT_ROWS)).astype(F32), shift)
        assert SHORT_ROWS & (SHORT_ROWS - 1) == 0
        group = jnp.right_shift(lax.broadcasted_iota(jnp.int32, (N_EXPERTS, width), 1), SHORT_ROWS.bit_length() - 1)
        spread = (group == lax.broadcasted_iota(jnp.int32, (N_EXPERTS, width), 0)).astype(BF16)
        pos = _dot((ranks + shift).astype(BF16), spread)
        g = _dot(gates.astype(BF16), spread)
        colf = jnp.bitwise_and(lax.broadcasted_iota(jnp.int32, (t, width), 1), SHORT_ROWS - 1).astype(F32)
        for e in range(N_EXPERTS):
            win_copy(slot, e, b, SHORT_ROWS).wait()
        rows = jnp.concatenate([buf[slot, e, 0:SHORT_ROWS, :] for e in range(N_EXPERTS)], axis=0)
        acc = _dot(jnp.where(pos == colf, g, 0.0).astype(BF16), rows)
        o_ref[...] = _rms(x2_ref[...] + acc, gfin_ref[...])

    pl.when(short_now)(short_path)
    pl.when(long_now)(long_path)


def _combine_call(starts, x2, aff, code, final_g, ye, cap):
    nt, ne, t = code.shape
    n, d = x2.shape
    assert WINDOW_ALIGN * ne == t and cap >= WINDOW_ROWS and cap % WINDOW_ALIGN == 0
    grid_spec = pltpu.PrefetchScalarGridSpec(
        num_scalar_prefetch=1,
        grid=(nt,),
        in_specs=[pl.BlockSpec((t, d), lambda b, s: (b, 0)),
                  pl.BlockSpec((t, LANES), lambda b, s: (b, 0)),
                  pl.BlockSpec((None, ne, t), lambda b, s: (b, 0, 0)),
                  pl.BlockSpec((1, d), lambda b, s: (0, 0)),
                  pl.BlockSpec(memory_space=pl.ANY)],
        out_specs=pl.BlockSpec((t, d), lambda b, s: (b, 0)),
        scratch_shapes=[pltpu.VMEM((2, ne, WINDOW_ROWS, d), BF16), pltpu.SemaphoreType.DMA((2, ne))],
    )
    return pl.pallas_call(
        functools.partial(_combine_kernel, cap=cap),
        grid_spec=grid_spec,
        out_shape=jax.ShapeDtypeStruct((n, d), F32),
        compiler_params=pltpu.CompilerParams(dimension_semantics=("arbitrary",), vmem_limit_bytes=VMEM_LIMIT_BYTES),
        name="combine",
    )(starts, x2, aff, code, final_g, ye)


def _rope_angles(seq, dim):
    inv = 1.0 / (ROPE_THETA ** (jnp.arange(0, dim, 2, dtype=F32) / dim))
    ang = jnp.arange(seq, dtype=F32)[:, None] * inv[None, :]
    return jnp.cos(ang), jnp.sin(ang)


def _mla_tables(seq, scale):
    c, s = _rope_angles(seq, MLA_ROPE_DIM)
    ones = jnp.ones((seq, MLA_NOPE_DIM), F32)
    zeros_n = jnp.zeros((seq, MLA_NOPE_DIM), F32)
    zeros_p = jnp.zeros((seq, HEAD_PAD - MLA_QK_DIM), F32)
    cos = jnp.concatenate([ones, c, c, zeros_p], axis=1) * scale
    sin = jnp.concatenate([zeros_n, -s, s, zeros_p], axis=1) * scale
    return cos, sin


def _ret_tables(seq, scale):
    c, s = _rope_angles(seq, RET_QK_DIM)
    cos = jnp.concatenate([c, c, c, c], axis=1) * scale
    sin = jnp.concatenate([-s, s, -s, s], axis=1) * scale
    return cos, sin


def _prepare_weights(norm_mix_g, w_in, q_norm_g, w_uq, kv_norm_g, w_ukv, ret_gn_g, w_branch_a, w_branch_b, w_out,
                     norm_ffn_g, w_router, norm_final_g):
    offs, acc = [], 0
    for width in IN_SPLITS:
        offs.append((acc, acc + width))
        acc += width
    w_cq, w_ckv, w_kr, w_rq, w_rk, w_rv, w_rg, w_ga, w_gb = [w_in[:, a:b].astype(BF16) for a, b in offs]
    w_kr = jnp.pad(w_kr, ((0, 0), (0, LANES - MLA_ROPE_DIM)))

    uq = w_uq.reshape(Q_LORA_RANK, MLA_HEADS, MLA_QK_DIM)
    wq_pad = jnp.pad(uq, ((0, 0), (0, 0), (0, HEAD_PAD - MLA_QK_DIM))).reshape(Q_LORA_RANK, MLA_PAD_WIDTH)
    ukv = w_ukv.reshape(KV_LORA_RANK, MLA_HEADS, MLA_NOPE_DIM + MLA_V_DIM)
    wk_pad = jnp.pad(ukv[:, :, :MLA_NOPE_DIM], ((0, 0), (0, 0), (0, HEAD_PAD - MLA_NOPE_DIM)))
    wk_pad = wk_pad.reshape(KV_LORA_RANK, MLA_PAD_WIDTH)
    wv = jnp.pad(ukv[:, :, MLA_NOPE_DIM:], ((0, 0), (0, 0), (0, HEAD_PAD - MLA_V_DIM)))
    wv = wv.reshape(KV_LORA_RANK, MLA_PAD_WIDTH).T
    src = jnp.arange(LANES)[:, None]
    dst = jnp.arange(MLA_PAD_WIDTH)[None, :]
    place = ((dst % HEAD_PAD) - MLA_NOPE_DIM == src) & (src < MLA_ROPE_DIM)

    pre_w = (norm_mix_g.reshape(1, -1), w_cq, w_ckv, w_kr, w_rq, w_rk, w_rv, w_rg, w_ga, w_gb,
             q_norm_g.reshape(1, -1), wq_pad.astype(BF16), kv_norm_g.reshape(1, -1), wk_pad.astype(BF16),
             place.astype(BF16), wv.astype(BF16))

    wr_hi = w_router.astype(BF16)
    wr_lo = (w_router - wr_hi.astype(F32)).astype(BF16)
    pad_to = lambda w: jnp.pad(w, ((0, 0), (0, LANES - w.shape[1])))
    post_w = (w_branch_a.astype(BF16), w_branch_b.astype(BF16), w_out.astype(BF16), norm_ffn_g.reshape(1, -1),
              pad_to(jnp.concatenate([wr_hi, wr_lo], axis=1)), pad_to(wr_hi))
    return pre_w, post_w, ret_gn_g.reshape(1, -1), norm_final_g.reshape(1, -1)


def _encode(x, pre_w, post_w, log_gamma, gn_g, final_g, wg, wu, wd):
    batch, seq, d = x.shape
    n = batch * seq
    x2d = x.reshape(n, d)
    tables = (*_mla_tables(seq, MLA_QK_DIM ** -0.5 * LOG2_E), *_mla_tables(seq, 1.0),
              *_ret_tables(seq, 1.0), *_ret_tables(seq, RET_QK_DIM ** -0.5))
    qcat, kcat, vt, rq, rk, rv, rg, ga, gb = _pre_call(x2d, seq, pre_w, tables)
    o = _attn_call(qcat, kcat, vt, batch, seq)
    r = _ret_call(log_gamma, rq, rk, rv, rg, gn_g, batch, seq)
    x2, xn, aff, afft = _post_call(o, r, ga, gb, x2d, post_w)

    cap = CAPACITY_FACTOR * n // N_EXPERTS
    code, starts = _route_call(afft, cap)
    xe = _dispatch_call(starts, code, xn, cap)
    ye = _ffn_call(xe, wg, wu, wd, cap)
    return _combine_call(starts, x2, aff, code, final_g, ye, cap).reshape(batch, seq, d)


def kernel(x_prompt, x_sample, norm_mix_g, w_in, q_norm_g, w_uq, kv_norm_g, w_ukv, ret_decay_fwd, ret_decay_bwd,
           ret_gn_g, w_branch_a, w_branch_b, w_out, norm_ffn_g, w_router, w_exp_gate, w_exp_up, w_exp_down,
           norm_final_g):
    assert norm_mix_g.shape[0] == 1, "single-layer trunk"
    pre_w, post_w, gn_g, final_g = _prepare_weights(
        norm_mix_g[0], w_in[0], q_norm_g[0], w_uq[0], kv_norm_g[0], w_ukv[0], ret_gn_g[0], w_branch_a[0],
        w_branch_b[0], w_out[0], norm_ffn_g[0], w_router[0], norm_final_g)
    log_gamma = jnp.stack([jax.nn.log_sigmoid(ret_decay_fwd[0].astype(F32)),
                           jax.nn.log_sigmoid(ret_decay_bwd[0].astype(F32))])
    wg = w_exp_gate[0].astype(BF16)
    wu = w_exp_up[0].astype(BF16)
    wd = w_exp_down[0].astype(BF16)
    enc = functools.partial(_encode, pre_w=pre_w, post_w=post_w, log_gamma=log_gamma, gn_g=gn_g, final_g=final_g,
                            wg=wg, wu=wu, wd=wd)
    return enc(x_prompt), enc(x_sample)
```
